```python
import math
import jax
import jax.numpy as jnp
from jax import lax
import numpy as np

D_MODEL = 1024
BATCH = 2
SEQ = 8192
DEPTH = 2

GRID_W = 64
CTX_LEN = 256
EPS = 1e-6
ADA_CHUNKS = 6

MLA_HEADS = 4
MLA_NOPE = 64
MLA_ROPE = 32
MLA_QK = MLA_NOPE + MLA_ROPE
MLA_V = 64
MLA_Q_LORA = 192
MLA_KV_LORA = 128
ROPE_THETA = 10000.0
Q_BLOCK = 128

GDN_HEADS = 4
GDN_DK = 128
GDN_DV = 128
GDN_CONV = 3
GDN_CHUNK = 64

POOL_WINDOWS = (2, 4, 8, 16)
POOL_GROUP = 64
POOL_WIDTH = POOL_GROUP * len(POOL_WINDOWS)

D_MIX = MLA_HEADS * MLA_V + GDN_HEADS * GDN_DV + POOL_WIDTH

N_EXPERTS = 16
EC_CAPACITY_FACTOR = 2
D_EXPERT = 512

IN_SPLITS = (MLA_Q_LORA, MLA_KV_LORA, MLA_ROPE,
             GDN_HEADS * GDN_DK, GDN_HEADS * GDN_DK, GDN_HEADS * GDN_DV, GDN_HEADS * GDN_DV,
             2 * GDN_HEADS, 2 * GDN_HEADS, POOL_WIDTH)
IN_COLS = sum(IN_SPLITS)

kernel_name = "hybrid_mla_gdn_pool_ecmoe_dit"


def rms_norm(x, g):
    xf = x.astype(jnp.float32)
    y = xf * lax.rsqrt(jnp.mean(xf * xf, axis=-1, keepdims=True) + EPS)
    return (y * g.astype(jnp.float32)).astype(x.dtype)


def l2_norm(x):
    xf = x.astype(jnp.float32)
    return (xf * lax.rsqrt(jnp.sum(xf * xf, axis=-1, keepdims=True) + EPS)).astype(x.dtype)


def modulate(x, g, shift, scale):
    return rms_norm(x, g) * (1.0 + scale) + shift


def split_cols(p):
    offsets = [int(o) for o in np.cumsum(IN_SPLITS)[:-1]]
    return jnp.split(p, offsets, axis=-1)


def axial_rope_tables(n_tokens):
    rows = n_tokens // GRID_W
    row = jnp.repeat(jnp.arange(rows, dtype=jnp.float32), GRID_W)
    col = jnp.tile(jnp.arange(GRID_W, dtype=jnp.float32), rows)
    n_freq = MLA_ROPE // 4
    inv_freq = ROPE_THETA ** (-jnp.arange(n_freq, dtype=jnp.float32) / n_freq)
    ang_r = row[:, None] * inv_freq
    ang_c = col[:, None] * inv_freq
    return tuple(t[None, :, None, :] for t in (jnp.cos(ang_r), jnp.sin(ang_r), jnp.cos(ang_c), jnp.sin(ang_c)))


def rotate_half(x, cos, sin):
    x1, x2 = jnp.split(x, 2, axis=-1)
    return jnp.concatenate([x1 * cos - x2 * sin, x2 * cos + x1 * sin], axis=-1)


def with_rope(x, rope_tabs):
    if rope_tabs is None:
        return x
    cr, sr, cc, sc = rope_tabs
    xr, xcol = jnp.split(x[..., MLA_NOPE:].astype(jnp.float32), 2, axis=-1)
    rot = jnp.concatenate([rotate_half(xr, cr, sr), rotate_half(xcol, cc, sc)], axis=-1)
    return jnp.concatenate([x[..., :MLA_NOPE], rot.astype(x.dtype)], axis=-1)


def mla_queries(p_q, q_a_norm, w_uq, q_norm, rope_tabs):
    B, T, _ = p_q.shape
    q = (rms_norm(p_q, q_a_norm) @ w_uq).reshape(B, T, MLA_HEADS, MLA_QK)
    return with_rope(rms_norm(q, q_norm), rope_tabs)


def mla_keys_values(p_kv, p_kr, kv_a_norm, w_ukv, k_norm, rope_tabs):
    B, T, _ = p_kv.shape
    kv = (rms_norm(p_kv, kv_a_norm) @ w_ukv).reshape(B, T, MLA_HEADS, MLA_NOPE + MLA_V)
    k_nope, v = kv[..., :MLA_NOPE], kv[..., MLA_NOPE:]
    k_rope = jnp.broadcast_to(p_kr[:, :, None, :], (B, T, MLA_HEADS, MLA_ROPE))
    k = rms_norm(jnp.concatenate([k_nope, k_rope], axis=-1), k_norm)
    return with_rope(k, rope_tabs), v


def softmax_attend(q, k, v):
    s = jnp.einsum('bqhd,bkhd->bhqk', q, k, preferred_element_type=jnp.float32) * (MLA_QK ** -0.5)
    p = jax.nn.softmax(s, axis=-1).astype(v.dtype)
    return jnp.einsum('bhqk,bkhd->bqhd', p, v)


def blocked_attend(q, k, v):
    B, T, H, d = q.shape
    nb = T // Q_BLOCK
    qb = jnp.moveaxis(q.reshape(B, nb, Q_BLOCK, H, d), 1, 0)
    out = lax.map(lambda qi: softmax_attend(qi, k, v), qb)
    return jnp.moveaxis(out, 0, 1).reshape(B, T, H, v.shape[-1])


def short_conv(u, w):
    pad = GDN_CONV // 2
    return lax.conv_general_dilated(u, w[:, None, :].astype(u.dtype), window_strides=(1,),
                                    padding=((pad, pad),), dimension_numbers=('NWC', 'WIO', 'NWC'),
                                    feature_group_count=u.shape[-1])


def gdn_inputs(pq, pk, pv, pa, pb, conv_w, a_log, dt_bias):
    B, T, _ = pq.shape
    qkv = jax.nn.silu(short_conv(jnp.concatenate([pq, pk, pv], axis=-1), conv_w))
    q, k, v = jnp.split(qkv, [GDN_HEADS * GDN_DK, 2 * GDN_HEADS * GDN_DK], axis=-1)
    q = l2_norm(q.reshape(B, T, GDN_HEADS, GDN_DK)) * (GDN_DK ** -0.5)
    k = l2_norm(k.reshape(B, T, GDN_HEADS, GDN_DK))
    v = v.reshape(B, T, GDN_HEADS, GDN_DV)
    pa = pa.reshape(B, T, 2, GDN_HEADS).astype(jnp.float32)
    pb = pb.reshape(B, T, 2, GDN_HEADS).astype(jnp.float32)
    g = -jnp.exp(a_log.astype(jnp.float32)) * jax.nn.softplus(pa + dt_bias.astype(jnp.float32))
    beta = jax.nn.sigmoid(pb)
    return q, k, v, g, beta


def gated_delta_chunked(q, k, v, g, beta, s0):
    B, T, H, _ = q.shape
    dv = v.shape[-1]
    C = GDN_CHUNK
    n = T // C
    f32 = jnp.float32

    def chunks(a):
        a = a.astype(f32).reshape((B, n, C, H) + a.shape[3:])
        return jnp.moveaxis(a, (1, 3), (0, 2))

    qc, kc, vc, bc = chunks(q), chunks(k), chunks(v), chunks(beta)
    gc = jnp.cumsum(chunks(g), axis=-1)
    idx = jnp.arange(C)
    incl = idx[:, None] >= idx[None, :]
    strict = idx[:, None] > idx[None, :]
    decay = jnp.exp(jnp.where(incl, gc[..., :, None] - gc[..., None, :], -jnp.inf))
    kb = kc * bc[..., None]
    a_low = jnp.where(strict, jnp.einsum('nbhcd,nbhsd->nbhcs', kb, kc) * decay, 0.0)
    tmat = a_low + jnp.eye(C, dtype=f32)
    w = lax.linalg.triangular_solve(tmat, kb * jnp.exp(gc)[..., None], left_side=True, lower=True, unit_diagonal=True)
    u = lax.linalg.triangular_solve(tmat, vc * bc[..., None], left_side=True, lower=True, unit_diagonal=True)
    qk = jnp.einsum('nbhcd,nbhsd->nbhcs', qc, kc) * decay
    qg = qc * jnp.exp(gc)[..., None]
    g_last = gc[..., -1]
    kd = kc * jnp.exp(g_last[..., None] - gc)[..., None]

    def step(s, inp):
        w_i, u_i, qk_i, qg_i, kd_i, gl_i = inp
        v_new = u_i - jnp.einsum('bhcd,bhde->bhce', w_i, s)
        o = jnp.einsum('bhcd,bhde->bhce', qg_i, s) + jnp.einsum('bhcs,bhse->bhce', qk_i, v_new)
        s = s * jnp.exp(gl_i)[..., None, None] + jnp.einsum('bhcd,bhce->bhde', kd_i, v_new)
        return s, o

    s_final, o = lax.scan(step, s0.astype(f32), (w, u, qk, qg, kd, g_last))
    o = jnp.moveaxis(o, (0, 2), (1, 3)).reshape(B, T, H, dv)
    return o.astype(v.dtype), s_final


def gdn_bidirectional(c_in, l_in):
    qc, kc, vc, gc, bc = c_in
    ql, kl, vl, gl, bl = l_in
    s0 = jnp.zeros((qc.shape[0], GDN_HEADS, GDN_DK, GDN_DV), jnp.float32)
    rev = lambda a: jnp.flip(a, axis=1)
    o_cf, s_cf = gated_delta_chunked(qc, kc, vc, gc[:, :, 0], bc[:, :, 0], s0)
    o_lf, _ = gated_delta_chunked(ql, kl, vl, gl[:, :, 0], bl[:, :, 0], s_cf)
    o_cb, s_cb = gated_delta_chunked(rev(qc), rev(kc), rev(vc), rev(gc[:, :, 1]), rev(bc[:, :, 1]), s0)
    o_lb, _ = gated_delta_chunked(rev(ql), rev(kl), rev(vl), rev(gl[:, :, 1]), rev(bl[:, :, 1]), s_cb)
    return o_cf + rev(o_cb), o_lf + rev(o_lb)


def gdn_output(o, z, norm_g):
    B, T = z.shape[:2]
    zh = z.reshape(B, T, GDN_HEADS, GDN_DV)
    return (rms_norm(o, norm_g) * jax.nn.silu(zh)).reshape(B, T, GDN_HEADS * GDN_DV)


def pool_mixer(u, pool_w, pool_scale):
    B, T, _ = u.shape
    uf = u.astype(jnp.float32)
    csum = jnp.concatenate([jnp.zeros((B, 1, POOL_WIDTH), jnp.float32), jnp.cumsum(uf, axis=1)], axis=1)
    t = jnp.arange(T)
    diffs = []
    for gi, win in enumerate(POOL_WINDOWS):
        lo = jnp.clip(t - win // 2, 0, T)
        hi = jnp.clip(t - win // 2 + win, 0, T)
        sl = slice(gi * POOL_GROUP, (gi + 1) * POOL_GROUP)
        mean = (csum[:, hi, sl] - csum[:, lo, sl]) / (hi - lo).astype(jnp.float32)[None, :, None]
        diffs.append(mean - uf[..., sl])
    d = jnp.stack(diffs, axis=2).astype(u.dtype)
    y = jnp.einsum('btgc,gcd->btgd', d, pool_w).reshape(B, T, POOL_WIDTH)
    return y * pool_scale


def expert_choice_ffn(h, w_router, w_gate, w_up, w_down):
    B, T, D = h.shape
    cap = EC_CAPACITY_FACTOR * T // N_EXPERTS
    aff = jax.nn.softmax(jnp.einsum('btd,de->bte', h, w_router, preferred_element_type=jnp.float32), axis=-1)
    gate, idx = lax.top_k(jnp.swapaxes(aff, 1, 2), cap)
    xs = jax.vmap(lambda hb, ib: hb[ib])(h, idx)
    hid = jax.nn.silu(jnp.einsum('becd,edf->becf', xs, w_gate)) * jnp.einsum('becd,edf->becf', xs, w_up)
    y = jnp.einsum('becf,efd->becd', hid, w_down) * gate.astype(h.dtype)[..., None]
    return jax.vmap(lambda yb, ib: jnp.zeros((T, D), h.dtype).at[ib.reshape(-1)].add(yb.reshape(-1, D)))(y, idx)


def mixer_sublayer(h_c, h_l, rope_tabs, need_ctx, w_in, q_a_norm, w_uq, kv_a_norm, w_ukv, q_norm, k_norm,
                   conv_w, a_log, dt_bias, gdn_norm_g, pool_w, pool_scale, w_out):
    B, T, _ = h_l.shape
    pc = split_cols(h_c @ w_in)
    pl = split_cols(h_l @ w_in)
    k_c, v_c = mla_keys_values(pc[1], pc[2], kv_a_norm, w_ukv, k_norm, None)
    k_l, v_l = mla_keys_values(pl[1], pl[2], kv_a_norm, w_ukv, k_norm, rope_tabs)
    q_l = mla_queries(pl[0], q_a_norm, w_uq, q_norm, rope_tabs)
    att_l = blocked_attend(q_l, jnp.concatenate([k_c, k_l], axis=1),
                           jnp.concatenate([v_c, v_l], axis=1)).reshape(B, T, MLA_HEADS * MLA_V)
    c_in = gdn_inputs(pc[3], pc[4], pc[5], pc[7], pc[8], conv_w, a_log, dt_bias)
    l_in = gdn_inputs(pl[3], pl[4], pl[5], pl[7], pl[8], conv_w, a_log, dt_bias)
    o_c, o_l = gdn_bidirectional(c_in, l_in)
    gdn_l = gdn_output(o_l, pl[6], gdn_norm_g)
    pool_l = pool_mixer(pl[9], pool_w, pool_scale)
    y_l = jnp.concatenate([att_l, gdn_l, pool_l], axis=-1) @ w_out
    if not need_ctx:
        return None, y_l
    q_c = mla_queries(pc[0], q_a_norm, w_uq, q_norm, None)
    att_c = softmax_attend(q_c, k_c, v_c).reshape(B, -1, MLA_HEADS * MLA_V)
    gdn_c = gdn_output(o_c, pc[6], gdn_norm_g)
    pool_c = pool_mixer(pc[9], pool_w, pool_scale)
    y_c = jnp.concatenate([att_c, gdn_c, pool_c], axis=-1) @ w_out
    return y_c, y_l


def setup_inputs(seed: int = 0) -> dict:
    key = jax.random.key(seed)
    keys = jax.random.split(key, 32)
    f32 = jnp.float32
    L, D = DEPTH, D_MODEL

    def nrm(i, shape, scale):
        return jax.random.normal(keys[i], shape, f32) * scale

    dt = jnp.exp(jax.random.uniform(keys[17], (L, 2, GDN_HEADS), f32, math.log(1e-3), math.log(1e-1)))
    return {
        'x': nrm(0, (BATCH, SEQ, D), 1.0),
        'c': nrm(1, (BATCH, D), 1.0),
        'ctx': nrm(2, (BATCH, CTX_LEN, D), 1.0),
        'c_ctx': nrm(3, (D,), 1.0),
        'ada_w': nrm(4, (L, D, ADA_CHUNKS * D), 0.5 * D ** -0.5),
        'ada_b': nrm(5, (L, ADA_CHUNKS * D), 0.02),
        'norm1_g': 1.0 + nrm(6, (L, D), 0.05),
        'norm2_g': 1.0 + nrm(7, (L, D), 0.05),
        'w_in': nrm(8, (L, D, IN_COLS), D ** -0.5),
        'mla_q_a_norm': 1.0 + nrm(9, (L, MLA_Q_LORA), 0.05),
        'mla_w_uq': nrm(10, (L, MLA_Q_LORA, MLA_HEADS * MLA_QK), MLA_Q_LORA ** -0.5),
        'mla_kv_a_norm': 1.0 + nrm(11, (L, MLA_KV_LORA), 0.05),
        'mla_w_ukv': nrm(12, (L, MLA_KV_LORA, MLA_HEADS * (MLA_NOPE + MLA_V)), MLA_KV_LORA ** -0.5),
        'mla_q_norm': 1.0 + nrm(13, (L, MLA_QK), 0.05),
        'mla_k_norm': 1.0 + nrm(14, (L, MLA_QK), 0.05),
        'gdn_conv_w': nrm(15, (L, GDN_CONV, 2 * GDN_HEADS * GDN_DK + GDN_HEADS * GDN_DV), GDN_CONV ** -0.5),
        'gdn_a_log': jnp.log(jax.random.uniform(keys[16], (L, 2, GDN_HEADS), f32, 1.0, 16.0)),
        'gdn_dt_bias': dt + jnp.log(-jnp.expm1(-dt)),
        'gdn_norm_g': 1.0 + nrm(18, (L, GDN_DV), 0.05),
        'pool_w': nrm(19, (L, len(POOL_WINDOWS), POOL_GROUP, POOL_GROUP), POOL_GROUP ** -0.5),
        'pool_scale': 1.0 + nrm(20, (L, POOL_WIDTH), 0.1),
        'w_out': nrm(21, (L, D_MIX, D), D_MIX ** -0.5),
        'moe_router': nrm(22, (L, D, N_EXPERTS), D ** -0.5),
        'moe_w_gate': nrm(23, (L, N_EXPERTS, D, D_EXPERT), D ** -0.5),
        'moe_w_up': nrm(24, (L, N_EXPERTS, D, D_EXPERT), D ** -0.5),
        'moe_w_down': nrm(25, (L, N_EXPERTS, D_EXPERT, D), D_EXPERT ** -0.5),
    }


def reference(x, c, ctx, c_ctx, ada_w, ada_b, norm1_g, norm2_g, w_in, mla_q_a_norm, mla_w_uq, mla_kv_a_norm,
              mla_w_ukv, mla_q_norm, mla_k_norm, gdn_conv_w, gdn_a_log, gdn_dt_bias, gdn_norm_g, pool_w,
              pool_scale, w_out, moe_router, moe_w_gate, moe_w_up, moe_w_down):
    rope_tabs = axial_rope_tables(x.shape[1])
    xc = ctx
    for l in range(DEPTH):
        need_ctx = l < DEPTH - 1
        m = jnp.split(jax.nn.silu(c) @ ada_w[l] + ada_b[l], ADA_CHUNKS, axis=-1)
        mc = jnp.split(jax.nn.silu(c_ctx) @ ada_w[l] + ada_b[l], ADA_CHUNKS, axis=-1)
        sh1, sc1, g1, sh2, sc2, g2 = [t[:, None, :] for t in m]
        csh1, csc1, cg1, csh2, csc2, cg2 = mc
        h_l = modulate(x, norm1_g[l], sh1, sc1)
        h_c = modulate(xc, norm1_g[l], csh1, csc1)
        y_c, y_l = mixer_sublayer(h_c, h_l, rope_tabs, need_ctx, w_in[l], mla_q_a_norm[l], mla_w_uq[l],
                                  mla_kv_a_norm[l], mla_w_ukv[l], mla_q_norm[l], mla_k_norm[l], gdn_conv_w[l],
                                  gdn_a_log[l], gdn_dt_bias[l], gdn_norm_g[l], pool_w[l], pool_scale[l], w_out[l])
        x = x + g1 * y_l
        x = x + g2 * expert_choice_ffn(modulate(x, norm2_g[l], sh2, sc2), moe_router[l], moe_w_gate[l],
                                       moe_w_up[l], moe_w_down[l])
        if need_ctx:
            xc = xc + cg1 * y_c
            xc = xc + cg2 * expert_choice_ffn(modulate(xc, norm2_g[l], csh2, csc2), moe_router[l],
                                              moe_w_gate[l], moe_w_up[l], moe_w_down[l])
    return x
```

```python
import functools
import math

import numpy as np
import jax
import jax.numpy as jnp
from jax import lax
from jax.experimental import pallas as pl
from jax.experimental.pallas import tpu as pltpu

F32 = jnp.float32
BF16 = jnp.bfloat16
HI = lax.Precision.HIGHEST

EPS = 1e-6
GRID_W = 64
ADA_CHUNKS = 6
MLA_HEADS = 4
MLA_NOPE = 64
MLA_ROPE = 32
MLA_QK = MLA_NOPE + MLA_ROPE
MLA_V = 64
MLA_Q_LORA = 192
MLA_KV_LORA = 128
ROPE_THETA = 10000.0
GDN_HEADS = 4
GDN_DK = 128
GDN_DV = 128
GDN_CHUNK = 64
POOL_WINDOWS = (2, 4, 8, 16)
POOL_GROUP = 64
POOL_WIDTH = POOL_GROUP * len(POOL_WINDOWS)
N_EXPERTS = 16
EC_CAPACITY_FACTOR = 2

LANE = 128
SUBLANE = 8
HEAD_PAD = 128
VMEM_LIMIT = 48 * 1024 * 1024

NT = (((1,), (1,)), ((), ()))
TN = (((0,), (0,)), ((), ()))

SEG = {}
_off = 0
for _name, _w in (("pq", 256), ("pkv", 128), ("pkr", 128), ("gq", 512), ("gk", 512), ("gv", 512),
                  ("gz", 512), ("gab", 256), ("pool", 256)):
    SEG[_name] = (_off, _w)
    _off += _w
IN_PAD = _off


def _cp(*dims):
    return pltpu.CompilerParams(dimension_semantics=dims, vmem_limit_bytes=VMEM_LIMIT)


def _silu(v):
    return v / (1.0 + jnp.exp(-v))


def _ada_kernel(c_ref, w_ref, b_ref, o_ref):
    s = _silu(c_ref[...])
    o_ref[...] = jnp.dot(s, w_ref[...], precision=HI, preferred_element_type=F32) + b_ref[...]


def ada_mod(cvec, ada_w, ada_b):
    L, D, N = ada_w.shape
    tn = N // 4
    return pl.pallas_call(
        _ada_kernel, grid=(L, N // tn),
        in_specs=[pl.BlockSpec((SUBLANE, D), lambda l, j: (0, 0)),
                  pl.BlockSpec((None, D, tn), lambda l, j: (l, 0, j)),
                  pl.BlockSpec((None, 1, tn), lambda l, j: (l, 0, j))],
        out_specs=pl.BlockSpec((None, SUBLANE, tn), lambda l, j: (l, 0, j)),
        out_shape=jax.ShapeDtypeStruct((L, SUBLANE, N), F32),
        compiler_params=_cp("parallel", "parallel"), name="ada_mod",
    )(cvec, ada_w, ada_b.reshape(L, 1, N))


def _inproj_kernel(x_ref, sh_ref, sc_ref, g_ref, w_ref, *out_refs):
    x = x_ref[...]
    h = x * lax.rsqrt(jnp.mean(x * x, axis=-1, keepdims=True) + EPS) * g_ref[...]
    hb = (h * (1.0 + sc_ref[...]) + sh_ref[...]).astype(BF16)
    for (off, n), o_ref in zip(SEG.values(), out_refs):
        o_ref[...] = jnp.dot(hb, w_ref[:, off:off + n], preferred_element_type=F32)


def inproj(x, mod, norm_g, w_in_p):
    B, T, D = x.shape
    tm = min(256, T)
    modspec = lambda k: pl.BlockSpec((None, 1, D), lambda b, i, k=k: (b, 0, k))
    return pl.pallas_call(
        _inproj_kernel, grid=(B, T // tm),
        in_specs=[pl.BlockSpec((None, tm, D), lambda b, i: (b, i, 0)), modspec(0), modspec(1),
                  pl.BlockSpec((1, D), lambda b, i: (0, 0)),
                  pl.BlockSpec((D, IN_PAD), lambda b, i: (0, 0))],
        out_specs=[pl.BlockSpec((None, tm, n), lambda b, i: (b, i, 0)) for _, n in SEG.values()],
        out_shape=[jax.ShapeDtypeStruct((B, T, n), F32) for _, n in SEG.values()],
        compiler_params=_cp("parallel", "parallel"), name="inproj",
    )(x, mod, mod, norm_g.reshape(1, D), w_in_p)


def _prep_kernel(pq_ref, pkv_ref, pkr_ref, gq_ref, gk_ref, gv_ref, gqp_ref, gkp_ref, gvp_ref,
                 gqn_ref, gkn_ref, gvn_ref, gab_ref, qan_ref, wuq_ref, kvan_ref, wuk_ref, wuv_ref,
                 qn_ref, kn_ref, cos_ref, sa_ref, sb_ref, cw_ref, alog_ref, dt_ref,
                 Q_ref, K_ref, V_ref, q_ref, k_ref, v_ref, gb_ref, *, nt, tm):
    i = pl.program_id(1)
    cos, sa, sb = cos_ref[...], sa_ref[...], sb_ref[...]

    def rope(xh):
        return xh * cos + pltpu.roll(xh, LANE - 8, 1) * sa + pltpu.roll(xh, 8, 1) * sb

    pq = pq_ref[...]
    qa = pq * lax.rsqrt(jnp.sum(pq * pq, axis=-1, keepdims=True) * (1.0 / MLA_Q_LORA) + EPS) * qan_ref[...]
    qall = jnp.dot(qa.astype(BF16), wuq_ref[...], preferred_element_type=F32)
    pkv = pkv_ref[...]
    kva = (pkv * lax.rsqrt(jnp.mean(pkv * pkv, axis=-1, keepdims=True) + EPS) * kvan_ref[...]).astype(BF16)
    kall = jnp.dot(kva, wuk_ref[...], preferred_element_type=F32)
    V_ref[...] = jnp.dot(kva, wuv_ref[...], preferred_element_type=F32).astype(BF16)
    pkr = pkr_ref[...]
    for h in range(MLA_HEADS):
        sl = slice(h * HEAD_PAD, (h + 1) * HEAD_PAD)
        qh = qall[:, sl]
        qh = qh * lax.rsqrt(jnp.sum(qh * qh, axis=-1, keepdims=True) * (1.0 / MLA_QK) + EPS) * qn_ref[...]
        Q_ref[:, sl] = rope(qh).astype(BF16)
        kh = kall[:, sl] + pkr
        kh = kh * lax.rsqrt(jnp.sum(kh * kh, axis=-1, keepdims=True) * (1.0 / MLA_QK) + EPS) * kn_ref[...]
        K_ref[:, sl] = rope(kh).astype(BF16)

    rid = lax.broadcasted_iota(jnp.int32, (tm, GDN_HEADS * GDN_DK), 0)

    def conv_silu(u_ref, up_ref, un_ref, c0):
        u = u_ref[...]
        n = u.shape[1]
        prev_row = jnp.where(i > 0, up_ref[SUBLANE - 1:SUBLANE, :], 0.0)
        next_row = jnp.where(i < nt - 1, un_ref[0:1, :], 0.0)
        um = jnp.where(rid == 0, prev_row, pltpu.roll(u, 1, 0))
        up = jnp.where(rid == tm - 1, next_row, pltpu.roll(u, tm - 1, 0))
        y = um * cw_ref[0:1, c0:c0 + n] + u * cw_ref[1:2, c0:c0 + n] + up * cw_ref[2:3, c0:c0 + n]
        return _silu(y)

    cq = conv_silu(gq_ref, gqp_ref, gqn_ref, 0)
    ck = conv_silu(gk_ref, gkp_ref, gkn_ref, GDN_HEADS * GDN_DK)
    v_ref[...] = conv_silu(gv_ref, gvp_ref, gvn_ref, 2 * GDN_HEADS * GDN_DK)
    for h in range(GDN_HEADS):
        sl = slice(h * GDN_DK, (h + 1) * GDN_DK)
        qh = cq[:, sl]
        q_ref[:, sl] = qh * lax.rsqrt(jnp.sum(qh * qh, axis=-1, keepdims=True) + EPS) * (GDN_DK ** -0.5)
        kh = ck[:, sl]
        k_ref[:, sl] = kh * lax.rsqrt(jnp.sum(kh * kh, axis=-1, keepdims=True) + EPS)

    pre = gab_ref[...]
    lane = lax.broadcasted_iota(jnp.int32, pre.shape, 1) % LANE
    sp_in = pre + dt_ref[...]
    softplus = jnp.maximum(sp_in, 0.0) + jnp.log(1.0 + jnp.exp(-jnp.abs(sp_in)))
    g = -jnp.exp(alog_ref[...]) * softplus
    beta = 1.0 / (1.0 + jnp.exp(-pre))
    gb_ref[...] = jnp.where(lane < GDN_HEADS, g, jnp.where(lane < 2 * GDN_HEADS, beta, 0.0))


def prep(p, wts, rope_tabs):
    pq, pkv, pkr, gq, gk, gv, gab = (p[k] for k in ("pq", "pkv", "pkr", "gq", "gk", "gv", "gab"))
    B, T, _ = pq.shape
    tm = min(256, T)
    nt = T // tm
    tb = tm // SUBLANE
    nb = T // SUBLANE
    cur = lambda n: pl.BlockSpec((None, tm, n), lambda b, i: (b, i, 0))
    prv = lambda n: pl.BlockSpec((None, SUBLANE, n), lambda b, i: (b, jnp.maximum(i * tb - 1, 0), 0))
    nxt = lambda n: pl.BlockSpec((None, SUBLANE, n), lambda b, i: (b, jnp.minimum((i + 1) * tb, nb - 1), 0))
    full = lambda a: pl.BlockSpec(a.shape, lambda b, i: (0,) * a.ndim)
    tab = pl.BlockSpec((tm, LANE), lambda b, i: (i, 0))
    W = GDN_HEADS * GDN_DK
    outs = [("Q", MLA_HEADS * HEAD_PAD, BF16), ("K", MLA_HEADS * HEAD_PAD, BF16), ("V", MLA_HEADS * HEAD_PAD, BF16),
            ("q", W, F32), ("k", W, F32), ("v", W, F32), ("gb", 2 * LANE, F32)]
    res = pl.pallas_call(
        functools.partial(_prep_kernel, nt=nt, tm=tm), grid=(B, nt),
        in_specs=[cur(256), cur(128), cur(128), cur(W), cur(W), cur(W), prv(W), prv(W), prv(W),
                  nxt(W), nxt(W), nxt(W), cur(256)] + [full(a) for a in wts[:7]] + [tab, tab, tab]
                 + [full(a) for a in wts[7:]],
        out_specs=[cur(n) for _, n, _ in outs],
        out_shape=[jax.ShapeDtypeStruct((B, T, n), dt) for _, n, dt in outs],
        compiler_params=_cp("parallel", "parallel"), name="prep",
    )(pq, pkv, pkr, gq, gk, gv, gq, gk, gv, gq, gk, gv, gab, *wts[:7], *rope_tabs, *wts[7:])
    return dict(zip([n for n, _, _ in outs], res))


def _attn_kernel(q_ref, k_ref, v_ref, o_ref, *, ck, nk):
    q = q_ref[...]
    tq = q.shape[0]

    def body(j, carry):
        m, l, acc = carry
        off = pl.multiple_of(j * ck, ck)
        kc = k_ref[pl.ds(off, ck), :]
        vc = v_ref[pl.ds(off, ck), :]
        s = lax.dot_general(q, kc, NT, preferred_element_type=F32)
        m_new = jnp.maximum(m, jnp.max(s, axis=-1, keepdims=True))
        p = jnp.exp(s - m_new)
        a = jnp.exp(m - m_new)
        l = a * l + jnp.sum(p, axis=-1, keepdims=True)
        acc = a * acc + jnp.dot(p.astype(BF16), vc, preferred_element_type=F32)
        return m_new, l, acc

    init = (jnp.full((tq, 1), -1e30, F32), jnp.zeros((tq, 1), F32), jnp.zeros((tq, HEAD_PAD), F32))
    _, l, acc = lax.fori_loop(0, nk, body, init)
    o_ref[...] = (acc / l).astype(o_ref.dtype)


def attention(Q, K, V):
    B, Tq, _ = Q.shape
    Tk = K.shape[1]
    tq = min(512, Tq)
    ck = next(c for c in (768, 512, 384, 256, 128) if Tk % c == 0)
    kv = pl.BlockSpec((None, Tk, HEAD_PAD), lambda b, h, i: (b, 0, h))
    qo = pl.BlockSpec((None, tq, HEAD_PAD), lambda b, h, i: (b, i, h))
    return pl.pallas_call(
        functools.partial(_attn_kernel, ck=ck, nk=Tk // ck), grid=(B, MLA_HEADS, Tq // tq),
        in_specs=[qo, kv, kv], out_specs=qo,
        out_shape=jax.ShapeDtypeStruct(Q.shape, BF16),
        compiler_params=_cp("parallel", "parallel", "parallel"), name="attention",
    )(Q, K, V)


def _gdn_kernel(q_ref, k_ref, v_ref, gb_ref, o_ref, s_ref, *, C):
    d = pl.program_id(1)

    @pl.when(pl.program_id(2) == 0)
    def _():
        s_ref[...] = jnp.zeros_like(s_ref)

    row = lax.broadcasted_iota(jnp.int32, (C, C), 0)
    col = lax.broadcasted_iota(jnp.int32, (C, C), 1)
    fwd = d == 0
    ahead = jnp.where(fwd, row - col, col - row)
    incl = ahead >= 0
    strict = ahead > 0
    inclf = incl.astype(F32)
    eye = (row == col).astype(F32)
    gb = gb_ref[...]
    gc = jnp.dot(inclf, gb, precision=HI, preferred_element_type=F32)
    gct = lax.dot_general(gb, inclf, (((0,), (1,)), ((), ())), precision=HI,
                          preferred_element_type=F32)
    glast = jnp.where(fwd, gc[C - 1:C, :], gc[0:1, :])
    for h in range(GDN_HEADS):
        sl = slice(h * GDN_DK, (h + 1) * GDN_DK)
        qh, kh, vh = q_ref[:, sl], k_ref[:, sl], v_ref[:, sl]
        gcol = gc[:, h:h + 1]
        bcol = gb[:, GDN_HEADS + h:GDN_HEADS + h + 1]
        decay = jnp.exp(jnp.where(incl, gcol - gct[h:h + 1, :], -1e30))
        kk = lax.dot_general(kh, kh, NT, precision=HI, preferred_element_type=F32)
        bk = -(jnp.where(strict, kk * decay, 0.0) * bcol)
        tinv = eye + bk
        for _ in range(int(math.log2(C)) - 1):
            bk = jnp.dot(bk, bk, precision=HI, preferred_element_type=F32)
            tinv = tinv + jnp.dot(tinv, bk, precision=HI, preferred_element_type=F32)
        eg = jnp.exp(gcol)
        rhs = jnp.concatenate([kh * (bcol * eg), vh * bcol], axis=1)
        wu = jnp.dot(tinv, rhs, precision=HI, preferred_element_type=F32)
        w, u = wu[:, :GDN_DK], wu[:, GDN_DK:]
        S = s_ref[h]
        v_new = u - jnp.dot(w, S, precision=HI, preferred_element_type=F32)
        qk = lax.dot_general(qh, kh, NT, precision=HI, preferred_element_type=F32) * decay
        o_ref[:, sl] = (jnp.dot(qh * eg, S, precision=HI, preferred_element_type=F32)
                        + jnp.dot(qk, v_new, precision=HI, preferred_element_type=F32))
        gl = glast[:, h:h + 1]
        kd = kh * jnp.exp(gl - gcol)
        s_ref[h] = S * jnp.exp(gl) + lax.dot_general(kd, v_new, TN, precision=HI, preferred_element_type=F32)


def gdn_scan(q, k, v, gb, n_ctx):
    B, Tt, W = q.shape
    C = GDN_CHUNK
    n = Tt // C
    ncc = n_ctx // C

    def chunk(d, s):
        return jnp.where(d == 0, s, jnp.where(s < ncc, ncc - 1 - s, n + ncc - 1 - s))

    io = pl.BlockSpec((None, C, W), lambda b, d, s: (b, chunk(d, s), 0))
    return pl.pallas_call(
        functools.partial(_gdn_kernel, C=C), grid=(B, 2, n),
        in_specs=[io, io, io, pl.BlockSpec((None, C, LANE), lambda b, d, s: (b, chunk(d, s), d))],
        out_specs=pl.BlockSpec((None, None, C, W), lambda b, d, s: (b, d, chunk(d, s), 0)),
        out_shape=jax.ShapeDtypeStruct((B, 2, Tt, W), F32),
        scratch_shapes=[pltpu.VMEM((GDN_HEADS, GDN_DK, GDN_DV), F32)],
        compiler_params=_cp("parallel", "arbitrary", "arbitrary"), name="gdn_scan",
    )(q, k, v, gb)


def _mixout_kernel(att_ref, of_ref, ob_ref, z_ref, u_ref, up_ref, un_ref, x_ref, g1_ref, sh2_ref, sc2_ref,
                   gng_ref, wa_ref, wg_ref, wp_ref, wbd_ref, ps_ref, n2g_ref, wr_ref,
                   x1_ref, h2_ref, aff_ref, *, tm, T, nt):
    i = pl.program_id(1)
    o = of_ref[...] + ob_ref[...]
    z = z_ref[...]
    parts = []
    for h in range(GDN_HEADS):
        sl = slice(h * GDN_DV, (h + 1) * GDN_DV)
        oh = o[:, sl]
        oh = oh * lax.rsqrt(jnp.mean(oh * oh, axis=-1, keepdims=True) + EPS) * gng_ref[...]
        parts.append((oh * _silu(z[:, sl])).astype(BF16))
    gdn = jnp.concatenate(parts, axis=1)

    u = u_ref[...]
    halo = SUBLANE
    ext = jnp.concatenate([jnp.where(i > 0, up_ref[...], 0.0), u, jnp.where(i < nt - 1, un_ref[...], 0.0)], axis=0)
    n_ext = tm + 2 * halo
    back = lambda a, s: pltpu.roll(a, s, 0)
    ahead = lambda a, s: pltpu.roll(a, n_ext - s, 0)
    s2 = ext + back(ext, 1)
    s4 = back(s2, 1) + ahead(s2, 1)
    s8 = back(s4, 2) + ahead(s4, 2)
    s16 = back(s8, 4) + ahead(s8, 4)
    t = i * tm + lax.broadcasted_iota(jnp.int32, (tm, 1), 0)
    lane = lax.broadcasted_iota(jnp.int32, (tm, POOL_WIDTH), 1)
    mean = None
    for gi, (win, sw) in reversed(list(enumerate(zip(POOL_WINDOWS, (s2, s4, s8, s16))))):
        lo = jnp.maximum(t - win // 2, 0)
        hi = jnp.minimum(t - win // 2 + win, T)
        m = sw[halo:halo + tm, :] / (hi - lo).astype(F32)
        mean = m if mean is None else jnp.where(lane < (gi + 1) * POOL_GROUP, m, mean)
    yp = jnp.dot((mean - u).astype(BF16), wbd_ref[...], preferred_element_type=F32) * ps_ref[...]

    y = (jnp.dot(att_ref[...], wa_ref[...], preferred_element_type=F32)
         + jnp.dot(gdn, wg_ref[...], preferred_element_type=F32)
         + jnp.dot(yp.astype(BF16), wp_ref[...], preferred_element_type=F32))
    x1 = x_ref[...] + g1_ref[...] * y
    x1_ref[...] = x1
    h2 = x1 * lax.rsqrt(jnp.mean(x1 * x1, axis=-1, keepdims=True) + EPS) * n2g_ref[...]
    h2 = h2 * (1.0 + sc2_ref[...]) + sh2_ref[...]
    h2_ref[...] = h2.astype(BF16)
    lg = lax.dot_general(wr_ref[...], h2, NT, precision=HI, preferred_element_type=F32)
    e = jnp.exp(lg - jnp.max(lg, axis=0, keepdims=True))
    aff_ref[...] = e / jnp.sum(e, axis=0, keepdims=True)


def mixout(att, o, o_off, z, u, x, mod, wts):
    B, T, D = x.shape
    tm = min(256, T)
    nt = T // tm
    tb = tm // SUBLANE
    nb = T // SUBLANE
    assert o_off % tm == 0
    ob = o_off // tm
    W = GDN_HEADS * GDN_DV
    cur = lambda n: pl.BlockSpec((None, tm, n), lambda b, i: (b, i, 0))
    odir = lambda dd: pl.BlockSpec((None, None, tm, W), lambda b, i: (b, dd, i + ob, 0))
    modspec = lambda k: pl.BlockSpec((None, 1, D), lambda b, i: (b, 0, k))
    full = lambda a: pl.BlockSpec(a.shape, lambda b, i: (0,) * a.ndim)
    return pl.pallas_call(
        functools.partial(_mixout_kernel, tm=tm, T=T, nt=nt), grid=(B, nt),
        in_specs=[cur(MLA_HEADS * HEAD_PAD), odir(0), odir(1), cur(W), cur(POOL_WIDTH),
                  pl.BlockSpec((None, SUBLANE, POOL_WIDTH), lambda b, i: (b, jnp.maximum(i * tb - 1, 0), 0)),
                  pl.BlockSpec((None, SUBLANE, POOL_WIDTH), lambda b, i: (b, jnp.minimum((i + 1) * tb, nb - 1), 0)),
                  cur(D), modspec(2), modspec(3), modspec(4)] + [full(a) for a in wts],
        out_specs=[cur(D), cur(D), pl.BlockSpec((None, N_EXPERTS, tm), lambda b, i: (b, 0, i))],
        out_shape=[jax.ShapeDtypeStruct((B, T, D), F32), jax.ShapeDtypeStruct((B, T, D), BF16),
                   jax.ShapeDtypeStruct((B, N_EXPERTS, T), F32)],
        compiler_params=_cp("parallel", "parallel"), name="mixout",
    )(att, o, o, z, u, u, u, x, mod, mod, mod, *wts)


def _route_kernel(aff_ref, gate_ref, *, cap, T):
    aff = aff_ref[...]

    def body(it, res):
        cand = res | jnp.left_shift(jnp.int32(1), 30 - it)
        cnt = jnp.sum((aff >= pltpu.bitcast(cand, F32)).astype(jnp.int32), axis=-1, keepdims=True)
        return jnp.where(cnt >= cap, cand, res)

    bits = lax.fori_loop(0, 31, body, jnp.zeros((N_EXPERTS, 1), jnp.int32))
    thr = pltpu.bitcast(bits, F32)
    above = pltpu.bitcast(bits + 1, F32)
    n_gt = jnp.sum((aff >= above).astype(jnp.int32), axis=-1, keepdims=True)
    need = (cap - n_gt).astype(F32)
    upper = (lax.broadcasted_iota(jnp.int32, (LANE, LANE), 0)
             < lax.broadcasted_iota(jnp.int32, (LANE, LANE), 1)).astype(BF16)
    seen = jnp.zeros((N_EXPERTS, 1), F32)
    for j in range(T // LANE):
        sl = slice(j * LANE, (j + 1) * LANE)
        aj = aff[:, sl]
        eq = jnp.where(aj >= thr, jnp.where(aj < above, 1.0, 0.0), 0.0)
        rank = jnp.dot(eq.astype(BF16), upper, preferred_element_type=F32) + seen
        tie = jnp.where(rank < need, eq, 0.0)
        gate_ref[:, sl] = jnp.where(aj >= above, aj, tie * aj)
        seen = seen + jnp.sum(eq, axis=-1, keepdims=True)


def route(aff, cap):
    B, E, T = aff.shape
    spec = pl.BlockSpec((None, E, T), lambda b: (b, 0, 0))
    return pl.pallas_call(
        functools.partial(_route_kernel, cap=cap, T=T), grid=(B,), in_specs=[spec], out_specs=spec,
        out_shape=jax.ShapeDtypeStruct(aff.shape, F32), compiler_params=_cp("parallel"), name="route",
    )(aff)


def _moe_kernel(h_ref, gate_ref, x1_ref, g2_ref, wg_ref, wu_ref, wd_ref, o_ref, acc_ref):
    e = pl.program_id(2)

    @pl.when(e == 0)
    def _():
        acc_ref[...] = jnp.zeros_like(acc_ref)

    h = h_ref[...]
    a = jnp.dot(h, wg_ref[...], preferred_element_type=F32)
    hid = (_silu(a) * jnp.dot(h, wu_ref[...], preferred_element_type=F32)).astype(BF16)
    y = jnp.dot(hid, wd_ref[...], preferred_element_type=F32)
    gt = gate_ref[...]
    lane = lax.broadcasted_iota(jnp.int32, gt.shape, 1)
    acc_ref[...] += y * jnp.sum(jnp.where(lane == e, gt, 0.0), axis=-1, keepdims=True)

    @pl.when(e == N_EXPERTS - 1)
    def _():
        o_ref[...] = x1_ref[...] + g2_ref[...] * acc_ref[...]


def moe(h2, gate_tok, x1, mod, wg, wu, wd):
    B, T, D = x1.shape
    F = wg.shape[-1]
    tm = min(512, T)
    tok = lambda n: pl.BlockSpec((None, tm, n), lambda b, i, e: (b, i, 0))
    return pl.pallas_call(
        _moe_kernel, grid=(B, T // tm, N_EXPERTS),
        in_specs=[tok(D), tok(N_EXPERTS), tok(D), pl.BlockSpec((None, 1, D), lambda b, i, e: (b, 0, 5)),
                  pl.BlockSpec((None, D, F), lambda b, i, e: (e, 0, 0)),
                  pl.BlockSpec((None, D, F), lambda b, i, e: (e, 0, 0)),
                  pl.BlockSpec((None, F, D), lambda b, i, e: (e, 0, 0))],
        out_specs=tok(D), out_shape=jax.ShapeDtypeStruct((B, T, D), F32),
        scratch_shapes=[pltpu.VMEM((tm, D), F32)],
        compiler_params=_cp("parallel", "parallel", "arbitrary"), name="moe",
    )(h2, gate_tok, x1, mod, wg, wu, wd)


def _in_cols():
    src = np.full((IN_PAD,), -1, np.int64)
    splits = (MLA_Q_LORA, MLA_KV_LORA, MLA_ROPE, 512, 512, 512, 512, 2 * GDN_HEADS, 2 * GDN_HEADS, POOL_WIDTH)
    o = np.concatenate([[0], np.cumsum(splits)])
    put = lambda name, at, lo, n: src.__setitem__(slice(SEG[name][0] + at, SEG[name][0] + at + n), np.arange(lo, lo + n))
    put("pq", 0, o[0], MLA_Q_LORA)
    put("pkv", 0, o[1], MLA_KV_LORA)
    put("pkr", MLA_NOPE, o[2], MLA_ROPE)
    for name, k in (("gq", 3), ("gk", 4), ("gv", 5), ("gz", 6)):
        put(name, 0, o[k], 512)
    for d in range(2):
        put("gab", d * LANE, o[7] + d * GDN_HEADS, GDN_HEADS)
        put("gab", d * LANE + GDN_HEADS, o[8] + d * GDN_HEADS, GDN_HEADS)
    put("pool", 0, o[9], POOL_WIDTH)
    return src


def _take_cols(w, src, axis):
    idx = jnp.asarray(np.maximum(src, 0), jnp.int32)
    mask = jnp.asarray(src >= 0)
    shape = [1] * w.ndim
    shape[axis] = -1
    return jnp.where(mask.reshape(shape), jnp.take(w, idx, axis=axis), 0.0)


def _head_pad_src(per_head, lo, n):
    src = np.full((MLA_HEADS * HEAD_PAD,), -1, np.int64)
    for h in range(MLA_HEADS):
        src[h * HEAD_PAD:h * HEAD_PAD + n] = h * per_head + lo + np.arange(n)
    return src


def _rope_tables(T, rotate):
    cos = np.ones((T, LANE), np.float32)
    sa = np.zeros((T, LANE), np.float32)
    sb = np.zeros((T, LANE), np.float32)
    if rotate:
        n_freq = MLA_ROPE // 4
        inv = ROPE_THETA ** (-np.arange(n_freq, dtype=np.float64) / n_freq)
        pos_r = np.repeat(np.arange(T // GRID_W, dtype=np.float64), GRID_W)
        pos_c = np.tile(np.arange(GRID_W, dtype=np.float64), T // GRID_W)
        for base, pos in ((MLA_NOPE, pos_r), (MLA_NOPE + 2 * n_freq, pos_c)):
            ang = pos[:, None] * inv[None, :]
            c, s = np.cos(ang), np.sin(ang)
            cos[:, base:base + n_freq] = c
            cos[:, base + n_freq:base + 2 * n_freq] = c
            sa[:, base:base + n_freq] = -s
            sb[:, base + n_freq:base + 2 * n_freq] = s
    return jnp.asarray(cos), jnp.asarray(sa), jnp.asarray(sb)


def _lane_vec(vals_by_dir, at):
    v = jnp.zeros((2, LANE), F32).at[:, at:at + GDN_HEADS].set(vals_by_dir)
    return v.reshape(1, 2 * LANE)


def kernel(x, c, ctx, c_ctx, ada_w, ada_b, norm1_g, norm2_g, w_in, mla_q_a_norm, mla_w_uq, mla_kv_a_norm, mla_w_ukv, mla_q_norm, mla_k_norm, gdn_conv_w, gdn_a_log, gdn_dt_bias, gdn_norm_g, pool_w, pool_scale, w_out, moe_router, moe_w_gate, moe_w_up, moe_w_down):
    B, T, D = x.shape
    Tc = ctx.shape[1]
    L = ada_w.shape[0]
    cvec = jnp.concatenate([c, c_ctx[None, :], jnp.zeros((SUBLANE - B - 1, D), F32)], axis=0)
    mod = ada_mod(cvec, ada_w, ada_b)
    rope_lat = _rope_tables(T, True)
    rope_ctx = _rope_tables(Tc, False)
    in_src = _in_cols()
    uq_src = _head_pad_src(MLA_QK, 0, MLA_QK)
    uk_src = _head_pad_src(MLA_NOPE + MLA_V, 0, MLA_NOPE)
    uv_src = _head_pad_src(MLA_NOPE + MLA_V, MLA_NOPE, MLA_V)
    att_src = _head_pad_src(MLA_V, 0, MLA_V)
    pad_to = lambda v, n: jnp.pad(v, (0, n - v.shape[0])).reshape(1, n)

    xc = ctx
    for l in range(L):
        need_ctx = l < L - 1
        mod_lat = mod[l, :B].reshape(B, 1, ADA_CHUNKS * D)
        mod_ctx = jnp.broadcast_to(mod[l, B].reshape(1, 1, ADA_CHUNKS * D), (B, 1, ADA_CHUNKS * D))
        w_in_p = _take_cols(w_in[l], in_src, 1).astype(BF16)
        prep_w = (
            pad_to(mla_q_a_norm[l], 256),
            jnp.pad(_take_cols(mla_w_uq[l], uq_src, 1), ((0, 256 - MLA_Q_LORA), (0, 0))).astype(BF16),
            mla_kv_a_norm[l].reshape(1, MLA_KV_LORA),
            _take_cols(mla_w_ukv[l], uk_src, 1).astype(BF16),
            _take_cols(mla_w_ukv[l], uv_src, 1).astype(BF16),
            pad_to(mla_q_norm[l] * (MLA_QK ** -0.5), HEAD_PAD),
            pad_to(mla_k_norm[l], HEAD_PAD),
            gdn_conv_w[l],
            _lane_vec(gdn_a_log[l], 0),
            _lane_vec(gdn_dt_bias[l], 0),
        )
        wo = w_out[l]
        n_att = MLA_HEADS * MLA_V
        n_gdn = GDN_HEADS * GDN_DV
        wbd = jnp.zeros((POOL_WIDTH, POOL_WIDTH), F32)
        for gi in range(len(POOL_WINDOWS)):
            wbd = wbd.at[gi * POOL_GROUP:(gi + 1) * POOL_GROUP, gi * POOL_GROUP:(gi + 1) * POOL_GROUP].set(pool_w[l, gi])
        mix_w = (
            gdn_norm_g[l].reshape(1, GDN_DV),
            _take_cols(wo[:n_att], att_src, 0).astype(BF16),
            wo[n_att:n_att + n_gdn].astype(BF16),
            wo[n_att + n_gdn:].astype(BF16),
            wbd.astype(BF16),
            pool_scale[l].reshape(1, POOL_WIDTH),
            norm2_g[l].reshape(1, D),
            moe_router[l].T,
        )
        wg, wu, wd = moe_w_gate[l].astype(BF16), moe_w_up[l].astype(BF16), moe_w_down[l].astype(BF16)

        p_lat = dict(zip(SEG, inproj(x, mod_lat, norm1_g[l], w_in_p)))
        p_ctx = dict(zip(SEG, inproj(xc, mod_ctx, norm1_g[l], w_in_p)))
        a_lat = prep(p_lat, prep_w, rope_lat)
        a_ctx = prep(p_ctx, prep_w, rope_ctx)
        cat = lambda name: jnp.concatenate([a_ctx[name], a_lat[name]], axis=1)
        att_l = attention(a_lat["Q"], cat("K"), cat("V"))
        o_all = gdn_scan(cat("q"), cat("k"), cat("v"), cat("gb"), Tc)

        def channel_mix(att, o_off, p, xin, m):
            Tn = xin.shape[1]
            x1, h2, aff = mixout(att, o_all, o_off, p["gz"], p["pool"], xin, m, mix_w)
            gate = route(aff, EC_CAPACITY_FACTOR * Tn // N_EXPERTS)
            return moe(h2, jnp.swapaxes(gate, 1, 2), x1, m, wg, wu, wd)

        x = channel_mix(att_l, Tc, p_lat, x, mod_lat)
        if need_ctx:
            att_c = attention(a_ctx["Q"], a_ctx["K"], a_ctx["V"])
            xc = channel_mix(att_c, 0, p_ctx, xc, mod_ctx)
    return x
```

```python
import functools
import math

import numpy as np
import jax
import jax.numpy as jnp
from jax import lax
from jax.experimental import pallas as pl
from jax.experimental.pallas import tpu as pltpu

F32 = jnp.float32
BF16 = jnp.bfloat16
HI = lax.Precision.HIGHEST

EPS = 1e-6
GRID_W = 64
ADA_CHUNKS = 6
MLA_HEADS = 4
MLA_NOPE = 64
MLA_ROPE = 32
MLA_QK = MLA_NOPE + MLA_ROPE
MLA_V = 64
MLA_Q_LORA = 192
MLA_KV_LORA = 128
ROPE_THETA = 10000.0
GDN_HEADS = 4
GDN_DK = 128
GDN_DV = 128
GDN_CHUNK = 64
POOL_WINDOWS = (2, 4, 8, 16)
POOL_GROUP = 64
POOL_WIDTH = POOL_GROUP * len(POOL_WINDOWS)
N_EXPERTS = 16
EC_CAPACITY_FACTOR = 2

LANE = 128
SUBLANE = 8
HEAD_PAD = 128
VMEM_LIMIT = 48 * 1024 * 1024

NT = (((1,), (1,)), ((), ()))
TN = (((0,), (0,)), ((), ()))

SEG = {}
_off = 0
for _name, _w in (("pq", 256), ("pkv", 128), ("pkr", 128), ("gq", 512), ("gk", 512), ("gv", 512),
                  ("gz", 512), ("gab", 256), ("pool", 256)):
    SEG[_name] = (_off, _w)
    _off += _w
IN_PAD = _off


def _cp(*dims):
    return pltpu.CompilerParams(dimension_semantics=dims, vmem_limit_bytes=VMEM_LIMIT)


def _silu(v):
    return v / (1.0 + jnp.exp(-v))


def _ada_kernel(c_ref, w_ref, b_ref, o_ref):
    s = _silu(c_ref[...])
    o_ref[...] = jnp.dot(s, w_ref[...], precision=HI, preferred_element_type=F32) + b_ref[...]


def ada_mod(cvec, ada_w, ada_b):
    L, D, N = ada_w.shape
    tn = N // 4
    return pl.pallas_call(
        _ada_kernel, grid=(L, N // tn),
        in_specs=[pl.BlockSpec((SUBLANE, D), lambda l, j: (0, 0)),
                  pl.BlockSpec((None, D, tn), lambda l, j: (l, 0, j)),
                  pl.BlockSpec((None, 1, tn), lambda l, j: (l, 0, j))],
        out_specs=pl.BlockSpec((None, SUBLANE, tn), lambda l, j: (l, 0, j)),
        out_shape=jax.ShapeDtypeStruct((L, SUBLANE, N), F32),
        compiler_params=_cp("parallel", "parallel"), name="ada_mod",
    )(cvec, ada_w, ada_b.reshape(L, 1, N))


def _inproj_kernel(x_ref, sh_ref, sc_ref, g_ref, w_ref, *out_refs):
    x = x_ref[...]
    h = x * lax.rsqrt(jnp.mean(x * x, axis=-1, keepdims=True) + EPS) * g_ref[...]
    hb = (h * (1.0 + sc_ref[...]) + sh_ref[...]).astype(BF16)
    for (off, n), o_ref in zip(SEG.values(), out_refs):
        o_ref[...] = jnp.dot(hb, w_ref[:, off:off + n], preferred_element_type=F32)


def inproj(x, mod, norm_g, w_in_p):
    B, T, D = x.shape
    tm = min(256, T)
    modspec = lambda k: pl.BlockSpec((None, 1, D), lambda b, i, k=k: (b, 0, k))
    return pl.pallas_call(
        _inproj_kernel, grid=(B, T // tm),
        in_specs=[pl.BlockSpec((None, tm, D), lambda b, i: (b, i, 0)), modspec(0), modspec(1),
                  pl.BlockSpec((1, D), lambda b, i: (0, 0)),
                  pl.BlockSpec((D, IN_PAD), lambda b, i: (0, 0))],
        out_specs=[pl.BlockSpec((None, tm, n), lambda b, i: (b, i, 0)) for _, n in SEG.values()],
        out_shape=[jax.ShapeDtypeStruct((B, T, n), F32) for _, n in SEG.values()],
        compiler_params=_cp("parallel", "parallel"), name="inproj",
    )(x, mod, mod, norm_g.reshape(1, D), w_in_p)


def _prep_kernel(pq_ref, pkv_ref, pkr_ref, gq_ref, gk_ref, gv_ref, gqp_ref, gkp_ref, gvp_ref,
                 gqn_ref, gkn_ref, gvn_ref, gab_ref, qan_ref, wuq_ref, kvan_ref, wuk_ref, wuv_ref,
                 qn_ref, kn_ref, cos_ref, sa_ref, sb_ref, cw_ref, alog_ref, dt_ref,
                 Q_ref, K_ref, VT_ref, q_ref, k_ref, v_ref, gb_ref, *, nt, tm):
    i = pl.program_id(1)
    cos, sa, sb = cos_ref[...], sa_ref[...], sb_ref[...]

    def rope(xh):
        return xh * cos + pltpu.roll(xh, LANE - 8, 1) * sa + pltpu.roll(xh, 8, 1) * sb

    pq = pq_ref[...]
    qa = pq * lax.rsqrt(jnp.sum(pq * pq, axis=-1, keepdims=True) * (1.0 / MLA_Q_LORA) + EPS) * qan_ref[...]
    qall = jnp.dot(qa.astype(BF16), wuq_ref[...], preferred_element_type=F32)
    pkv = pkv_ref[...]
    kva = (pkv * lax.rsqrt(jnp.mean(pkv * pkv, axis=-1, keepdims=True) + EPS) * kvan_ref[...]).astype(BF16)
    kall = jnp.dot(kva, wuk_ref[...], preferred_element_type=F32)
    vt = lax.dot_general(wuv_ref[...], kva, NT, preferred_element_type=F32)
    ones_row = lax.broadcasted_iota(jnp.int32, vt.shape, 0) % HEAD_PAD == MLA_V
    VT_ref[...] = jnp.where(ones_row, 1.0, vt).astype(BF16)
    pkr = pkr_ref[...]
    for h in range(MLA_HEADS):
        sl = slice(h * HEAD_PAD, (h + 1) * HEAD_PAD)
        qh = qall[:, sl]
        qh = qh * lax.rsqrt(jnp.sum(qh * qh, axis=-1, keepdims=True) * (1.0 / MLA_QK) + EPS) * qn_ref[...]
        Q_ref[:, sl] = rope(qh).astype(BF16)
        kh = kall[:, sl] + pkr
        kh = kh * lax.rsqrt(jnp.sum(kh * kh, axis=-1, keepdims=True) * (1.0 / MLA_QK) + EPS) * kn_ref[...]
        K_ref[:, sl] = rope(kh).astype(BF16)

    rid = lax.broadcasted_iota(jnp.int32, (tm, GDN_HEADS * GDN_DK), 0)

    def conv_silu(u_ref, up_ref, un_ref, c0):
        u = u_ref[...]
        n = u.shape[1]
        prev_row = jnp.where(i > 0, up_ref[SUBLANE - 1:SUBLANE, :], 0.0)
        next_row = jnp.where(i < nt - 1, un_ref[0:1, :], 0.0)
        um = jnp.where(rid == 0, prev_row, pltpu.roll(u, 1, 0))
        up = jnp.where(rid == tm - 1, next_row, pltpu.roll(u, tm - 1, 0))
        y = um * cw_ref[0:1, c0:c0 + n] + u * cw_ref[1:2, c0:c0 + n] + up * cw_ref[2:3, c0:c0 + n]
        return _silu(y)

    cq = conv_silu(gq_ref, gqp_ref, gqn_ref, 0)
    ck = conv_silu(gk_ref, gkp_ref, gkn_ref, GDN_HEADS * GDN_DK)
    v_ref[...] = conv_silu(gv_ref, gvp_ref, gvn_ref, 2 * GDN_HEADS * GDN_DK)
    for h in range(GDN_HEADS):
        sl = slice(h * GDN_DK, (h + 1) * GDN_DK)
        qh = cq[:, sl]
        q_ref[:, sl] = qh * lax.rsqrt(jnp.sum(qh * qh, axis=-1, keepdims=True) + EPS) * (GDN_DK ** -0.5)
        kh = ck[:, sl]
        k_ref[:, sl] = kh * lax.rsqrt(jnp.sum(kh * kh, axis=-1, keepdims=True) + EPS)

    pre = gab_ref[...]
    lane = lax.broadcasted_iota(jnp.int32, pre.shape, 1) % LANE
    sp_in = pre + dt_ref[...]
    softplus = jnp.maximum(sp_in, 0.0) + jnp.log(1.0 + jnp.exp(-jnp.abs(sp_in)))
    g = -jnp.exp(alog_ref[...]) * softplus
    beta = 1.0 / (1.0 + jnp.exp(-pre))
    gb_ref[...] = jnp.where(lane < GDN_HEADS, g, jnp.where(lane < 2 * GDN_HEADS, beta, 0.0))


def prep(p, wts, rope_tabs):
    pq, pkv, pkr, gq, gk, gv, gab = (p[k] for k in ("pq", "pkv", "pkr", "gq", "gk", "gv", "gab"))
    B, T, _ = pq.shape
    tm = min(256, T)
    nt = T // tm
    tb = tm // SUBLANE
    nb = T // SUBLANE
    cur = lambda n: pl.BlockSpec((None, tm, n), lambda b, i: (b, i, 0))
    prv = lambda n: pl.BlockSpec((None, SUBLANE, n), lambda b, i: (b, jnp.maximum(i * tb - 1, 0), 0))
    nxt = lambda n: pl.BlockSpec((None, SUBLANE, n), lambda b, i: (b, jnp.minimum((i + 1) * tb, nb - 1), 0))
    full = lambda a: pl.BlockSpec(a.shape, lambda b, i: (0,) * a.ndim)
    tab = pl.BlockSpec((tm, LANE), lambda b, i: (i, 0))
    W = GDN_HEADS * GDN_DK
    outs = [("Q", MLA_HEADS * HEAD_PAD, BF16), ("K", MLA_HEADS * HEAD_PAD, BF16), ("VT", None, BF16),
            ("q", W, F32), ("k", W, F32), ("v", W, F32), ("gb", 2 * LANE, F32)]
    HP = MLA_HEADS * HEAD_PAD
    res = pl.pallas_call(
        functools.partial(_prep_kernel, nt=nt, tm=tm), grid=(B, nt),
        in_specs=[cur(256), cur(128), cur(128), cur(W), cur(W), cur(W), prv(W), prv(W), prv(W),
                  nxt(W), nxt(W), nxt(W), cur(256)] + [full(a) for a in wts[:7]] + [tab, tab, tab]
                 + [full(a) for a in wts[7:]],
        out_specs=[cur(n) if n else pl.BlockSpec((None, HP, tm), lambda b, i: (b, 0, i)) for _, n, _ in outs],
        out_shape=[jax.ShapeDtypeStruct((B, T, n) if n else (B, HP, T), dt) for _, n, dt in outs],
        compiler_params=_cp("parallel", "parallel"), name="prep",
    )(pq, pkv, pkr, gq, gk, gv, gq, gk, gv, gq, gk, gv, gab, *wts[:7], *rope_tabs, *wts[7:])
    return dict(zip([n for n, _, _ in outs], res))


def _attn_kernel(q_ref, k_ref, vt_ref, o_ref, sa_ref, sb_ref, *, ck, nk):
    q = q_ref[...]
    tq = q.shape[0]

    def scores(j):
        off = pl.multiple_of(j * ck, ck)
        return lax.dot_general(k_ref[pl.ds(off, ck), :], q, NT, preferred_element_type=F32)

    nv = MLA_V + 16

    def update(carry, s_ref, j):
        m, acc = carry
        s = s_ref[...]
        m_new = jnp.maximum(m, jnp.max(s, axis=0, keepdims=True))
        p = jnp.exp2(s - m_new).astype(BF16)
        off = pl.multiple_of(j * ck, ck)
        acc = jnp.exp2(m - m_new) * acc + jnp.dot(vt_ref[0:nv, pl.ds(off, ck)], p, preferred_element_type=F32)
        return m_new, acc

    def pair(jj, carry):
        j = 2 * jj
        sb_ref[...] = scores(j + 1)
        carry = update(carry, sa_ref, j)
        sa_ref[...] = scores(jnp.minimum(j + 2, nk - 1))
        return update(carry, sb_ref, j + 1)

    sa_ref[...] = scores(0)
    carry = (jnp.full((1, tq), -1e30, F32), jnp.zeros((nv, tq), F32))
    carry = lax.fori_loop(0, nk // 2, pair, carry)
    if nk % 2:
        carry = update(carry, sa_ref, nk - 1)
    acc = carry[1]
    o = acc[:MLA_V] / acc[MLA_V:MLA_V + 1]
    o_ref[...] = jnp.concatenate([o, jnp.zeros((HEAD_PAD - MLA_V, tq), F32)], axis=0).T.astype(o_ref.dtype)


def attention(Q, K, VT):
    B, Tq, _ = Q.shape
    Tk = K.shape[1]
    tq = min(512, Tq)
    ck = next(c for c in (768, 512, 384, 256, 128) if Tk % c == 0)
    qo = pl.BlockSpec((None, tq, HEAD_PAD), lambda b, h, i: (b, i, h))
    return pl.pallas_call(
        functools.partial(_attn_kernel, ck=ck, nk=Tk // ck), grid=(B, MLA_HEADS, Tq // tq),
        in_specs=[qo, pl.BlockSpec((None, Tk, HEAD_PAD), lambda b, h, i: (b, 0, h)),
                  pl.BlockSpec((None, HEAD_PAD, Tk), lambda b, h, i: (b, h, 0))],
        out_specs=qo, out_shape=jax.ShapeDtypeStruct(Q.shape, BF16),
        scratch_shapes=[pltpu.VMEM((ck, tq), F32), pltpu.VMEM((ck, tq), F32)],
        compiler_params=_cp("parallel", "parallel", "parallel"), name="attention",
    )(Q, K, VT)


GDN_GROUP = 4


def _gdn_prep_kernel(q_ref, k_ref, v_ref, gb_ref, wq_ref, u_ref, qk_ref, kdt_ref, egl_ref, *, C, G):
    fwd = pl.program_id(1) == 0
    row = lax.broadcasted_iota(jnp.int32, (C, C), 0)
    col = lax.broadcasted_iota(jnp.int32, (C, C), 1)
    ahead = jnp.where(fwd, row - col, col - row)
    incl = ahead >= 0
    strict = ahead > 0
    incl16 = incl.astype(F32).astype(BF16)
    eye = (row == col).astype(F32)
    eye16 = eye.astype(BF16)
    dot = functools.partial(jnp.dot, preferred_element_type=F32)

    def terms(v):
        hi = v.astype(BF16)
        rest = v - hi.astype(F32)
        mid = rest.astype(BF16)
        return jnp.concatenate([hi, mid, (rest - mid.astype(F32)).astype(BF16)], axis=1)

    gbs = [gb_ref[g * C:(g + 1) * C, :] for g in range(G)]
    gterms = [terms(gb) for gb in gbs]
    fold = lambda a, axis: sum(jnp.split(a, 3, axis=axis)[1:], jnp.split(a, 3, axis=axis)[0])
    gcs = [fold(dot(incl16, t), 1) for t in gterms]
    gcts = [fold(lax.dot_general(t, incl16, (((0,), (1,)), ((), ())), preferred_element_type=F32), 0)
            for t in gterms]
    glasts = [jnp.where(fwd, gc[C - 1:C, :], gc[0:1, :]) for gc in gcs]
    for g in range(G):
        egl_ref[g] = jnp.broadcast_to(jnp.exp(glasts[g]), (SUBLANE, LANE))

    chains = [(g, h) for g in range(G) for h in range(GDN_HEADS)]
    hsl = lambda h: slice(h * GDN_DK, (h + 1) * GDN_DK)
    csl = lambda h: slice(h * C, (h + 1) * C)
    ld = lambda ref, g, h: ref[g * C:(g + 1) * C, hsl(h)]
    gcol = {c: gcs[c[0]][:, c[1]:c[1] + 1] for c in chains}
    bcol = {c: gbs[c[0]][:, GDN_HEADS + c[1]:GDN_HEADS + c[1] + 1] for c in chains}
    decay = {c: jnp.exp(jnp.where(incl, gcol[c] - gcts[c[0]][c[1]:c[1] + 1, :], -1e30)) for c in chains}
    k16 = {c: ld(k_ref, *c).astype(BF16) for c in chains}
    qkk = {c: lax.dot_general(jnp.concatenate([ld(q_ref, *c).astype(BF16), k16[c]], axis=0), k16[c], NT,
                              preferred_element_type=F32) for c in chains}
    a = {}
    for c in chains:
        g, h = c
        qk_ref[g, :, csl(h)] = (qkk[c][:C] * decay[c]).astype(BF16)
        a[c] = jnp.where(strict, qkk[c][C:] * decay[c], 0.0) * bcol[c]
    same = lambda s: (row >> s) == (col >> s)
    tinv = {c: eye - jnp.where(same(1), a[c], 0.0) for c in chains}
    for s in range(1, int(math.log2(C))):
        join = lambda v: jnp.where(same(s + 1), jnp.where(same(s), 0.0, v), 0.0)
        x16 = {c: tinv[c].astype(BF16) for c in chains}
        xl = {c: dot(x16[c], join(a[c]).astype(BF16)) for c in chains}
        xlx = {c: dot(xl[c].astype(BF16), x16[c]) for c in chains}
        tinv = {c: tinv[c] - xlx[c] for c in chains}
    eg = {c: jnp.exp(gcol[c]) for c in chains}
    wu = {c: dot(tinv[c].astype(BF16),
                 jnp.concatenate([ld(k_ref, *c) * (bcol[c] * eg[c]), ld(v_ref, *c) * bcol[c]], axis=1).astype(BF16))
          for c in chains}
    kdt = {c: lax.dot_general((ld(k_ref, *c) * jnp.exp(glasts[c[0]][:, c[1]:c[1] + 1] - gcol[c])).astype(BF16), eye16,
                              TN, preferred_element_type=F32) for c in chains}
    for c in chains:
        g, h = c
        wq_ref[g, 0:C, hsl(h)] = wu[c][:, :GDN_DK].astype(BF16)
        wq_ref[g, C:2 * C, hsl(h)] = (ld(q_ref, *c) * eg[c]).astype(BF16)
        u_ref[g, :, hsl(h)] = wu[c][:, GDN_DK:]
        kdt_ref[g, :, csl(h)] = kdt[c].astype(BF16)


def _gdn_rec_kernel(*refs, C, G, B):
    ins = (refs[0:5], refs[5:10])
    outs = refs[10:12]
    s_ref = refs[12]
    dot = functools.partial(jnp.dot, preferred_element_type=F32)

    @pl.when(pl.program_id(0) == 0)
    def _():
        s_ref[...] = jnp.zeros_like(s_ref)

    hsl = lambda h: slice(h * GDN_DK, (h + 1) * GDN_DK)
    csl = lambda h: slice(h * C, (h + 1) * C)
    for step in range(G):
        chains = [(d, b, h, step if d == 0 else G - 1 - step)
                  for d in range(2) for b in range(B) for h in range(GDN_HEADS)]
        S = {c: s_ref[c[0], c[1], c[2]] for c in chains}
        r = {(d, b, h, g): dot(ins[d][0][b, g, :, hsl(h)], S[(d, b, h, g)].astype(BF16))
             for (d, b, h, g) in chains}
        vn = {(d, b, h, g): (ins[d][1][b, g, :, hsl(h)] - r[(d, b, h, g)][:C]).astype(BF16) for (d, b, h, g) in chains}
        o = {(d, b, h, g): dot(ins[d][2][b, g, :, csl(h)], vn[(d, b, h, g)]) for (d, b, h, g) in chains}
        upd = {(d, b, h, g): dot(ins[d][3][b, g, :, csl(h)], vn[(d, b, h, g)]) for (d, b, h, g) in chains}
        for c in chains:
            d, b, h, g = c
            outs[d][b, g, :, hsl(h)] = r[c][C:] + o[c]
            s_ref[d, b, h] = S[c] * ins[d][4][b, g, 0:1, h:h + 1] + upd[c]


def gdn_scan(q, k, v, gb, n_ctx):
    B, Tt, W = q.shape
    C, G = GDN_CHUNK, GDN_GROUP
    n = Tt // C
    ng = n // G
    ncg = n_ctx // (C * G)
    assert n % G == 0 and n_ctx % (C * G) == 0
    tok = pl.BlockSpec((None, G * C, W), lambda b, d, s: (b, s, 0))
    per_chunk = lambda r, w, idx: pl.BlockSpec((None, None, G, r, w), idx)
    shapes = [((2 * C, W), BF16), ((C, W), F32), ((C, GDN_HEADS * C), BF16), ((GDN_DK, GDN_HEADS * C), BF16),
              ((SUBLANE, LANE), F32)]
    nat = lambda b, d, s: (b, d, s, 0, 0)
    mid = pl.pallas_call(
        functools.partial(_gdn_prep_kernel, C=C, G=G), grid=(B, 2, ng),
        in_specs=[tok, tok, tok, pl.BlockSpec((None, G * C, LANE), lambda b, d, s: (b, s, d))],
        out_specs=[per_chunk(r, w, nat) for (r, w), _ in shapes],
        out_shape=[jax.ShapeDtypeStruct((B, 2, n, r, w), dt) for (r, w), dt in shapes],
        compiler_params=_cp("parallel", "parallel", "parallel"), name="gdn_prep",
    )(q, k, v, gb)

    bwd = lambda s: jnp.where(s < ncg, ncg - 1 - s, ng + ncg - 1 - s)
    both = lambda r, w, d: pl.BlockSpec((B, None, G, r, w), (lambda s: (0, 0, s, 0, 0)) if d == 0
                                        else (lambda s: (0, 1, bwd(s), 0, 0)))
    o_spec = lambda d: pl.BlockSpec((B, G, C, W), (lambda s: (0, s, 0, 0)) if d == 0 else (lambda s: (0, bwd(s), 0, 0)))
    o_f, o_b = pl.pallas_call(
        functools.partial(_gdn_rec_kernel, C=C, G=G, B=B), grid=(ng,),
        in_specs=[both(r, w, d) for d in range(2) for (r, w), _ in shapes],
        out_specs=[o_spec(0), o_spec(1)],
        out_shape=[jax.ShapeDtypeStruct((B, n, C, W), F32)] * 2,
        scratch_shapes=[pltpu.VMEM((2, B, GDN_HEADS, GDN_DK, GDN_DV), F32)],
        compiler_params=_cp("arbitrary"), name="gdn_rec",
    )(*mid, *mid)
    return o_f.reshape(B, Tt, W), o_b.reshape(B, Tt, W)


def _mixout_kernel(att_ref, of_ref, ob_ref, z_ref, u_ref, up_ref, un_ref, x_ref, g1_ref, sh2_ref, sc2_ref,
                   gng_ref, wa_ref, wg_ref, wp_ref, wbd_ref, ps_ref, n2g_ref, wr_ref,
                   x1_ref, h2_ref, aff_ref, *, tm, T, nt):
    i = pl.program_id(1)
    o = of_ref[...] + ob_ref[...]
    z = z_ref[...]
    parts = []
    for h in range(GDN_HEADS):
        sl = slice(h * GDN_DV, (h + 1) * GDN_DV)
        oh = o[:, sl]
        oh = oh * lax.rsqrt(jnp.mean(oh * oh, axis=-1, keepdims=True) + EPS) * gng_ref[...]
        parts.append((oh * _silu(z[:, sl])).astype(BF16))
    gdn = jnp.concatenate(parts, axis=1)

    u = u_ref[...]
    halo = SUBLANE
    ext = jnp.concatenate([jnp.where(i > 0, up_ref[...], 0.0), u, jnp.where(i < nt - 1, un_ref[...], 0.0)], axis=0)
    n_ext = tm + 2 * halo
    back = lambda a, s: pltpu.roll(a, s, 0)
    ahead = lambda a, s: pltpu.roll(a, n_ext - s, 0)
    s2 = ext + back(ext, 1)
    s4 = back(s2, 1) + ahead(s2, 1)
    s8 = back(s4, 2) + ahead(s4, 2)
    s16 = back(s8, 4) + ahead(s8, 4)
    t = i * tm + lax.broadcasted_iota(jnp.int32, (tm, 1), 0)
    lane = lax.broadcasted_iota(jnp.int32, (tm, POOL_WIDTH), 1)
    mean = None
    for gi, (win, sw) in reversed(list(enumerate(zip(POOL_WINDOWS, (s2, s4, s8, s16))))):
        lo = jnp.maximum(t - win // 2, 0)
        hi = jnp.minimum(t - win // 2 + win, T)
        m = sw[halo:halo + tm, :] / (hi - lo).astype(F32)
        mean = m if mean is None else jnp.where(lane < (gi + 1) * POOL_GROUP, m, mean)
    yp = jnp.dot((mean - u).astype(BF16), wbd_ref[...], preferred_element_type=F32) * ps_ref[...]

    y = (jnp.dot(att_ref[...], wa_ref[...], preferred_element_type=F32)
         + jnp.dot(gdn, wg_ref[...], preferred_element_type=F32)
         + jnp.dot(yp.astype(BF16), wp_ref[...], preferred_element_type=F32))
    x1 = x_ref[...] + g1_ref[...] * y
    x1_ref[...] = x1
    h2 = x1 * lax.rsqrt(jnp.mean(x1 * x1, axis=-1, keepdims=True) + EPS) * n2g_ref[...]
    h2 = h2 * (1.0 + sc2_ref[...]) + sh2_ref[...]
    h2_ref[...] = h2.astype(BF16)
    lg = lax.dot_general(wr_ref[...], h2, NT, precision=HI, preferred_element_type=F32)
    e = jnp.exp(lg - jnp.max(lg, axis=0, keepdims=True))
    aff_ref[...] = e / jnp.sum(e, axis=0, keepdims=True)


def mixout(att, o, o_off, z, u, x, mod, wts):
    B, T, D = x.shape
    tm = min(256, T)
    nt = T // tm
    tb = tm // SUBLANE
    nb = T // SUBLANE
    assert o_off % tm == 0
    ob = o_off // tm
    W = GDN_HEADS * GDN_DV
    cur = lambda n: pl.BlockSpec((None, tm, n), lambda b, i: (b, i, 0))
    odir = pl.BlockSpec((None, tm, W), lambda b, i: (b, i + ob, 0))
    modspec = lambda k: pl.BlockSpec((None, 1, D), lambda b, i: (b, 0, k))
    full = lambda a: pl.BlockSpec(a.shape, lambda b, i: (0,) * a.ndim)
    return pl.pallas_call(
        functools.partial(_mixout_kernel, tm=tm, T=T, nt=nt), grid=(B, nt),
        in_specs=[cur(MLA_HEADS * HEAD_PAD), odir, odir, cur(W), cur(POOL_WIDTH),
                  pl.BlockSpec((None, SUBLANE, POOL_WIDTH), lambda b, i: (b, jnp.maximum(i * tb - 1, 0), 0)),
                  pl.BlockSpec((None, SUBLANE, POOL_WIDTH), lambda b, i: (b, jnp.minimum((i + 1) * tb, nb - 1), 0)),
                  cur(D), modspec(2), modspec(3), modspec(4)] + [full(a) for a in wts],
        out_specs=[cur(D), cur(D), pl.BlockSpec((None, N_EXPERTS, tm), lambda b, i: (b, 0, i))],
        out_shape=[jax.ShapeDtypeStruct((B, T, D), F32), jax.ShapeDtypeStruct((B, T, D), BF16),
                   jax.ShapeDtypeStruct((B, N_EXPERTS, T), F32)],
        compiler_params=_cp("parallel", "parallel"), name="mixout",
    )(att, o[0], o[1], z, u, u, u, x, mod, mod, mod, *wts)


def _route_kernel(aff_ref, gate_ref, *, cap, T):
    aff = aff_ref[...]

    def body(it, res):
        cand = res | jnp.left_shift(jnp.int32(1), 30 - it)
        cnt = jnp.sum((aff >= pltpu.bitcast(cand, F32)).astype(jnp.int32), axis=-1, keepdims=True)
        return jnp.where(cnt >= cap, cand, res)

    bits = lax.fori_loop(0, 31, body, jnp.zeros((N_EXPERTS, 1), jnp.int32))
    thr = pltpu.bitcast(bits, F32)
    above = pltpu.bitcast(bits + 1, F32)
    n_gt = jnp.sum((aff >= above).astype(jnp.int32), axis=-1, keepdims=True)
    need = (cap - n_gt).astype(F32)
    upper = (lax.broadcasted_iota(jnp.int32, (LANE, LANE), 0)
             < lax.broadcasted_iota(jnp.int32, (LANE, LANE), 1)).astype(BF16)
    seen = jnp.zeros((N_EXPERTS, 1), F32)
    for j in range(T // LANE):
        sl = slice(j * LANE, (j + 1) * LANE)
        aj = aff[:, sl]
        eq = jnp.where(aj >= thr, jnp.where(aj < above, 1.0, 0.0), 0.0)
        rank = jnp.dot(eq.astype(BF16), upper, preferred_element_type=F32) + seen
        tie = jnp.where(rank < need, eq, 0.0)
        gate_ref[:, sl] = jnp.where(aj >= above, aj, tie * aj)
        seen = seen + jnp.sum(eq, axis=-1, keepdims=True)


def route(aff, cap):
    B, E, T = aff.shape
    spec = pl.BlockSpec((None, E, T), lambda b: (b, 0, 0))
    return pl.pallas_call(
        functools.partial(_route_kernel, cap=cap, T=T), grid=(B,), in_specs=[spec], out_specs=spec,
        out_shape=jax.ShapeDtypeStruct(aff.shape, F32), compiler_params=_cp("parallel"), name="route",
    )(aff)


def _moe_kernel(h_ref, gate_ref, x1_ref, g2_ref, wg_ref, wu_ref, wd_ref, o_ref, acc_ref):
    e = pl.program_id(2)

    @pl.when(e == 0)
    def _():
        acc_ref[...] = jnp.zeros_like(acc_ref)

    h = h_ref[...]
    a = jnp.dot(h, wg_ref[...], preferred_element_type=F32)
    hid = (_silu(a) * jnp.dot(h, wu_ref[...], preferred_element_type=F32)).astype(BF16)
    y = jnp.dot(hid, wd_ref[...], preferred_element_type=F32)
    gt = gate_ref[...]
    lane = lax.broadcasted_iota(jnp.int32, gt.shape, 1)
    acc_ref[...] += y * jnp.sum(jnp.where(lane == e, gt, 0.0), axis=-1, keepdims=True)

    @pl.when(e == N_EXPERTS - 1)
    def _():
        o_ref[...] = x1_ref[...] + g2_ref[...] * acc_ref[...]


def moe(h2, gate_tok, x1, mod, wg, wu, wd):
    B, T, D = x1.shape
    F = wg.shape[-1]
    tm = min(512, T)
    tok = lambda n: pl.BlockSpec((None, tm, n), lambda b, i, e: (b, i, 0))
    return pl.pallas_call(
        _moe_kernel, grid=(B, T // tm, N_EXPERTS),
        in_specs=[tok(D), tok(N_EXPERTS), tok(D), pl.BlockSpec((None, 1, D), lambda b, i, e: (b, 0, 5)),
                  pl.BlockSpec((None, D, F), lambda b, i, e: (e, 0, 0)),
                  pl.BlockSpec((None, D, F), lambda b, i, e: (e, 0, 0)),
                  pl.BlockSpec((None, F, D), lambda b, i, e: (e, 0, 0))],
        out_specs=tok(D), out_shape=jax.ShapeDtypeStruct((B, T, D), F32),
        scratch_shapes=[pltpu.VMEM((tm, D), F32)],
        compiler_params=_cp("parallel", "parallel", "arbitrary"), name="moe",
    )(h2, gate_tok, x1, mod, wg, wu, wd)


def _in_cols():
    src = np.full((IN_PAD,), -1, np.int64)
    splits = (MLA_Q_LORA, MLA_KV_LORA, MLA_ROPE, 512, 512, 512, 512, 2 * GDN_HEADS, 2 * GDN_HEADS, POOL_WIDTH)
    o = np.concatenate([[0], np.cumsum(splits)])
    put = lambda name, at, lo, n: src.__setitem__(slice(SEG[name][0] + at, SEG[name][0] + at + n), np.arange(lo, lo + n))
    put("pq", 0, o[0], MLA_Q_LORA)
    put("pkv", 0, o[1], MLA_KV_LORA)
    put("pkr", MLA_NOPE, o[2], MLA_ROPE)
    for name, k in (("gq", 3), ("gk", 4), ("gv", 5), ("gz", 6)):
        put(name, 0, o[k], 512)
    for d in range(2):
        put("gab", d * LANE, o[7] + d * GDN_HEADS, GDN_HEADS)
        put("gab", d * LANE + GDN_HEADS, o[8] + d * GDN_HEADS, GDN_HEADS)
    put("pool", 0, o[9], POOL_WIDTH)
    return src


def _take_cols(w, src, axis):
    idx = jnp.asarray(np.maximum(src, 0), jnp.int32)
    mask = jnp.asarray(src >= 0)
    shape = [1] * w.ndim
    shape[axis] = -1
    return jnp.where(mask.reshape(shape), jnp.take(w, idx, axis=axis), 0.0)


def _head_pad_src(per_head, lo, n):
    src = np.full((MLA_HEADS * HEAD_PAD,), -1, np.int64)
    for h in range(MLA_HEADS):
        src[h * HEAD_PAD:h * HEAD_PAD + n] = h * per_head + lo + np.arange(n)
    return src


def _rope_tables(T, rotate):
    cos = np.ones((T, LANE), np.float32)
    sa = np.zeros((T, LANE), np.float32)
    sb = np.zeros((T, LANE), np.float32)
    if rotate:
        n_freq = MLA_ROPE // 4
        inv = ROPE_THETA ** (-np.arange(n_freq, dtype=np.float64) / n_freq)
        pos_r = np.repeat(np.arange(T // GRID_W, dtype=np.float64), GRID_W)
        pos_c = np.tile(np.arange(GRID_W, dtype=np.float64), T // GRID_W)
        for base, pos in ((MLA_NOPE, pos_r), (MLA_NOPE + 2 * n_freq, pos_c)):
            ang = pos[:, None] * inv[None, :]
            c, s = np.cos(ang), np.sin(ang)
            cos[:, base:base + n_freq] = c
            cos[:, base + n_freq:base + 2 * n_freq] = c
            sa[:, base:base + n_freq] = -s
            sb[:, base + n_freq:base + 2 * n_freq] = s
    return jnp.asarray(cos), jnp.asarray(sa), jnp.asarray(sb)


def _lane_vec(vals_by_dir, at):
    v = jnp.zeros((2, LANE), F32).at[:, at:at + GDN_HEADS].set(vals_by_dir)
    return v.reshape(1, 2 * LANE)


def kernel(x, c, ctx, c_ctx, ada_w, ada_b, norm1_g, norm2_g, w_in, mla_q_a_norm, mla_w_uq, mla_kv_a_norm, mla_w_ukv, mla_q_norm, mla_k_norm, gdn_conv_w, gdn_a_log, gdn_dt_bias, gdn_norm_g, pool_w, pool_scale, w_out, moe_router, moe_w_gate, moe_w_up, moe_w_down):
    B, T, D = x.shape
    Tc = ctx.shape[1]
    L = ada_w.shape[0]
    cvec = jnp.concatenate([c, c_ctx[None, :], jnp.zeros((SUBLANE - B - 1, D), F32)], axis=0)
    mod = ada_mod(cvec, ada_w, ada_b)
    rope_lat = _rope_tables(T, True)
    rope_ctx = _rope_tables(Tc, False)
    in_src = _in_cols()
    uq_src = _head_pad_src(MLA_QK, 0, MLA_QK)
    uk_src = _head_pad_src(MLA_NOPE + MLA_V, 0, MLA_NOPE)
    uv_src = _head_pad_src(MLA_NOPE + MLA_V, MLA_NOPE, MLA_V)
    att_src = _head_pad_src(MLA_V, 0, MLA_V)
    pad_to = lambda v, n: jnp.pad(v, (0, n - v.shape[0])).reshape(1, n)

    xc = ctx
    for l in range(L):
        need_ctx = l < L - 1
        mod_lat = mod[l, :B].reshape(B, 1, ADA_CHUNKS * D)
        mod_ctx = jnp.broadcast_to(mod[l, B].reshape(1, 1, ADA_CHUNKS * D), (B, 1, ADA_CHUNKS * D))
        w_in_p = _take_cols(w_in[l], in_src, 1).astype(BF16)
        prep_w = (
            pad_to(mla_q_a_norm[l], 256),
            jnp.pad(_take_cols(mla_w_uq[l], uq_src, 1), ((0, 256 - MLA_Q_LORA), (0, 0))).astype(BF16),
            mla_kv_a_norm[l].reshape(1, MLA_KV_LORA),
            _take_cols(mla_w_ukv[l], uk_src, 1).astype(BF16),
            _take_cols(mla_w_ukv[l], uv_src, 1).T.astype(BF16),
            pad_to(mla_q_norm[l] * (MLA_QK ** -0.5 * math.log2(math.e)), HEAD_PAD),
            pad_to(mla_k_norm[l], HEAD_PAD),
            gdn_conv_w[l],
            _lane_vec(gdn_a_log[l], 0),
            _lane_vec(gdn_dt_bias[l], 0),
        )
        wo = w_out[l]
        n_att = MLA_HEADS * MLA_V
        n_gdn = GDN_HEADS * GDN_DV
        wbd = jnp.zeros((POOL_WIDTH, POOL_WIDTH), F32)
        for gi in range(len(POOL_WINDOWS)):
            wbd = wbd.at[gi * POOL_GROUP:(gi + 1) * POOL_GROUP, gi * POOL_GROUP:(gi + 1) * POOL_GROUP].set(pool_w[l, gi])
        mix_w = (
            gdn_norm_g[l].reshape(1, GDN_DV),
            _take_cols(wo[:n_att], att_src, 0).astype(BF16),
            wo[n_att:n_att + n_gdn].astype(BF16),
            wo[n_att + n_gdn:].astype(BF16),
            wbd.astype(BF16),
            pool_scale[l].reshape(1, POOL_WIDTH),
            norm2_g[l].reshape(1, D),
            moe_router[l].T,
        )
        wg, wu, wd = moe_w_gate[l].astype(BF16), moe_w_up[l].astype(BF16), moe_w_down[l].astype(BF16)

        p_lat = dict(zip(SEG, inproj(x, mod_lat, norm1_g[l], w_in_p)))
        p_ctx = dict(zip(SEG, inproj(xc, mod_ctx, norm1_g[l], w_in_p)))
        a_lat = prep(p_lat, prep_w, rope_lat)
        a_ctx = prep(p_ctx, prep_w, rope_ctx)
        cat = lambda name, axis=1: jnp.concatenate([a_ctx[name], a_lat[name]], axis=axis)
        att_l = attention(a_lat["Q"], cat("K"), cat("VT", 2))
        o_all = gdn_scan(cat("q"), cat("k"), cat("v"), cat("gb"), Tc)

        def channel_mix(att, o_off, p, xin, m):
            Tn = xin.shape[1]
            x1, h2, aff = mixout(att, o_all, o_off, p["gz"], p["pool"], xin, m, mix_w)
            gate = route(aff, EC_CAPACITY_FACTOR * Tn // N_EXPERTS)
            return moe(h2, jnp.swapaxes(gate, 1, 2), x1, m, wg, wu, wd)

        x = channel_mix(att_l, Tc, p_lat, x, mod_lat)
        if need_ctx:
            att_c = attention(a_ctx["Q"], a_ctx["K"], a_ctx["VT"])
            xc = channel_mix(att_c, 0, p_ctx, xc, mod_ctx)
    return x
```

```python
import functools
import math

import numpy as np
import jax
import jax.numpy as jnp
from jax import lax
from jax.experimental import pallas as pl
from jax.experimental.pallas import tpu as pltpu

F32 = jnp.float32
BF16 = jnp.bfloat16
HI = lax.Precision.HIGHEST

EPS = 1e-6
GRID_W = 64
ADA_CHUNKS = 6
MLA_HEADS = 4
MLA_NOPE = 64
MLA_ROPE = 32
MLA_QK = MLA_NOPE + MLA_ROPE
MLA_V = 64
MLA_Q_LORA = 192
MLA_KV_LORA = 128
ROPE_THETA = 10000.0
GDN_HEADS = 4
GDN_DK = 128
GDN_DV = 128
GDN_CHUNK = 64
POOL_WINDOWS = (2, 4, 8, 16)
POOL_GROUP = 64
POOL_WIDTH = POOL_GROUP * len(POOL_WINDOWS)
N_EXPERTS = 16
EC_CAPACITY_FACTOR = 2

LANE = 128
SUBLANE = 8
HEAD_PAD = 128
VMEM_LIMIT = 48 * 1024 * 1024

NT = (((1,), (1,)), ((), ()))
TN = (((0,), (0,)), ((), ()))

SEG = {}
_off = 0
for _name, _w in (("pq", 256), ("pkv", 128), ("pkr", 128), ("gq", 512), ("gk", 512), ("gv", 512),
                  ("gz", 512), ("gab", 256), ("pool", 256)):
    SEG[_name] = (_off, _w)
    _off += _w
IN_PAD = _off


def _cp(*dims):
    return pltpu.CompilerParams(dimension_semantics=dims, vmem_limit_bytes=VMEM_LIMIT)


def _silu(v):
    return v / (1.0 + jnp.exp(-v))


def _ada_kernel(c_ref, w_ref, b_ref, o_ref):
    s = _silu(c_ref[...])
    o_ref[...] = jnp.dot(s, w_ref[...], precision=HI, preferred_element_type=F32) + b_ref[...]


def ada_mod(cvec, ada_w, ada_b):
    L, D, N = ada_w.shape
    tn = N // 4
    return pl.pallas_call(
        _ada_kernel, grid=(L, N // tn),
        in_specs=[pl.BlockSpec((SUBLANE, D), lambda l, j: (0, 0)),
                  pl.BlockSpec((None, D, tn), lambda l, j: (l, 0, j)),
                  pl.BlockSpec((None, 1, tn), lambda l, j: (l, 0, j))],
        out_specs=pl.BlockSpec((None, SUBLANE, tn), lambda l, j: (l, 0, j)),
        out_shape=jax.ShapeDtypeStruct((L, SUBLANE, N), F32),
        compiler_params=_cp("parallel", "parallel"), name="ada_mod",
    )(cvec, ada_w, ada_b.reshape(L, 1, N))


def _inproj_kernel(x_ref, sh_ref, sc_ref, g_ref, w_ref, *out_refs):
    x = x_ref[...]
    h = x * lax.rsqrt(jnp.mean(x * x, axis=-1, keepdims=True) + EPS) * g_ref[...]
    hb = (h * (1.0 + sc_ref[...]) + sh_ref[...]).astype(BF16)
    for (off, n), o_ref in zip(SEG.values(), out_refs):
        o_ref[...] = jnp.dot(hb, w_ref[:, off:off + n], preferred_element_type=F32)


def inproj(x, mod, norm_g, w_in_p):
    B, T, D = x.shape
    tm = min(256, T)
    modspec = lambda k: pl.BlockSpec((None, 1, D), lambda b, i, k=k: (b, 0, k))
    return pl.pallas_call(
        _inproj_kernel, grid=(B, T // tm),
        in_specs=[pl.BlockSpec((None, tm, D), lambda b, i: (b, i, 0)), modspec(0), modspec(1),
                  pl.BlockSpec((1, D), lambda b, i: (0, 0)),
                  pl.BlockSpec((D, IN_PAD), lambda b, i: (0, 0))],
        out_specs=[pl.BlockSpec((None, tm, n), lambda b, i: (b, i, 0)) for _, n in SEG.values()],
        out_shape=[jax.ShapeDtypeStruct((B, T, n), F32) for _, n in SEG.values()],
        compiler_params=_cp("parallel", "parallel"), name="inproj",
    )(x, mod, mod, norm_g.reshape(1, D), w_in_p)


def _prep_kernel(*refs, nt, tm):
    (pq_ref, pkv_ref, pkr_ref, gq_ref, gk_ref, gv_ref, gqp_ref, gkp_ref, gvp_ref,
     gqn_ref, gkn_ref, gvn_ref, gab_ref, qan_ref, wuq_ref, kvan_ref, wuk_ref, wuv_ref,
     qn_ref, kn_ref, cos_ref, sa_ref, sb_ref, cw_ref, alog_ref, dt_ref) = refs[:26]
    Q_ref, K_ref, VT_ref, q_ref, k_ref, v_ref, gb_ref = refs[-7:]
    i = pl.program_id(1)
    cos, sa, sb = cos_ref[...], sa_ref[...], sb_ref[...]

    def rope(xh):
        return xh * cos + pltpu.roll(xh, LANE - 8, 1) * sa + pltpu.roll(xh, 8, 1) * sb

    pq = pq_ref[...]
    qa = pq * lax.rsqrt(jnp.sum(pq * pq, axis=-1, keepdims=True) * (1.0 / MLA_Q_LORA) + EPS) * qan_ref[...]
    qall = jnp.dot(qa.astype(BF16), wuq_ref[...], preferred_element_type=F32)
    pkv = pkv_ref[...]
    kva = (pkv * lax.rsqrt(jnp.mean(pkv * pkv, axis=-1, keepdims=True) + EPS) * kvan_ref[...]).astype(BF16)
    kall = jnp.dot(kva, wuk_ref[...], preferred_element_type=F32)
    vt = lax.dot_general(wuv_ref[...], kva, NT, preferred_element_type=F32)
    ones_row = lax.broadcasted_iota(jnp.int32, vt.shape, 0) % HEAD_PAD == MLA_V
    VT_ref[...] = jnp.where(ones_row, 1.0, vt).astype(BF16)
    pkr = pkr_ref[...]
    for h in range(MLA_HEADS):
        sl = slice(h * HEAD_PAD, (h + 1) * HEAD_PAD)
        qh = qall[:, sl]
        qh = qh * lax.rsqrt(jnp.sum(qh * qh, axis=-1, keepdims=True) * (1.0 / MLA_QK) + EPS) * qn_ref[...]
        Q_ref[:, sl] = rope(qh).astype(BF16)
        kh = kall[:, sl] + pkr
        kh = kh * lax.rsqrt(jnp.sum(kh * kh, axis=-1, keepdims=True) * (1.0 / MLA_QK) + EPS) * kn_ref[...]
        K_ref[:, sl] = rope(kh).astype(BF16)

    rid = lax.broadcasted_iota(jnp.int32, (tm, GDN_HEADS * GDN_DK), 0)

    def conv_silu(u_ref, up_ref, un_ref, c0):
        u = u_ref[...]
        n = u.shape[1]
        prev_row = jnp.where(i > 0, up_ref[SUBLANE - 1:SUBLANE, :], 0.0)
        next_row = jnp.where(i < nt - 1, un_ref[0:1, :], 0.0)
        um = jnp.where(rid == 0, prev_row, pltpu.roll(u, 1, 0))
        up = jnp.where(rid == tm - 1, next_row, pltpu.roll(u, tm - 1, 0))
        y = um * cw_ref[0:1, c0:c0 + n] + u * cw_ref[1:2, c0:c0 + n] + up * cw_ref[2:3, c0:c0 + n]
        return _silu(y)

    cq = conv_silu(gq_ref, gqp_ref, gqn_ref, 0)
    ck = conv_silu(gk_ref, gkp_ref, gkn_ref, GDN_HEADS * GDN_DK)
    v_ref[...] = conv_silu(gv_ref, gvp_ref, gvn_ref, 2 * GDN_HEADS * GDN_DK)
    for h in range(GDN_HEADS):
        sl = slice(h * GDN_DK, (h + 1) * GDN_DK)
        qh = cq[:, sl]
        q_ref[:, sl] = qh * lax.rsqrt(jnp.sum(qh * qh, axis=-1, keepdims=True) + EPS) * (GDN_DK ** -0.5)
        kh = ck[:, sl]
        k_ref[:, sl] = kh * lax.rsqrt(jnp.sum(kh * kh, axis=-1, keepdims=True) + EPS)

    pre = gab_ref[...]
    lane = lax.broadcasted_iota(jnp.int32, pre.shape, 1) % LANE
    sp_in = pre + dt_ref[...]
    softplus = jnp.maximum(sp_in, 0.0) + jnp.log(1.0 + jnp.exp(-jnp.abs(sp_in)))
    g = -jnp.exp(alog_ref[...]) * softplus
    beta = 1.0 / (1.0 + jnp.exp(-pre))
    gb_ref[...] = jnp.where(lane < GDN_HEADS, g, jnp.where(lane < 2 * GDN_HEADS, beta, 0.0))


def prep(p, wts, rope_tabs, row_off, t_all, shared=None):
    pq, pkv, pkr, gq, gk, gv, gab = (p[k] for k in ("pq", "pkv", "pkr", "gq", "gk", "gv", "gab"))
    B, T, _ = pq.shape
    tm = min(256, T)
    assert row_off % tm == 0
    nt = T // tm
    tb = tm // SUBLANE
    nb = T // SUBLANE
    ro = row_off // tm
    cur = lambda n: pl.BlockSpec((None, tm, n), lambda b, i: (b, i, 0))
    dst = lambda n: pl.BlockSpec((None, tm, n), lambda b, i: (b, i + ro, 0))
    prv = lambda n: pl.BlockSpec((None, SUBLANE, n), lambda b, i: (b, jnp.maximum(i * tb - 1, 0), 0))
    nxt = lambda n: pl.BlockSpec((None, SUBLANE, n), lambda b, i: (b, jnp.minimum((i + 1) * tb, nb - 1), 0))
    full = lambda a: pl.BlockSpec(a.shape, lambda b, i: (0,) * a.ndim)
    tab = pl.BlockSpec((tm, LANE), lambda b, i: (i, 0))
    W = GDN_HEADS * GDN_DK
    outs = [("Q", MLA_HEADS * HEAD_PAD, BF16), ("K", MLA_HEADS * HEAD_PAD, BF16), ("VT", None, BF16),
            ("q", W, F32), ("k", W, F32), ("v", W, F32), ("gb", 2 * LANE, F32)]
    HP = MLA_HEADS * HEAD_PAD
    shared = [] if shared is None else [shared[n] for n, _, _ in outs[1:]]
    n_in = 26
    res = pl.pallas_call(
        functools.partial(_prep_kernel, nt=nt, tm=tm), grid=(B, nt),
        in_specs=[cur(256), cur(128), cur(128), cur(W), cur(W), cur(W), prv(W), prv(W), prv(W),
                  nxt(W), nxt(W), nxt(W), cur(256)] + [full(a) for a in wts[:7]] + [tab, tab, tab]
                 + [full(a) for a in wts[7:]] + [pl.BlockSpec(memory_space=pl.ANY)] * len(shared),
        out_specs=[cur(outs[0][1])] + [dst(n) if n else pl.BlockSpec((None, HP, tm), lambda b, i: (b, 0, i + ro))
                                       for _, n, _ in outs[1:]],
        out_shape=[jax.ShapeDtypeStruct((B, T, outs[0][1]), outs[0][2])]
                  + [jax.ShapeDtypeStruct((B, t_all, n) if n else (B, HP, t_all), dt) for _, n, dt in outs[1:]],
        input_output_aliases={n_in + k: 1 + k for k in range(len(shared))},
        compiler_params=_cp("parallel", "parallel"), name="prep",
    )(pq, pkv, pkr, gq, gk, gv, gq, gk, gv, gq, gk, gv, gab, *wts[:7], *rope_tabs, *wts[7:], *shared)
    return dict(zip([n for n, _, _ in outs], res))


def _attn_kernel(q_ref, k_ref, vt_ref, o_ref, sa_ref, sb_ref, *, ck, nk):
    q = q_ref[...]
    tq = q.shape[0]

    def scores(j):
        off = pl.multiple_of(j * ck, ck)
        return lax.dot_general(k_ref[pl.ds(off, ck), :], q, NT, preferred_element_type=F32)

    nv = MLA_V + 16

    def update(carry, s_ref, j):
        m, acc = carry
        s = s_ref[...]
        m_new = jnp.maximum(m, jnp.max(s, axis=0, keepdims=True))
        p = jnp.exp2(s - m_new).astype(BF16)
        off = pl.multiple_of(j * ck, ck)
        acc = jnp.exp2(m - m_new) * acc + jnp.dot(vt_ref[0:nv, pl.ds(off, ck)], p, preferred_element_type=F32)
        return m_new, acc

    def pair(jj, carry):
        j = 2 * jj
        sb_ref[...] = scores(j + 1)
        carry = update(carry, sa_ref, j)
        sa_ref[...] = scores(jnp.minimum(j + 2, nk - 1))
        return update(carry, sb_ref, j + 1)

    sa_ref[...] = scores(0)
    carry = (jnp.full((1, tq), -1e30, F32), jnp.zeros((nv, tq), F32))
    carry = lax.fori_loop(0, nk // 2, pair, carry)
    if nk % 2:
        carry = update(carry, sa_ref, nk - 1)
    acc = carry[1]
    o = acc[:MLA_V] / acc[MLA_V:MLA_V + 1]
    o_ref[...] = jnp.concatenate([o, jnp.zeros((HEAD_PAD - MLA_V, tq), F32)], axis=0).T.astype(o_ref.dtype)


def attention(Q, K, VT, Tk):
    B, Tq, _ = Q.shape
    tq = min(512, Tq)
    ck = next(c for c in (768, 512, 384, 256, 128) if Tk % c == 0)
    qo = pl.BlockSpec((None, tq, HEAD_PAD), lambda b, h, i: (b, i, h))
    return pl.pallas_call(
        functools.partial(_attn_kernel, ck=ck, nk=Tk // ck), grid=(B, MLA_HEADS, Tq // tq),
        in_specs=[qo, pl.BlockSpec((None, Tk, HEAD_PAD), lambda b, h, i: (b, 0, h)),
                  pl.BlockSpec((None, HEAD_PAD, Tk), lambda b, h, i: (b, h, 0))],
        out_specs=qo, out_shape=jax.ShapeDtypeStruct(Q.shape, BF16),
        scratch_shapes=[pltpu.VMEM((ck, tq), F32), pltpu.VMEM((ck, tq), F32)],
        compiler_params=_cp("parallel", "parallel", "parallel"), name="attention",
    )(Q, K, VT)


GDN_GROUP = 4


def _gdn_prep_kernel(q_ref, k_ref, v_ref, gb_ref, wq_ref, u_ref, qk_ref, kdt_ref, egl_ref, *, C, G):
    fwd = pl.program_id(1) == 0
    row = lax.broadcasted_iota(jnp.int32, (C, C), 0)
    col = lax.broadcasted_iota(jnp.int32, (C, C), 1)
    ahead = jnp.where(fwd, row - col, col - row)
    incl = ahead >= 0
    strict = ahead > 0
    incl16 = incl.astype(F32).astype(BF16)
    eye = (row == col).astype(F32)
    eye16 = eye.astype(BF16)
    dot = functools.partial(jnp.dot, preferred_element_type=F32)

    def terms(v):
        hi = v.astype(BF16)
        rest = v - hi.astype(F32)
        mid = rest.astype(BF16)
        return jnp.concatenate([hi, mid, (rest - mid.astype(F32)).astype(BF16)], axis=1)

    gbs = [gb_ref[g * C:(g + 1) * C, :] for g in range(G)]
    gterms = [terms(gb) for gb in gbs]
    fold = lambda a, axis: sum(jnp.split(a, 3, axis=axis)[1:], jnp.split(a, 3, axis=axis)[0])
    gcs = [fold(dot(incl16, t), 1) for t in gterms]
    gcts = [fold(lax.dot_general(t, incl16, (((0,), (1,)), ((), ())), preferred_element_type=F32), 0)
            for t in gterms]
    glasts = [jnp.where(fwd, gc[C - 1:C, :], gc[0:1, :]) for gc in gcs]
    for g in range(G):
        egl_ref[g] = jnp.broadcast_to(jnp.exp(glasts[g]), (SUBLANE, LANE))

    chains = [(g, h) for g in range(G) for h in range(GDN_HEADS)]
    hsl = lambda h: slice(h * GDN_DK, (h + 1) * GDN_DK)
    csl = lambda h: slice(h * C, (h + 1) * C)
    ld = lambda ref, g, h: ref[g * C:(g + 1) * C, hsl(h)]
    gcol = {c: gcs[c[0]][:, c[1]:c[1] + 1] for c in chains}
    bcol = {c: gbs[c[0]][:, GDN_HEADS + c[1]:GDN_HEADS + c[1] + 1] for c in chains}
    decay = {c: jnp.exp(jnp.where(incl, gcol[c] - gcts[c[0]][c[1]:c[1] + 1, :], -1e30)) for c in chains}
    k16 = {c: ld(k_ref, *c).astype(BF16) for c in chains}
    qkk = {c: lax.dot_general(jnp.concatenate([ld(q_ref, *c).astype(BF16), k16[c]], axis=0), k16[c], NT,
                              preferred_element_type=F32) for c in chains}
    a = {}
    for c in chains:
        g, h = c
        qk_ref[g, :, csl(h)] = (qkk[c][:C] * decay[c]).astype(BF16)
        a[c] = jnp.where(strict, qkk[c][C:] * decay[c], 0.0) * bcol[c]
    same = lambda s: (row >> s) == (col >> s)
    tinv = {c: eye - jnp.where(same(1), a[c], 0.0) for c in chains}
    for s in range(1, int(math.log2(C))):
        join = lambda v: jnp.where(same(s + 1), jnp.where(same(s), 0.0, v), 0.0)
        x16 = {c: tinv[c].astype(BF16) for c in chains}
        xl = {c: dot(x16[c], join(a[c]).astype(BF16)) for c in chains}
        xlx = {c: dot(xl[c].astype(BF16), x16[c]) for c in chains}
        tinv = {c: tinv[c] - xlx[c] for c in chains}
    eg = {c: jnp.exp(gcol[c]) for c in chains}
    wu = {c: dot(tinv[c].astype(BF16),
                 jnp.concatenate([ld(k_ref, *c) * (bcol[c] * eg[c]), ld(v_ref, *c) * bcol[c]], axis=1).astype(BF16))
          for c in chains}
    kdt = {c: lax.dot_general((ld(k_ref, *c) * jnp.exp(glasts[c[0]][:, c[1]:c[1] + 1] - gcol[c])).astype(BF16), eye16,
                              TN, preferred_element_type=F32) for c in chains}
    for c in chains:
        g, h = c
        wq_ref[g, 0:C, hsl(h)] = wu[c][:, :GDN_DK].astype(BF16)
        wq_ref[g, C:2 * C, hsl(h)] = (ld(q_ref, *c) * eg[c]).astype(BF16)
        u_ref[g, :, hsl(h)] = wu[c][:, GDN_DK:]
        kdt_ref[g, :, csl(h)] = kdt[c].astype(BF16)


def _gdn_rec_kernel(*refs, C, G, B):
    ins = (refs[0:5], refs[5:10])
    outs = refs[10:12]
    s_ref = refs[12]
    dot = functools.partial(jnp.dot, preferred_element_type=F32)

    @pl.when(pl.program_id(0) == 0)
    def _():
        s_ref[...] = jnp.zeros_like(s_ref)

    hsl = lambda h: slice(h * GDN_DK, (h + 1) * GDN_DK)
    csl = lambda h: slice(h * C, (h + 1) * C)
    for step in range(G):
        chains = [(d, b, h, step if d == 0 else G - 1 - step)
                  for d in range(2) for b in range(B) for h in range(GDN_HEADS)]
        S = {c: s_ref[c[0], c[1], c[2]] for c in chains}
        r = {(d, b, h, g): dot(ins[d][0][b, g, :, hsl(h)], S[(d, b, h, g)].astype(BF16))
             for (d, b, h, g) in chains}
        vn = {(d, b, h, g): (ins[d][1][b, g, :, hsl(h)] - r[(d, b, h, g)][:C]).astype(BF16) for (d, b, h, g) in chains}
        o = {(d, b, h, g): dot(ins[d][2][b, g, :, csl(h)], vn[(d, b, h, g)]) for (d, b, h, g) in chains}
        upd = {(d, b, h, g): dot(ins[d][3][b, g, :, csl(h)], vn[(d, b, h, g)]) for (d, b, h, g) in chains}
        for c in chains:
            d, b, h, g = c
            outs[d][b, g, :, hsl(h)] = r[c][C:] + o[c]
            s_ref[d, b, h] = S[c] * ins[d][4][b, g, 0:1, h:h + 1] + upd[c]


def gdn_scan(q, k, v, gb, n_ctx):
    B, Tt, W = q.shape
    C, G = GDN_CHUNK, GDN_GROUP
    n = Tt // C
    ng = n // G
    ncg = n_ctx // (C * G)
    assert n % G == 0 and n_ctx % (C * G) == 0
    tok = pl.BlockSpec((None, G * C, W), lambda b, d, s: (b, s, 0))
    per_chunk = lambda r, w, idx: pl.BlockSpec((None, None, G, r, w), idx)
    shapes = [((2 * C, W), BF16), ((C, W), F32), ((C, GDN_HEADS * C), BF16), ((GDN_DK, GDN_HEADS * C), BF16),
              ((SUBLANE, LANE), F32)]
    nat = lambda b, d, s: (b, d, s, 0, 0)
    mid = pl.pallas_call(
        functools.partial(_gdn_prep_kernel, C=C, G=G), grid=(B, 2, ng),
        in_specs=[tok, tok, tok, pl.BlockSpec((None, G * C, LANE), lambda b, d, s: (b, s, d))],
        out_specs=[per_chunk(r, w, nat) for (r, w), _ in shapes],
        out_shape=[jax.ShapeDtypeStruct((B, 2, n, r, w), dt) for (r, w), dt in shapes],
        compiler_params=_cp("parallel", "parallel", "parallel"), name="gdn_prep",
    )(q, k, v, gb)

    bwd = lambda s: jnp.where(s < ncg, ncg - 1 - s, ng + ncg - 1 - s)
    both = lambda r, w, d: pl.BlockSpec((B, None, G, r, w), (lambda s: (0, 0, s, 0, 0)) if d == 0
                                        else (lambda s: (0, 1, bwd(s), 0, 0)))
    o_spec = lambda d: pl.BlockSpec((B, G, C, W), (lambda s: (0, s, 0, 0)) if d == 0 else (lambda s: (0, bwd(s), 0, 0)))
    o_f, o_b = pl.pallas_call(
        functools.partial(_gdn_rec_kernel, C=C, G=G, B=B), grid=(ng,),
        in_specs=[both(r, w, d) for d in range(2) for (r, w), _ in shapes],
        out_specs=[o_spec(0), o_spec(1)],
        out_shape=[jax.ShapeDtypeStruct((B, n, C, W), F32)] * 2,
        scratch_shapes=[pltpu.VMEM((2, B, GDN_HEADS, GDN_DK, GDN_DV), F32)],
        compiler_params=_cp("arbitrary"), name="gdn_rec",
    )(*mid, *mid)
    return o_f.reshape(B, Tt, W), o_b.reshape(B, Tt, W)


def _mixout_kernel(att_ref, of_ref, ob_ref, z_ref, u_ref, up_ref, un_ref, x_ref, g1_ref, sh2_ref, sc2_ref,
                   gng_ref, wa_ref, wg_ref, wp_ref, wbd_ref, ps_ref, n2g_ref, wr_ref,
                   x1_ref, h2_ref, aff_ref, *, tm, T, nt):
    i = pl.program_id(1)
    o = of_ref[...] + ob_ref[...]
    z = z_ref[...]
    parts = []
    for h in range(GDN_HEADS):
        sl = slice(h * GDN_DV, (h + 1) * GDN_DV)
        oh = o[:, sl]
        oh = oh * lax.rsqrt(jnp.mean(oh * oh, axis=-1, keepdims=True) + EPS) * gng_ref[...]
        parts.append((oh * _silu(z[:, sl])).astype(BF16))
    gdn = jnp.concatenate(parts, axis=1)

    u = u_ref[...]
    halo = SUBLANE
    ext = jnp.concatenate([jnp.where(i > 0, up_ref[...], 0.0), u, jnp.where(i < nt - 1, un_ref[...], 0.0)], axis=0)
    n_ext = tm + 2 * halo
    back = lambda a, s: pltpu.roll(a, s, 0)
    ahead = lambda a, s: pltpu.roll(a, n_ext - s, 0)
    s2 = ext + back(ext, 1)
    s4 = back(s2, 1) + ahead(s2, 1)
    s8 = back(s4, 2) + ahead(s4, 2)
    s16 = back(s8, 4) + ahead(s8, 4)
    t = i * tm + lax.broadcasted_iota(jnp.int32, (tm, 1), 0)
    lane = lax.broadcasted_iota(jnp.int32, (tm, POOL_WIDTH), 1)
    mean = None
    for gi, (win, sw) in reversed(list(enumerate(zip(POOL_WINDOWS, (s2, s4, s8, s16))))):
        lo = jnp.maximum(t - win // 2, 0)
        hi = jnp.minimum(t - win // 2 + win, T)
        m = sw[halo:halo + tm, :] / (hi - lo).astype(F32)
        mean = m if mean is None else jnp.where(lane < (gi + 1) * POOL_GROUP, m, mean)
    yp = jnp.dot((mean - u).astype(BF16), wbd_ref[...], preferred_element_type=F32) * ps_ref[...]

    y = (jnp.dot(att_ref[...], wa_ref[...], preferred_element_type=F32)
         + jnp.dot(gdn, wg_ref[...], preferred_element_type=F32)
         + jnp.dot(yp.astype(BF16), wp_ref[...], preferred_element_type=F32))
    x1 = x_ref[...] + g1_ref[...] * y
    x1_ref[...] = x1
    h2 = x1 * lax.rsqrt(jnp.mean(x1 * x1, axis=-1, keepdims=True) + EPS) * n2g_ref[...]
    h2 = h2 * (1.0 + sc2_ref[...]) + sh2_ref[...]
    h2_ref[...] = h2.astype(BF16)
    lg = lax.dot_general(wr_ref[...], h2, NT, precision=HI, preferred_element_type=F32)
    e = jnp.exp(lg - jnp.max(lg, axis=0, keepdims=True))
    aff_ref[...] = e / jnp.sum(e, axis=0, keepdims=True)


def mixout(att, o, o_off, z, u, x, mod, wts):
    B, T, D = x.shape
    tm = min(256, T)
    nt = T // tm
    tb = tm // SUBLANE
    nb = T // SUBLANE
    assert o_off % tm == 0
    ob = o_off // tm
    W = GDN_HEADS * GDN_DV
    cur = lambda n: pl.BlockSpec((None, tm, n), lambda b, i: (b, i, 0))
    odir = pl.BlockSpec((None, tm, W), lambda b, i: (b, i + ob, 0))
    modspec = lambda k: pl.BlockSpec((None, 1, D), lambda b, i: (b, 0, k))
    full = lambda a: pl.BlockSpec(a.shape, lambda b, i: (0,) * a.ndim)
    return pl.pallas_call(
        functools.partial(_mixout_kernel, tm=tm, T=T, nt=nt), grid=(B, nt),
        in_specs=[cur(MLA_HEADS * HEAD_PAD), odir, odir, cur(W), cur(POOL_WIDTH),
                  pl.BlockSpec((None, SUBLANE, POOL_WIDTH), lambda b, i: (b, jnp.maximum(i * tb - 1, 0), 0)),
                  pl.BlockSpec((None, SUBLANE, POOL_WIDTH), lambda b, i: (b, jnp.minimum((i + 1) * tb, nb - 1), 0)),
                  cur(D), modspec(2), modspec(3), modspec(4)] + [full(a) for a in wts],
        out_specs=[cur(D), cur(D), pl.BlockSpec((None, N_EXPERTS, tm), lambda b, i: (b, 0, i))],
        out_shape=[jax.ShapeDtypeStruct((B, T, D), F32), jax.ShapeDtypeStruct((B, T, D), BF16),
                   jax.ShapeDtypeStruct((B, N_EXPERTS, T), F32)],
        compiler_params=_cp("parallel", "parallel"), name="mixout",
    )(att, o[0], o[1], z, u, u, u, x, mod, mod, mod, *wts)


MOE_SUB = 256
MOE_SLOTS = 128


def _route_kernel(aff_ref, gate_ref, slot_ref, starts_ref, *, cap, T):
    aff = aff_ref[...]

    def body(it, res):
        cand = res | jnp.left_shift(jnp.int32(1), 30 - it)
        cnt = jnp.sum((aff >= pltpu.bitcast(cand, F32)).astype(jnp.int32), axis=-1, keepdims=True)
        return jnp.where(cnt >= cap, cand, res)

    bits = lax.fori_loop(0, 31, body, jnp.zeros((N_EXPERTS, 1), jnp.int32))
    thr = pltpu.bitcast(bits, F32)
    above = pltpu.bitcast(bits + 1, F32)
    n_gt = jnp.sum((aff >= above).astype(jnp.int32), axis=-1, keepdims=True)
    need = (cap - n_gt).astype(F32)
    upper = (lax.broadcasted_iota(jnp.int32, (LANE, LANE), 0)
             < lax.broadcasted_iota(jnp.int32, (LANE, LANE), 1)).astype(BF16)
    seen = jnp.zeros((N_EXPERTS, 1), F32)
    taken = jnp.zeros((N_EXPERTS, 1), F32)
    lane = lax.broadcasted_iota(jnp.int32, (N_EXPERTS, LANE), 1)
    starts = jnp.zeros((N_EXPERTS, LANE), jnp.int32)
    per_sub = MOE_SUB // LANE
    for j in range(T // LANE):
        if j % per_sub == 0:
            starts = jnp.where(lane == j // per_sub, taken.astype(jnp.int32), starts)
        sl = slice(j * LANE, (j + 1) * LANE)
        aj = aff[:, sl]
        eq = jnp.where(aj >= thr, jnp.where(aj < above, 1.0, 0.0), 0.0)
        rank = jnp.dot(eq.astype(BF16), upper, preferred_element_type=F32) + seen
        sel = jnp.where(aj >= above, 1.0, jnp.where(rank < need, eq, 0.0))
        gate_ref[:, sl] = sel * aj
        slot = jnp.dot(sel.astype(BF16), upper, preferred_element_type=F32) + taken
        slot_ref[:, sl] = jnp.where(sel > 0.0, slot, -1.0)
        seen = seen + jnp.sum(eq, axis=-1, keepdims=True)
        taken = taken + jnp.sum(sel, axis=-1, keepdims=True)
    starts_ref[...] = jnp.where(lane == T // MOE_SUB, taken.astype(jnp.int32), starts)


def route(aff, cap):
    B, E, T = aff.shape
    assert T % MOE_SUB == 0 and T // MOE_SUB < LANE
    spec = pl.BlockSpec((None, E, T), lambda b: (b, 0, 0))
    return pl.pallas_call(
        functools.partial(_route_kernel, cap=cap, T=T), grid=(B,), in_specs=[spec],
        out_specs=[spec, spec, pl.BlockSpec((None, E, LANE), lambda b: (b, 0, 0))],
        out_shape=[jax.ShapeDtypeStruct(aff.shape, F32), jax.ShapeDtypeStruct(aff.shape, F32),
                   jax.ShapeDtypeStruct((B, E, LANE), jnp.int32)],
        compiler_params=_cp("parallel"), name="route",
    )(aff)


def _slot_blocks(starts_ref, b, e, sub, R):
    s0 = starts_ref[b, e, sub]
    s1 = starts_ref[b, e, sub + 1]
    return s0 // R, (s1 + R - 1) // R


def _moe_ffn_kernel(starts_ref, h_ref, slot_ref, wg_ref, wu_ref, wd_ref, y_ref, xs_ref, *, n_sub, R):
    b, e, j = pl.program_id(0), pl.program_id(1), pl.program_id(2)

    @pl.when(j == 0)
    def _():
        xs_ref[...] = jnp.zeros_like(xs_ref)

    rows = lax.broadcasted_iota(jnp.int32, (R, 1), 0)
    for sub in range(n_sub):
        tsl = slice(sub * MOE_SUB, (sub + 1) * MOE_SUB)
        h = h_ref[tsl, :]
        srow = slot_ref[:, tsl]

        def gather(i, carry):
            base = pl.multiple_of(i * R, R)
            onehot = jnp.where(srow == (base + rows).astype(F32), 1.0, 0.0).astype(BF16)
            xs_ref[pl.ds(base, R), :] += jnp.dot(onehot, h, preferred_element_type=F32)
            return carry

        lax.fori_loop(*_slot_blocks(starts_ref, b, e, j * n_sub + sub, R), gather, 0)

    @pl.when(j == pl.num_programs(2) - 1)
    def _():
        xs = xs_ref[...].astype(BF16)
        a = jnp.dot(xs, wg_ref[...], preferred_element_type=F32)
        hid = (_silu(a) * jnp.dot(xs, wu_ref[...], preferred_element_type=F32)).astype(BF16)
        y_ref[...] = jnp.dot(hid, wd_ref[...], preferred_element_type=F32).astype(BF16)


def _moe_combine_kernel(starts_ref, y_ref, slot_ref, gate_ref, x1_ref, g2_ref, o_ref, acc_ref, *, n_sub, R):
    b, j, e = pl.program_id(0), pl.program_id(1), pl.program_id(2)

    @pl.when(e == 0)
    def _():
        acc_ref[...] = jnp.zeros_like(acc_ref)

    lane = lax.broadcasted_iota(jnp.int32, slot_ref.shape, 1)
    scol = jnp.sum(jnp.where(lane == e, slot_ref[...], 0.0), axis=-1, keepdims=True)
    gcol = jnp.sum(jnp.where(lane == e, gate_ref[...], 0.0), axis=-1, keepdims=True)
    cols = lax.broadcasted_iota(jnp.int32, (1, R), 1)
    for sub in range(n_sub):
        tsl = slice(sub * MOE_SUB, (sub + 1) * MOE_SUB)
        sc, gc = scol[tsl], gcol[tsl]

        def scatter(i, carry):
            base = pl.multiple_of(i * R, R)
            onehot = jnp.where(sc == (base + cols).astype(F32), 1.0, 0.0).astype(BF16)
            acc_ref[tsl, :] += gc * jnp.dot(onehot, y_ref[pl.ds(base, R), :], preferred_element_type=F32)
            return carry

        lax.fori_loop(*_slot_blocks(starts_ref, b, e, j * n_sub + sub, R), scatter, 0)

    @pl.when(e == N_EXPERTS - 1)
    def _():
        o_ref[...] = x1_ref[...] + g2_ref[...] * acc_ref[...]


def moe(h2, routed, x1, mod, wg, wu, wd, cap):
    gate, slot, starts = routed
    B, T, D = x1.shape
    E, F = N_EXPERTS, wg.shape[-1]
    R = min(MOE_SLOTS, cap)
    assert cap % R == 0
    tt = min(2048, T)
    n_sub = tt // MOE_SUB
    y = pl.pallas_call(
        functools.partial(_moe_ffn_kernel, n_sub=n_sub, R=R),
        grid_spec=pltpu.PrefetchScalarGridSpec(
            num_scalar_prefetch=1, grid=(B, E, T // tt),
            in_specs=[pl.BlockSpec((None, tt, D), lambda b, e, j, st: (b, j, 0)),
                      pl.BlockSpec((None, None, 1, tt), lambda b, e, j, st: (b, e, 0, j)),
                      pl.BlockSpec((None, D, F), lambda b, e, j, st: (e, 0, 0)),
                      pl.BlockSpec((None, D, F), lambda b, e, j, st: (e, 0, 0)),
                      pl.BlockSpec((None, F, D), lambda b, e, j, st: (e, 0, 0))],
            out_specs=pl.BlockSpec((None, None, cap, D), lambda b, e, j, st: (b, e, 0, 0)),
            scratch_shapes=[pltpu.VMEM((cap, D), F32)]),
        out_shape=jax.ShapeDtypeStruct((B, E, cap, D), BF16),
        compiler_params=_cp("parallel", "parallel", "arbitrary"), name="moe_ffn",
    )(starts, h2, slot.reshape(B, E, 1, T), wg, wu, wd)
    tc = min(1024, T)
    tok = lambda n: pl.BlockSpec((None, tc, n), lambda b, j, e, st: (b, j, 0))
    return pl.pallas_call(
        functools.partial(_moe_combine_kernel, n_sub=tc // MOE_SUB, R=R),
        grid_spec=pltpu.PrefetchScalarGridSpec(
            num_scalar_prefetch=1, grid=(B, T // tc, E),
            in_specs=[pl.BlockSpec((None, None, cap, D), lambda b, j, e, st: (b, e, 0, 0)),
                      tok(E), tok(E), tok(D), pl.BlockSpec((None, 1, D), lambda b, j, e, st: (b, 0, 5))],
            out_specs=tok(D),
            scratch_shapes=[pltpu.VMEM((tc, D), F32)]),
        out_shape=jax.ShapeDtypeStruct((B, T, D), F32),
        compiler_params=_cp("parallel", "parallel", "arbitrary"), name="moe_combine",
    )(starts, y, jnp.swapaxes(slot, 1, 2), jnp.swapaxes(gate, 1, 2), x1, mod)


def _in_cols():
    src = np.full((IN_PAD,), -1, np.int64)
    splits = (MLA_Q_LORA, MLA_KV_LORA, MLA_ROPE, 512, 512, 512, 512, 2 * GDN_HEADS, 2 * GDN_HEADS, POOL_WIDTH)
    o = np.concatenate([[0], np.cumsum(splits)])
    put = lambda name, at, lo, n: src.__setitem__(slice(SEG[name][0] + at, SEG[name][0] + at + n), np.arange(lo, lo + n))
    put("pq", 0, o[0], MLA_Q_LORA)
    put("pkv", 0, o[1], MLA_KV_LORA)
    put("pkr", MLA_NOPE, o[2], MLA_ROPE)
    for name, k in (("gq", 3), ("gk", 4), ("gv", 5), ("gz", 6)):
        put(name, 0, o[k], 512)
    for d in range(2):
        put("gab", d * LANE, o[7] + d * GDN_HEADS, GDN_HEADS)
        put("gab", d * LANE + GDN_HEADS, o[8] + d * GDN_HEADS, GDN_HEADS)
    put("pool", 0, o[9], POOL_WIDTH)
    return src


def _take_cols(w, src, axis):
    idx = jnp.asarray(np.maximum(src, 0), jnp.int32)
    mask = jnp.asarray(src >= 0)
    shape = [1] * w.ndim
    shape[axis] = -1
    return jnp.where(mask.reshape(shape), jnp.take(w, idx, axis=axis), 0.0)


def _head_pad_src(per_head, lo, n):
    src = np.full((MLA_HEADS * HEAD_PAD,), -1, np.int64)
    for h in range(MLA_HEADS):
        src[h * HEAD_PAD:h * HEAD_PAD + n] = h * per_head + lo + np.arange(n)
    return src


def _rope_tables(T, rotate):
    cos = np.ones((T, LANE), np.float32)
    sa = np.zeros((T, LANE), np.float32)
    sb = np.zeros((T, LANE), np.float32)
    if rotate:
        n_freq = MLA_ROPE // 4
        inv = ROPE_THETA ** (-np.arange(n_freq, dtype=np.float64) / n_freq)
        pos_r = np.repeat(np.arange(T // GRID_W, dtype=np.float64), GRID_W)
        pos_c = np.tile(np.arange(GRID_W, dtype=np.float64), T // GRID_W)
        for base, pos in ((MLA_NOPE, pos_r), (MLA_NOPE + 2 * n_freq, pos_c)):
            ang = pos[:, None] * inv[None, :]
            c, s = np.cos(ang), np.sin(ang)
            cos[:, base:base + n_freq] = c
            cos[:, base + n_freq:base + 2 * n_freq] = c
            sa[:, base:base + n_freq] = -s
            sb[:, base + n_freq:base + 2 * n_freq] = s
    return jnp.asarray(cos), jnp.asarray(sa), jnp.asarray(sb)


def _lane_vec(vals_by_dir, at):
    v = jnp.zeros((2, LANE), F32).at[:, at:at + GDN_HEADS].set(vals_by_dir)
    return v.reshape(1, 2 * LANE)


def kernel(x, c, ctx, c_ctx, ada_w, ada_b, norm1_g, norm2_g, w_in, mla_q_a_norm, mla_w_uq, mla_kv_a_norm, mla_w_ukv, mla_q_norm, mla_k_norm, gdn_conv_w, gdn_a_log, gdn_dt_bias, gdn_norm_g, pool_w, pool_scale, w_out, moe_router, moe_w_gate, moe_w_up, moe_w_down):
    B, T, D = x.shape
    Tc = ctx.shape[1]
    L = ada_w.shape[0]
    cvec = jnp.concatenate([c, c_ctx[None, :], jnp.zeros((SUBLANE - B - 1, D), F32)], axis=0)
    mod = ada_mod(cvec, ada_w, ada_b)
    rope_lat = _rope_tables(T, True)
    rope_ctx = _rope_tables(Tc, False)
    in_src = _in_cols()
    uq_src = _head_pad_src(MLA_QK, 0, MLA_QK)
    uk_src = _head_pad_src(MLA_NOPE + MLA_V, 0, MLA_NOPE)
    uv_src = _head_pad_src(MLA_NOPE + MLA_V, MLA_NOPE, MLA_V)
    att_src = _head_pad_src(MLA_V, 0, MLA_V)
    pad_to = lambda v, n: jnp.pad(v, (0, n - v.shape[0])).reshape(1, n)

    xc = ctx
    for l in range(L):
        need_ctx = l < L - 1
        mod_lat = mod[l, :B].reshape(B, 1, ADA_CHUNKS * D)
        mod_ctx = jnp.broadcast_to(mod[l, B].reshape(1, 1, ADA_CHUNKS * D), (B, 1, ADA_CHUNKS * D))
        w_in_p = _take_cols(w_in[l], in_src, 1).astype(BF16)
        prep_w = (
            pad_to(mla_q_a_norm[l], 256),
            jnp.pad(_take_cols(mla_w_uq[l], uq_src, 1), ((0, 256 - MLA_Q_LORA), (0, 0))).astype(BF16),
            mla_kv_a_norm[l].reshape(1, MLA_KV_LORA),
            _take_cols(mla_w_ukv[l], uk_src, 1).astype(BF16),
            _take_cols(mla_w_ukv[l], uv_src, 1).T.astype(BF16),
            pad_to(mla_q_norm[l] * (MLA_QK ** -0.5 * math.log2(math.e)), HEAD_PAD),
            pad_to(mla_k_norm[l], HEAD_PAD),
            gdn_conv_w[l],
            _lane_vec(gdn_a_log[l], 0),
            _lane_vec(gdn_dt_bias[l], 0),
        )
        wo = w_out[l]
        n_att = MLA_HEADS * MLA_V
        n_gdn = GDN_HEADS * GDN_DV
        wbd = jnp.zeros((POOL_WIDTH, POOL_WIDTH), F32)
        for gi in range(len(POOL_WINDOWS)):
            wbd = wbd.at[gi * POOL_GROUP:(gi + 1) * POOL_GROUP, gi * POOL_GROUP:(gi + 1) * POOL_GROUP].set(pool_w[l, gi])
        mix_w = (
            gdn_norm_g[l].reshape(1, GDN_DV),
            _take_cols(wo[:n_att], att_src, 0).astype(BF16),
            wo[n_att:n_att + n_gdn].astype(BF16),
            wo[n_att + n_gdn:].astype(BF16),
            wbd.astype(BF16),
            pool_scale[l].reshape(1, POOL_WIDTH),
            norm2_g[l].reshape(1, D),
            moe_router[l].T,
        )
        wg, wu, wd = moe_w_gate[l].astype(BF16), moe_w_up[l].astype(BF16), moe_w_down[l].astype(BF16)

        p_lat = dict(zip(SEG, inproj(x, mod_lat, norm1_g[l], w_in_p)))
        p_ctx = dict(zip(SEG, inproj(xc, mod_ctx, norm1_g[l], w_in_p)))
        a_ctx = prep(p_ctx, prep_w, rope_ctx, 0, Tc + T)
        a_lat = prep(p_lat, prep_w, rope_lat, Tc, Tc + T, shared=a_ctx)
        att_l = attention(a_lat["Q"], a_lat["K"], a_lat["VT"], Tc + T)
        o_all = gdn_scan(a_lat["q"], a_lat["k"], a_lat["v"], a_lat["gb"], Tc)

        def channel_mix(att, o_off, p, xin, m):
            Tn = xin.shape[1]
            x1, h2, aff = mixout(att, o_all, o_off, p["gz"], p["pool"], xin, m, mix_w)
            cap = EC_CAPACITY_FACTOR * Tn // N_EXPERTS
            return moe(h2, route(aff, cap), x1, m, wg, wu, wd, cap)

        x = channel_mix(att_l, Tc, p_lat, x, mod_lat)
        if need_ctx:
            att_c = attention(a_ctx["Q"], a_lat["K"], a_lat["VT"], Tc)
            xc = channel_mix(att_c, 0, p_ctx, xc, mod_ctx)
    return x
```

```python
import functools
import math

import numpy as np
import jax
import jax.numpy as jnp
from jax import lax
from jax.experimental import pallas as pl
from jax.experimental.pallas import tpu as pltpu

F32 = jnp.float32
BF16 = jnp.bfloat16
HI = lax.Precision.HIGHEST

EPS = 1e-6
GRID_W = 64
ADA_CHUNKS = 6
MLA_HEADS = 4
MLA_NOPE = 64
MLA_ROPE = 32
MLA_QK = MLA_NOPE + MLA_ROPE
MLA_V = 64
MLA_Q_LORA = 192
MLA_KV_LORA = 128
ROPE_THETA = 10000.0
GDN_HEADS = 4
GDN_DK = 128
GDN_DV = 128
GDN_CHUNK = 64
POOL_WINDOWS = (2, 4, 8, 16)
POOL_GROUP = 64
POOL_WIDTH = POOL_GROUP * len(POOL_WINDOWS)
N_EXPERTS = 16
EC_CAPACITY_FACTOR = 2

LANE = 128
SUBLANE = 8
HEAD_PAD = 128
VMEM_LIMIT = 48 * 1024 * 1024

NT = (((1,), (1,)), ((), ()))
TN = (((0,), (0,)), ((), ()))

SEG = {}
_off = 0
for _name, _w in (("pq", 256), ("pkv", 128), ("pkr", 128), ("gq", 512), ("gk", 512), ("gv", 512),
                  ("gz", 512), ("gab", 256), ("pool", 256)):
    SEG[_name] = (_off, _w)
    _off += _w
IN_PAD = _off


def _cp(*dims):
    return pltpu.CompilerParams(dimension_semantics=dims, vmem_limit_bytes=VMEM_LIMIT)


def _silu(v):
    return v / (1.0 + jnp.exp(-v))


def _ada_kernel(c_ref, w_ref, b_ref, o_ref):
    s = _silu(c_ref[...])
    o_ref[...] = jnp.dot(s, w_ref[...], precision=HI, preferred_element_type=F32) + b_ref[...]


def ada_mod(cvec, ada_w, ada_b):
    L, D, N = ada_w.shape
    tn = N // 4
    return pl.pallas_call(
        _ada_kernel, grid=(L, N // tn),
        in_specs=[pl.BlockSpec((SUBLANE, D), lambda l, j: (0, 0)),
                  pl.BlockSpec((None, D, tn), lambda l, j: (l, 0, j)),
                  pl.BlockSpec((None, 1, tn), lambda l, j: (l, 0, j))],
        out_specs=pl.BlockSpec((None, SUBLANE, tn), lambda l, j: (l, 0, j)),
        out_shape=jax.ShapeDtypeStruct((L, SUBLANE, N), F32),
        compiler_params=_cp("parallel", "parallel"), name="ada_mod",
    )(cvec, ada_w, ada_b.reshape(L, 1, N))


def _inproj_kernel(x_ref, sh_ref, sc_ref, g_ref, w_ref, *out_refs):
    x = x_ref[...]
    h = x * lax.rsqrt(jnp.mean(x * x, axis=-1, keepdims=True) + EPS) * g_ref[...]
    hb = (h * (1.0 + sc_ref[...]) + sh_ref[...]).astype(BF16)
    for (off, n), o_ref in zip(SEG.values(), out_refs):
        o_ref[...] = jnp.dot(hb, w_ref[:, off:off + n], preferred_element_type=F32)


def inproj(x, mod, norm_g, w_in_p):
    B, T, D = x.shape
    tm = min(256, T)
    modspec = lambda k: pl.BlockSpec((None, 1, D), lambda b, i, k=k: (b, 0, k))
    return pl.pallas_call(
        _inproj_kernel, grid=(B, T // tm),
        in_specs=[pl.BlockSpec((None, tm, D), lambda b, i: (b, i, 0)), modspec(0), modspec(1),
                  pl.BlockSpec((1, D), lambda b, i: (0, 0)),
                  pl.BlockSpec((D, IN_PAD), lambda b, i: (0, 0))],
        out_specs=[pl.BlockSpec((None, tm, n), lambda b, i: (b, i, 0)) for _, n in SEG.values()],
        out_shape=[jax.ShapeDtypeStruct((B, T, n), F32) for _, n in SEG.values()],
        compiler_params=_cp("parallel", "parallel"), name="inproj",
    )(x, mod, mod, norm_g.reshape(1, D), w_in_p)


def _prep_kernel(*refs, nt, tm):
    (pq_ref, pkv_ref, pkr_ref, gq_ref, gk_ref, gv_ref, gqp_ref, gkp_ref, gvp_ref,
     gqn_ref, gkn_ref, gvn_ref, gab_ref, qan_ref, wuq_ref, kvan_ref, wuk_ref, wuv_ref,
     qn_ref, kn_ref, cos_ref, sa_ref, sb_ref, cw_ref, alog_ref, dt_ref) = refs[:26]
    Q_ref, K_ref, VT_ref, q_ref, k_ref, v_ref, gb_ref = refs[-7:]
    i = pl.program_id(1)
    cos, sa, sb = cos_ref[...], sa_ref[...], sb_ref[...]

    def rope(xh):
        return xh * cos + pltpu.roll(xh, LANE - 8, 1) * sa + pltpu.roll(xh, 8, 1) * sb

    pq = pq_ref[...]
    qa = pq * lax.rsqrt(jnp.sum(pq * pq, axis=-1, keepdims=True) * (1.0 / MLA_Q_LORA) + EPS) * qan_ref[...]
    qall = jnp.dot(qa.astype(BF16), wuq_ref[...], preferred_element_type=F32)
    pkv = pkv_ref[...]
    kva = (pkv * lax.rsqrt(jnp.mean(pkv * pkv, axis=-1, keepdims=True) + EPS) * kvan_ref[...]).astype(BF16)
    kall = jnp.dot(kva, wuk_ref[...], preferred_element_type=F32)
    vt = lax.dot_general(wuv_ref[...], kva, NT, preferred_element_type=F32)
    ones_row = lax.broadcasted_iota(jnp.int32, vt.shape, 0) % HEAD_PAD == MLA_V
    VT_ref[...] = jnp.where(ones_row, 1.0, vt).astype(BF16)
    pkr = pkr_ref[...]
    for h in range(MLA_HEADS):
        sl = slice(h * HEAD_PAD, (h + 1) * HEAD_PAD)
        qh = qall[:, sl]
        qh = qh * lax.rsqrt(jnp.sum(qh * qh, axis=-1, keepdims=True) * (1.0 / MLA_QK) + EPS) * qn_ref[...]
        Q_ref[:, sl] = rope(qh).astype(BF16)
        kh = kall[:, sl] + pkr
        kh = kh * lax.rsqrt(jnp.sum(kh * kh, axis=-1, keepdims=True) * (1.0 / MLA_QK) + EPS) * kn_ref[...]
        K_ref[:, sl] = rope(kh).astype(BF16)

    rid = lax.broadcasted_iota(jnp.int32, (tm, GDN_HEADS * GDN_DK), 0)

    def conv_silu(u_ref, up_ref, un_ref, c0):
        u = u_ref[...]
        n = u.shape[1]
        prev_row = jnp.where(i > 0, up_ref[SUBLANE - 1:SUBLANE, :], 0.0)
        next_row = jnp.where(i < nt - 1, un_ref[0:1, :], 0.0)
        um = jnp.where(rid == 0, prev_row, pltpu.roll(u, 1, 0))
        up = jnp.where(rid == tm - 1, next_row, pltpu.roll(u, tm - 1, 0))
        y = um * cw_ref[0:1, c0:c0 + n] + u * cw_ref[1:2, c0:c0 + n] + up * cw_ref[2:3, c0:c0 + n]
        return _silu(y)

    cq = conv_silu(gq_ref, gqp_ref, gqn_ref, 0)
    ck = conv_silu(gk_ref, gkp_ref, gkn_ref, GDN_HEADS * GDN_DK)
    v_ref[...] = conv_silu(gv_ref, gvp_ref, gvn_ref, 2 * GDN_HEADS * GDN_DK)
    for h in range(GDN_HEADS):
        sl = slice(h * GDN_DK, (h + 1) * GDN_DK)
        qh = cq[:, sl]
        q_ref[:, sl] = qh * lax.rsqrt(jnp.sum(qh * qh, axis=-1, keepdims=True) + EPS) * (GDN_DK ** -0.5)
        kh = ck[:, sl]
        k_ref[:, sl] = kh * lax.rsqrt(jnp.sum(kh * kh, axis=-1, keepdims=True) + EPS)

    pre = gab_ref[...]
    lane = lax.broadcasted_iota(jnp.int32, pre.shape, 1) % LANE
    sp_in = pre + dt_ref[...]
    softplus = jnp.maximum(sp_in, 0.0) + jnp.log(1.0 + jnp.exp(-jnp.abs(sp_in)))
    g = -jnp.exp(alog_ref[...]) * softplus
    beta = 1.0 / (1.0 + jnp.exp(-pre))
    gb_ref[...] = jnp.where(lane < GDN_HEADS, g, jnp.where(lane < 2 * GDN_HEADS, beta, 0.0))


def prep(p, wts, rope_tabs, row_off, t_all, shared=None):
    pq, pkv, pkr, gq, gk, gv, gab = (p[k] for k in ("pq", "pkv", "pkr", "gq", "gk", "gv", "gab"))
    B, T, _ = pq.shape
    tm = min(256, T)
    assert row_off % tm == 0
    nt = T // tm
    tb = tm // SUBLANE
    nb = T // SUBLANE
    ro = row_off // tm
    cur = lambda n: pl.BlockSpec((None, tm, n), lambda b, i: (b, i, 0))
    dst = lambda n: pl.BlockSpec((None, tm, n), lambda b, i: (b, i + ro, 0))
    prv = lambda n: pl.BlockSpec((None, SUBLANE, n), lambda b, i: (b, jnp.maximum(i * tb - 1, 0), 0))
    nxt = lambda n: pl.BlockSpec((None, SUBLANE, n), lambda b, i: (b, jnp.minimum((i + 1) * tb, nb - 1), 0))
    full = lambda a: pl.BlockSpec(a.shape, lambda b, i: (0,) * a.ndim)
    tab = pl.BlockSpec((tm, LANE), lambda b, i: (i, 0))
    W = GDN_HEADS * GDN_DK
    outs = [("Q", MLA_HEADS * HEAD_PAD, BF16), ("K", MLA_HEADS * HEAD_PAD, BF16), ("VT", None, BF16),
            ("q", W, F32), ("k", W, F32), ("v", W, F32), ("gb", 2 * LANE, F32)]
    HP = MLA_HEADS * HEAD_PAD
    shared = [] if shared is None else [shared[n] for n, _, _ in outs[1:]]
    n_in = 26
    res = pl.pallas_call(
        functools.partial(_prep_kernel, nt=nt, tm=tm), grid=(B, nt),
        in_specs=[cur(256), cur(128), cur(128), cur(W), cur(W), cur(W), prv(W), prv(W), prv(W),
                  nxt(W), nxt(W), nxt(W), cur(256)] + [full(a) for a in wts[:7]] + [tab, tab, tab]
                 + [full(a) for a in wts[7:]] + [pl.BlockSpec(memory_space=pl.ANY)] * len(shared),
        out_specs=[cur(outs[0][1])] + [dst(n) if n else pl.BlockSpec((None, HP, tm), lambda b, i: (b, 0, i + ro))
                                       for _, n, _ in outs[1:]],
        out_shape=[jax.ShapeDtypeStruct((B, T, outs[0][1]), outs[0][2])]
                  + [jax.ShapeDtypeStruct((B, t_all, n) if n else (B, HP, t_all), dt) for _, n, dt in outs[1:]],
        input_output_aliases={n_in + k: 1 + k for k in range(len(shared))},
        compiler_params=_cp("parallel", "parallel"), name="prep",
    )(pq, pkv, pkr, gq, gk, gv, gq, gk, gv, gq, gk, gv, gab, *wts[:7], *rope_tabs, *wts[7:], *shared)
    return dict(zip([n for n, _, _ in outs], res))


def _attn_kernel(q_ref, k_ref, vt_ref, o_ref, sa_ref, sb_ref, *, ck, nk):
    q = q_ref[...]
    tq = q.shape[0]

    def scores(j):
        return lax.dot_general(k_ref[j * ck:(j + 1) * ck, :], q, NT, preferred_element_type=F32)

    nv = MLA_V + 16

    def update(carry, s_ref, j):
        m, acc = carry
        s = s_ref[...]
        m_new = jnp.maximum(m, jnp.max(s, axis=0, keepdims=True))
        p = jnp.exp2(s - m_new).astype(BF16)
        acc = jnp.exp2(m - m_new) * acc + jnp.dot(vt_ref[0:nv, j * ck:(j + 1) * ck], p, preferred_element_type=F32)
        return m_new, acc

    bufs = (sa_ref, sb_ref)
    bufs[0][...] = scores(0)
    carry = (jnp.full((1, tq), -1e30, F32), jnp.zeros((nv, tq), F32))
    for j in range(nk):
        if j + 1 < nk:
            bufs[(j + 1) % 2][...] = scores(j + 1)
        carry = update(carry, bufs[j % 2], j)
    acc = carry[1]
    o = acc[:MLA_V] / acc[MLA_V:MLA_V + 1]
    o_ref[...] = jnp.concatenate([o, jnp.zeros((HEAD_PAD - MLA_V, tq), F32)], axis=0).T.astype(o_ref.dtype)


def attention(Q, K, VT, Tk):
    B, Tq, _ = Q.shape
    tq = min(512, Tq)
    ck = next(c for c in (768, 512, 384, 256, 128) if Tk % c == 0)
    qo = pl.BlockSpec((None, tq, HEAD_PAD), lambda b, h, i: (b, i, h))
    return pl.pallas_call(
        functools.partial(_attn_kernel, ck=ck, nk=Tk // ck), grid=(B, MLA_HEADS, Tq // tq),
        in_specs=[qo, pl.BlockSpec((None, Tk, HEAD_PAD), lambda b, h, i: (b, 0, h)),
                  pl.BlockSpec((None, HEAD_PAD, Tk), lambda b, h, i: (b, h, 0))],
        out_specs=qo, out_shape=jax.ShapeDtypeStruct(Q.shape, BF16),
        scratch_shapes=[pltpu.VMEM((ck, tq), F32), pltpu.VMEM((ck, tq), F32)],
        compiler_params=_cp("parallel", "parallel", "parallel"), name="attention",
    )(Q, K, VT)


GDN_GROUP = 4


def _gdn_prep_kernel(q_ref, k_ref, v_ref, gb_ref, wq_ref, u_ref, qk_ref, kdt_ref, egl_ref, *, C, G):
    H, DK = GDN_HEADS, GDN_DK
    fwd = pl.program_id(1) == 0
    dot = functools.partial(jnp.dot, preferred_element_type=F32)
    row = lax.broadcasted_iota(jnp.int32, (C, H * C), 0)
    lane = lax.broadcasted_iota(jnp.int32, (C, H * C), 1)
    col = lane & (C - 1)
    hmask = [(lane >> int(math.log2(C))) == h for h in range(H)]
    wide = lax.broadcasted_iota(jnp.int32, (C, H * DK), 1)
    kmask = [(wide >> int(math.log2(DK))) == h for h in range(H)]
    ahead = jnp.where(fwd, row - col, col - row)
    incl = ahead >= 0
    strict = ahead > 0
    eye = (row == col).astype(F32)
    r1 = lax.broadcasted_iota(jnp.int32, (C, C), 0)
    c1 = lax.broadcasted_iota(jnp.int32, (C, C), 1)
    incl16 = (jnp.where(fwd, r1 - c1, c1 - r1) >= 0).astype(F32).astype(BF16)
    eye16 = (r1 == c1).astype(F32).astype(BF16)

    def blockdiag(m, masks):
        return jnp.concatenate([jnp.where(mk, m, 0.0) for mk in masks], axis=0).astype(BF16)

    def per_head(cols, width):
        n = cols.shape[0]
        if width == LANE:
            return jnp.concatenate([jnp.broadcast_to(cols[:, h:h + 1], (n, LANE)) for h in range(H)], axis=1)
        low = lax.broadcasted_iota(jnp.int32, (n, LANE), 1) < width
        return jnp.concatenate([jnp.where(low, cols[:, h:h + 1], cols[:, h + 1:h + 2]) for h in range(0, H, 2)], axis=1)

    def terms(v):
        hi = v.astype(BF16)
        rest = v - hi.astype(F32)
        mid = rest.astype(BF16)
        return jnp.concatenate([hi, mid, (rest - mid.astype(F32)).astype(BF16)], axis=1)

    fold = lambda a, axis: sum(jnp.split(a, 3, axis=axis)[1:], jnp.split(a, 3, axis=axis)[0])
    chunks = range(G)
    rows = [slice(g * C, (g + 1) * C) for g in chunks]
    gb = [gb_ref[r, :] for r in rows]
    gterms = [terms(v) for v in gb]
    gc = [fold(dot(incl16, t), 1) for t in gterms]
    gct = [fold(lax.dot_general(t, incl16, (((0,), (1,)), ((), ())), preferred_element_type=F32), 0)
           for t in gterms]
    glast = [jnp.where(fwd, v[C - 1:C, :], v[0:1, :]) for v in gc]
    k16 = [k_ref[r, :].astype(BF16) for r in rows]
    qkk = [lax.dot_general(jnp.concatenate([q_ref[rows[g], :].astype(BF16), k16[g]], axis=0),
                           blockdiag(k_ref[rows[g], :], kmask), NT, preferred_element_type=F32)
           for g in chunks]
    a, tinv = [], []
    same = lambda s: (row >> s) == (col >> s)
    pairs = jnp.where(same(1), 1.0, 0.0)
    for g in chunks:
        egl_ref[g] = jnp.broadcast_to(jnp.exp(glast[g]), (SUBLANE, LANE))
        grow = jnp.concatenate([gct[g][h:h + 1, :] for h in range(H)], axis=1)
        decay = jnp.exp(jnp.where(incl, per_head(gc[g], C) - grow, -1e30))
        qk_ref[g] = (qkk[g][:C] * decay).astype(BF16)
        a.append(jnp.where(strict, qkk[g][C:] * decay, 0.0) * per_head(gb[g][:, H:], C))
        tinv.append(eye - a[g] * pairs)
    for s in range(1, int(math.log2(C))):
        join = jnp.where(same(s + 1), jnp.where(same(s), 0.0, 1.0), 0.0)
        x16 = [tinv[g].astype(BF16) for g in chunks]
        xl = [dot(x16[g], blockdiag(a[g] * join, hmask)) for g in chunks]
        xlx = [dot(xl[g].astype(BF16), blockdiag(tinv[g], hmask)) for g in chunks]
        tinv = [tinv[g] - xlx[g] for g in chunks]
    gcw = [per_head(gc[g], DK) for g in chunks]
    bw = [per_head(gb[g][:, H:], DK) for g in chunks]
    wu = [dot(tinv[g].astype(BF16),
              jnp.concatenate([blockdiag(k_ref[rows[g], :] * (bw[g] * jnp.exp(gcw[g])), kmask),
                               blockdiag(v_ref[rows[g], :] * bw[g], kmask)], axis=1)) for g in chunks]
    kdt = [lax.dot_general((k_ref[rows[g], :] * jnp.exp(per_head(glast[g], DK) - gcw[g])).astype(BF16), eye16,
                           TN, preferred_element_type=F32) for g in chunks]
    for g in chunks:
        wq_ref[g, 0:C, :] = wu[g][:, :H * DK].astype(BF16)
        wq_ref[g, C:2 * C, :] = (q_ref[rows[g], :] * jnp.exp(gcw[g])).astype(BF16)
        u_ref[g] = wu[g][:, H * DK:]
        for h in range(H):
            kdt_ref[g, :, h * C:(h + 1) * C] = kdt[g][h * DK:(h + 1) * DK].astype(BF16)


def _gdn_rec_kernel(*refs, C, G, B):
    ins = (refs[0:5], refs[5:10])
    outs = refs[10:12]
    s_ref = refs[12]
    dot = functools.partial(jnp.dot, preferred_element_type=F32)

    @pl.when(pl.program_id(0) == 0)
    def _():
        s_ref[...] = jnp.zeros_like(s_ref)

    hsl = lambda h: slice(h * GDN_DK, (h + 1) * GDN_DK)
    csl = lambda h: slice(h * C, (h + 1) * C)
    for step in range(G):
        chains = [(d, b, h, step if d == 0 else G - 1 - step)
                  for d in range(2) for b in range(B) for h in range(GDN_HEADS)]
        S = {c: s_ref[c[0], c[1], c[2]] for c in chains}
        r = {(d, b, h, g): dot(ins[d][0][b, g, :, hsl(h)], S[(d, b, h, g)].astype(BF16))
             for (d, b, h, g) in chains}
        vn = {(d, b, h, g): (ins[d][1][b, g, :, hsl(h)] - r[(d, b, h, g)][:C]).astype(BF16) for (d, b, h, g) in chains}
        o = {(d, b, h, g): dot(ins[d][2][b, g, :, csl(h)], vn[(d, b, h, g)]) for (d, b, h, g) in chains}
        upd = {(d, b, h, g): dot(ins[d][3][b, g, :, csl(h)], vn[(d, b, h, g)]) for (d, b, h, g) in chains}
        for c in chains:
            d, b, h, g = c
            outs[d][b, g, :, hsl(h)] = r[c][C:] + o[c]
            s_ref[d, b, h] = S[c] * ins[d][4][b, g, 0:1, h:h + 1] + upd[c]


def gdn_scan(q, k, v, gb, n_ctx):
    B, Tt, W = q.shape
    C, G = GDN_CHUNK, GDN_GROUP
    n = Tt // C
    ng = n // G
    ncg = n_ctx // (C * G)
    assert n % G == 0 and n_ctx % (C * G) == 0
    gp = next(c for c in (12, 8, 6, 4, 3, 2, 1) if n % c == 0)
    tok = pl.BlockSpec((None, gp * C, W), lambda b, d, s: (b, s, 0))
    shapes = [((2 * C, W), BF16), ((C, W), F32), ((C, GDN_HEADS * C), BF16), ((GDN_DK, GDN_HEADS * C), BF16),
              ((SUBLANE, LANE), F32)]
    nat = lambda b, d, s: (b, d, s, 0, 0)
    mid = pl.pallas_call(
        functools.partial(_gdn_prep_kernel, C=C, G=gp), grid=(B, 2, n // gp),
        in_specs=[tok, tok, tok, pl.BlockSpec((None, gp * C, LANE), lambda b, d, s: (b, s, d))],
        out_specs=[pl.BlockSpec((None, None, gp, r, w), nat) for (r, w), _ in shapes],
        out_shape=[jax.ShapeDtypeStruct((B, 2, n, r, w), dt) for (r, w), dt in shapes],
        compiler_params=_cp("parallel", "parallel", "parallel"), name="gdn_prep",
    )(q, k, v, gb)

    bwd = lambda s: jnp.where(s < ncg, ncg - 1 - s, ng + ncg - 1 - s)
    both = lambda r, w, d: pl.BlockSpec((B, None, G, r, w), (lambda s: (0, 0, s, 0, 0)) if d == 0
                                        else (lambda s: (0, 1, bwd(s), 0, 0)))
    o_spec = lambda d: pl.BlockSpec((B, G, C, W), (lambda s: (0, s, 0, 0)) if d == 0 else (lambda s: (0, bwd(s), 0, 0)))
    o_f, o_b = pl.pallas_call(
        functools.partial(_gdn_rec_kernel, C=C, G=G, B=B), grid=(ng,),
        in_specs=[both(r, w, d) for d in range(2) for (r, w), _ in shapes],
        out_specs=[o_spec(0), o_spec(1)],
        out_shape=[jax.ShapeDtypeStruct((B, n, C, W), F32)] * 2,
        scratch_shapes=[pltpu.VMEM((2, B, GDN_HEADS, GDN_DK, GDN_DV), F32)],
        compiler_params=_cp("arbitrary"), name="gdn_rec",
    )(*mid, *mid)
    return o_f.reshape(B, Tt, W), o_b.reshape(B, Tt, W)


def _mixout_kernel(att_ref, of_ref, ob_ref, z_ref, u_ref, up_ref, un_ref, x_ref, g1_ref, sh2_ref, sc2_ref,
                   gng_ref, wa_ref, wg_ref, wp_ref, wbd_ref, ps_ref, n2g_ref, wr_ref,
                   x1_ref, h2_ref, aff_ref, *, tm, T, nt):
    i = pl.program_id(1)
    o = of_ref[...] + ob_ref[...]
    z = z_ref[...]
    parts = []
    for h in range(GDN_HEADS):
        sl = slice(h * GDN_DV, (h + 1) * GDN_DV)
        oh = o[:, sl]
        oh = oh * lax.rsqrt(jnp.mean(oh * oh, axis=-1, keepdims=True) + EPS) * gng_ref[...]
        parts.append((oh * _silu(z[:, sl])).astype(BF16))
    gdn = jnp.concatenate(parts, axis=1)

    u = u_ref[...]
    halo = SUBLANE
    ext = jnp.concatenate([jnp.where(i > 0, up_ref[...], 0.0), u, jnp.where(i < nt - 1, un_ref[...], 0.0)], axis=0)
    n_ext = tm + 2 * halo
    back = lambda a, s: pltpu.roll(a, s, 0)
    ahead = lambda a, s: pltpu.roll(a, n_ext - s, 0)
    s2 = ext + back(ext, 1)
    s4 = back(s2, 1) + ahead(s2, 1)
    s8 = back(s4, 2) + ahead(s4, 2)
    s16 = back(s8, 4) + ahead(s8, 4)
    t = i * tm + lax.broadcasted_iota(jnp.int32, (tm, 1), 0)
    lane = lax.broadcasted_iota(jnp.int32, (tm, POOL_WIDTH), 1)
    mean = None
    for gi, (win, sw) in reversed(list(enumerate(zip(POOL_WINDOWS, (s2, s4, s8, s16))))):
        lo = jnp.maximum(t - win // 2, 0)
        hi = jnp.minimum(t - win // 2 + win, T)
        m = sw[halo:halo + tm, :] / (hi - lo).astype(F32)
        mean = m if mean is None else jnp.where(lane < (gi + 1) * POOL_GROUP, m, mean)
    yp = jnp.dot((mean - u).astype(BF16), wbd_ref[...], preferred_element_type=F32) * ps_ref[...]

    y = (jnp.dot(att_ref[...], wa_ref[...], preferred_element_type=F32)
         + jnp.dot(gdn, wg_ref[...], preferred_element_type=F32)
         + jnp.dot(yp.astype(BF16), wp_ref[...], preferred_element_type=F32))
    x1 = x_ref[...] + g1_ref[...] * y
    x1_ref[...] = x1
    h2 = x1 * lax.rsqrt(jnp.mean(x1 * x1, axis=-1, keepdims=True) + EPS) * n2g_ref[...]
    h2 = h2 * (1.0 + sc2_ref[...]) + sh2_ref[...]
    h2_ref[...] = h2.astype(BF16)
    lg = lax.dot_general(wr_ref[...], h2, NT, precision=HI, preferred_element_type=F32)
    e = jnp.exp(lg - jnp.max(lg, axis=0, keepdims=True))
    aff_ref[...] = e / jnp.sum(e, axis=0, keepdims=True)


def mixout(att, o, o_off, z, u, x, mod, wts):
    B, T, D = x.shape
    tm = min(256, T)
    nt = T // tm
    tb = tm // SUBLANE
    nb = T // SUBLANE
    assert o_off % tm == 0
    ob = o_off // tm
    W = GDN_HEADS * GDN_DV
    cur = lambda n: pl.BlockSpec((None, tm, n), lambda b, i: (b, i, 0))
    odir = pl.BlockSpec((None, tm, W), lambda b, i: (b, i + ob, 0))
    modspec = lambda k: pl.BlockSpec((None, 1, D), lambda b, i: (b, 0, k))
    full = lambda a: pl.BlockSpec(a.shape, lambda b, i: (0,) * a.ndim)
    return pl.pallas_call(
        functools.partial(_mixout_kernel, tm=tm, T=T, nt=nt), grid=(B, nt),
        in_specs=[cur(MLA_HEADS * HEAD_PAD), odir, odir, cur(W), cur(POOL_WIDTH),
                  pl.BlockSpec((None, SUBLANE, POOL_WIDTH), lambda b, i: (b, jnp.maximum(i * tb - 1, 0), 0)),
                  pl.BlockSpec((None, SUBLANE, POOL_WIDTH), lambda b, i: (b, jnp.minimum((i + 1) * tb, nb - 1), 0)),
                  cur(D), modspec(2), modspec(3), modspec(4)] + [full(a) for a in wts],
        out_specs=[cur(D), cur(D), pl.BlockSpec((None, N_EXPERTS, tm), lambda b, i: (b, 0, i))],
        out_shape=[jax.ShapeDtypeStruct((B, T, D), F32), jax.ShapeDtypeStruct((B, T, D), BF16),
                   jax.ShapeDtypeStruct((B, N_EXPERTS, T), F32)],
        compiler_params=_cp("parallel", "parallel"), name="mixout",
    )(att, o[0], o[1], z, u, u, u, x, mod, mod, mod, *wts)


MOE_SUB = 256
MOE_SLOTS = 128


def _route_kernel(aff_ref, gate_ref, slot_ref, starts_ref, *, cap, T):
    aff = aff_ref[...]

    def body(it, res):
        cand = res | jnp.left_shift(jnp.int32(1), 30 - it)
        cnt = jnp.sum((aff >= pltpu.bitcast(cand, F32)).astype(jnp.int32), axis=-1, keepdims=True)
        return jnp.where(cnt >= cap, cand, res)

    bits = lax.fori_loop(0, 31, body, jnp.zeros((N_EXPERTS, 1), jnp.int32))
    thr = pltpu.bitcast(bits, F32)
    above = pltpu.bitcast(bits + 1, F32)
    n_gt = jnp.sum((aff >= above).astype(jnp.int32), axis=-1, keepdims=True)
    need = (cap - n_gt).astype(F32)
    upper = (lax.broadcasted_iota(jnp.int32, (LANE, LANE), 0)
             < lax.broadcasted_iota(jnp.int32, (LANE, LANE), 1)).astype(BF16)
    seen = jnp.zeros((N_EXPERTS, 1), F32)
    taken = jnp.zeros((N_EXPERTS, 1), F32)
    lane = lax.broadcasted_iota(jnp.int32, (N_EXPERTS, LANE), 1)
    starts = jnp.zeros((N_EXPERTS, LANE), jnp.int32)
    per_sub = MOE_SUB // LANE
    for j in range(T // LANE):
        if j % per_sub == 0:
            starts = jnp.where(lane == j // per_sub, taken.astype(jnp.int32), starts)
        sl = slice(j * LANE, (j + 1) * LANE)
        aj = aff[:, sl]
        eq = jnp.where(aj >= thr, jnp.where(aj < above, 1.0, 0.0), 0.0)
        rank = jnp.dot(eq.astype(BF16), upper, preferred_element_type=F32) + seen
        sel = jnp.where(aj >= above, 1.0, jnp.where(rank < need, eq, 0.0))
        gate_ref[:, sl] = sel * aj
        slot = jnp.dot(sel.astype(BF16), upper, preferred_element_type=F32) + taken
        slot_ref[:, sl] = jnp.where(sel > 0.0, slot, -1.0)
        seen = seen + jnp.sum(eq, axis=-1, keepdims=True)
        taken = taken + jnp.sum(sel, axis=-1, keepdims=True)
    starts_ref[...] = jnp.where(lane == T // MOE_SUB, taken.astype(jnp.int32), starts)


def route(aff, cap):
    B, E, T = aff.shape
    assert T % MOE_SUB == 0 and T // MOE_SUB < LANE
    spec = pl.BlockSpec((None, E, T), lambda b: (b, 0, 0))
    return pl.pallas_call(
        functools.partial(_route_kernel, cap=cap, T=T), grid=(B,), in_specs=[spec],
        out_specs=[spec, spec, pl.BlockSpec((None, E, LANE), lambda b: (b, 0, 0))],
        out_shape=[jax.ShapeDtypeStruct(aff.shape, F32), jax.ShapeDtypeStruct(aff.shape, F32),
                   jax.ShapeDtypeStruct((B, E, LANE), jnp.int32)],
        compiler_params=_cp("parallel"), name="route",
    )(aff)


def _slot_blocks(starts_ref, b, e, sub, R):
    s0 = starts_ref[b, e, sub]
    s1 = starts_ref[b, e, sub + 1]
    return s0 // R, (s1 + R - 1) // R


def _moe_ffn_kernel(starts_ref, h_ref, slot_ref, wg_ref, wu_ref, wd_ref, y_ref, xs_ref, *, n_sub, R):
    b, e, j = pl.program_id(0), pl.program_id(1), pl.program_id(2)

    @pl.when(j == 0)
    def _():
        xs_ref[...] = jnp.zeros_like(xs_ref)

    rows = lax.broadcasted_iota(jnp.int32, (R, 1), 0)
    for sub in range(n_sub):
        tsl = slice(sub * MOE_SUB, (sub + 1) * MOE_SUB)
        h = h_ref[tsl, :]
        srow = slot_ref[:, tsl]

        def gather(i, carry):
            base = pl.multiple_of(i * R, R)
            onehot = jnp.where(srow == (base + rows).astype(F32), 1.0, 0.0).astype(BF16)
            xs_ref[pl.ds(base, R), :] += jnp.dot(onehot, h, preferred_element_type=F32)
            return carry

        lax.fori_loop(*_slot_blocks(starts_ref, b, e, j * n_sub + sub, R), gather, 0)

    @pl.when(j == pl.num_programs(2) - 1)
    def _():
        xs = xs_ref[...].astype(BF16)
        a = jnp.dot(xs, wg_ref[...].astype(BF16), preferred_element_type=F32)
        hid = (_silu(a) * jnp.dot(xs, wu_ref[...].astype(BF16), preferred_element_type=F32)).astype(BF16)
        y_ref[...] = jnp.dot(hid, wd_ref[...].astype(BF16), preferred_element_type=F32).astype(BF16)


def _moe_combine_kernel(starts_ref, y_ref, slot_ref, gate_ref, x1_ref, g2_ref, o_ref, acc_ref, *, n_sub, R):
    b, j, e = pl.program_id(0), pl.program_id(1), pl.program_id(2)

    @pl.when(e == 0)
    def _():
        acc_ref[...] = jnp.zeros_like(acc_ref)

    lane = lax.broadcasted_iota(jnp.int32, slot_ref.shape, 1)
    scol = jnp.sum(jnp.where(lane == e, slot_ref[...], 0.0), axis=-1, keepdims=True)
    gcol = jnp.sum(jnp.where(lane == e, gate_ref[...], 0.0), axis=-1, keepdims=True)
    cols = lax.broadcasted_iota(jnp.int32, (1, R), 1)
    for sub in range(n_sub):
        tsl = slice(sub * MOE_SUB, (sub + 1) * MOE_SUB)
        sc, gc = scol[tsl], gcol[tsl]

        def scatter(i, carry):
            base = pl.multiple_of(i * R, R)
            onehot = jnp.where(sc == (base + cols).astype(F32), 1.0, 0.0).astype(BF16)
            acc_ref[tsl, :] += gc * jnp.dot(onehot, y_ref[pl.ds(base, R), :], preferred_element_type=F32)
            return carry

        lax.fori_loop(*_slot_blocks(starts_ref, b, e, j * n_sub + sub, R), scatter, 0)

    @pl.when(e == N_EXPERTS - 1)
    def _():
        o_ref[...] = x1_ref[...] + g2_ref[...] * acc_ref[...]


def moe(h2, routed, x1, mod, wg, wu, wd, layer, cap):
    gate, slot, starts = routed
    B, T, D = x1.shape
    E, F = N_EXPERTS, wg.shape[-1]
    R = min(MOE_SLOTS, cap)
    assert cap % R == 0
    tt = min(2048, T)
    n_sub = tt // MOE_SUB
    y = pl.pallas_call(
        functools.partial(_moe_ffn_kernel, n_sub=n_sub, R=R),
        grid_spec=pltpu.PrefetchScalarGridSpec(
            num_scalar_prefetch=1, grid=(B, E, T // tt),
            in_specs=[pl.BlockSpec((None, tt, D), lambda b, e, j, st: (b, j, 0)),
                      pl.BlockSpec((None, None, 1, tt), lambda b, e, j, st: (b, e, 0, j)),
                      pl.BlockSpec((None, None, D, F), lambda b, e, j, st: (layer, e, 0, 0)),
                      pl.BlockSpec((None, None, D, F), lambda b, e, j, st: (layer, e, 0, 0)),
                      pl.BlockSpec((None, None, F, D), lambda b, e, j, st: (layer, e, 0, 0))],
            out_specs=pl.BlockSpec((None, None, cap, D), lambda b, e, j, st: (b, e, 0, 0)),
            scratch_shapes=[pltpu.VMEM((cap, D), F32)]),
        out_shape=jax.ShapeDtypeStruct((B, E, cap, D), BF16),
        compiler_params=_cp("parallel", "parallel", "arbitrary"), name="moe_ffn",
    )(starts, h2, slot.reshape(B, E, 1, T), wg, wu, wd)
    tc = min(1024, T)
    tok = lambda n: pl.BlockSpec((None, tc, n), lambda b, j, e, st: (b, j, 0))
    return pl.pallas_call(
        functools.partial(_moe_combine_kernel, n_sub=tc // MOE_SUB, R=R),
        grid_spec=pltpu.PrefetchScalarGridSpec(
            num_scalar_prefetch=1, grid=(B, T // tc, E),
            in_specs=[pl.BlockSpec((None, None, cap, D), lambda b, j, e, st: (b, e, 0, 0)),
                      tok(E), tok(E), tok(D), pl.BlockSpec((None, 1, D), lambda b, j, e, st: (b, 0, 5))],
            out_specs=tok(D),
            scratch_shapes=[pltpu.VMEM((tc, D), F32)]),
        out_shape=jax.ShapeDtypeStruct((B, T, D), F32),
        compiler_params=_cp("parallel", "parallel", "arbitrary"), name="moe_combine",
    )(starts, y, jnp.swapaxes(slot, 1, 2), jnp.swapaxes(gate, 1, 2), x1, mod)


def _in_cols():
    src = np.full((IN_PAD,), -1, np.int64)
    splits = (MLA_Q_LORA, MLA_KV_LORA, MLA_ROPE, 512, 512, 512, 512, 2 * GDN_HEADS, 2 * GDN_HEADS, POOL_WIDTH)
    o = np.concatenate([[0], np.cumsum(splits)])
    put = lambda name, at, lo, n: src.__setitem__(slice(SEG[name][0] + at, SEG[name][0] + at + n), np.arange(lo, lo + n))
    put("pq", 0, o[0], MLA_Q_LORA)
    put("pkv", 0, o[1], MLA_KV_LORA)
    put("pkr", MLA_NOPE, o[2], MLA_ROPE)
    for name, k in (("gq", 3), ("gk", 4), ("gv", 5), ("gz", 6)):
        put(name, 0, o[k], 512)
    for d in range(2):
        put("gab", d * LANE, o[7] + d * GDN_HEADS, GDN_HEADS)
        put("gab", d * LANE + GDN_HEADS, o[8] + d * GDN_HEADS, GDN_HEADS)
    put("pool", 0, o[9], POOL_WIDTH)
    return src


def _take_cols(w, src, axis):
    idx = jnp.asarray(np.maximum(src, 0), jnp.int32)
    mask = jnp.asarray(src >= 0)
    shape = [1] * w.ndim
    shape[axis] = -1
    return jnp.where(mask.reshape(shape), jnp.take(w, idx, axis=axis), 0.0)


def _head_pad_src(per_head, lo, n):
    src = np.full((MLA_HEADS * HEAD_PAD,), -1, np.int64)
    for h in range(MLA_HEADS):
        src[h * HEAD_PAD:h * HEAD_PAD + n] = h * per_head + lo + np.arange(n)
    return src


def _rope_tables(T, rotate):
    cos = np.ones((T, LANE), np.float32)
    sa = np.zeros((T, LANE), np.float32)
    sb = np.zeros((T, LANE), np.float32)
    if rotate:
        n_freq = MLA_ROPE // 4
        inv = ROPE_THETA ** (-np.arange(n_freq, dtype=np.float64) / n_freq)
        pos_r = np.repeat(np.arange(T // GRID_W, dtype=np.float64), GRID_W)
        pos_c = np.tile(np.arange(GRID_W, dtype=np.float64), T // GRID_W)
        for base, pos in ((MLA_NOPE, pos_r), (MLA_NOPE + 2 * n_freq, pos_c)):
            ang = pos[:, None] * inv[None, :]
            c, s = np.cos(ang), np.sin(ang)
            cos[:, base:base + n_freq] = c
            cos[:, base + n_freq:base + 2 * n_freq] = c
            sa[:, base:base + n_freq] = -s
            sb[:, base + n_freq:base + 2 * n_freq] = s
    return jnp.asarray(cos), jnp.asarray(sa), jnp.asarray(sb)


def _lane_vec(vals_by_dir, at):
    v = jnp.zeros((2, LANE), F32).at[:, at:at + GDN_HEADS].set(vals_by_dir)
    return v.reshape(1, 2 * LANE)


def kernel(x, c, ctx, c_ctx, ada_w, ada_b, norm1_g, norm2_g, w_in, mla_q_a_norm, mla_w_uq, mla_kv_a_norm, mla_w_ukv, mla_q_norm, mla_k_norm, gdn_conv_w, gdn_a_log, gdn_dt_bias, gdn_norm_g, pool_w, pool_scale, w_out, moe_router, moe_w_gate, moe_w_up, moe_w_down):
    B, T, D = x.shape
    Tc = ctx.shape[1]
    L = ada_w.shape[0]
    cvec = jnp.concatenate([c, c_ctx[None, :], jnp.zeros((SUBLANE - B - 1, D), F32)], axis=0)
    mod = ada_mod(cvec, ada_w, ada_b)
    rope_lat = _rope_tables(T, True)
    rope_ctx = _rope_tables(Tc, False)
    in_src = _in_cols()
    uq_src = _head_pad_src(MLA_QK, 0, MLA_QK)
    uk_src = _head_pad_src(MLA_NOPE + MLA_V, 0, MLA_NOPE)
    uv_src = _head_pad_src(MLA_NOPE + MLA_V, MLA_NOPE, MLA_V)
    att_src = _head_pad_src(MLA_V, 0, MLA_V)
    pad_to = lambda v, n: jnp.pad(v, (0, n - v.shape[0])).reshape(1, n)

    xc = ctx
    for l in range(L):
        need_ctx = l < L - 1
        mod_lat = mod[l, :B].reshape(B, 1, ADA_CHUNKS * D)
        mod_ctx = jnp.broadcast_to(mod[l, B].reshape(1, 1, ADA_CHUNKS * D), (B, 1, ADA_CHUNKS * D))
        w_in_p = _take_cols(w_in[l], in_src, 1).astype(BF16)
        prep_w = (
            pad_to(mla_q_a_norm[l], 256),
            jnp.pad(_take_cols(mla_w_uq[l], uq_src, 1), ((0, 256 - MLA_Q_LORA), (0, 0))).astype(BF16),
            mla_kv_a_norm[l].reshape(1, MLA_KV_LORA),
            _take_cols(mla_w_ukv[l], uk_src, 1).astype(BF16),
            _take_cols(mla_w_ukv[l], uv_src, 1).T.astype(BF16),
            pad_to(mla_q_norm[l] * (MLA_QK ** -0.5 * math.log2(math.e)), HEAD_PAD),
            pad_to(mla_k_norm[l], HEAD_PAD),
            gdn_conv_w[l],
            _lane_vec(gdn_a_log[l], 0),
            _lane_vec(gdn_dt_bias[l], 0),
        )
        wo = w_out[l]
        n_att = MLA_HEADS * MLA_V
        n_gdn = GDN_HEADS * GDN_DV
        wbd = jnp.zeros((POOL_WIDTH, POOL_WIDTH), F32)
        for gi in range(len(POOL_WINDOWS)):
            wbd = wbd.at[gi * POOL_GROUP:(gi + 1) * POOL_GROUP, gi * POOL_GROUP:(gi + 1) * POOL_GROUP].set(pool_w[l, gi])
        mix_w = (
            gdn_norm_g[l].reshape(1, GDN_DV),
            _take_cols(wo[:n_att], att_src, 0).astype(BF16),
            wo[n_att:n_att + n_gdn].astype(BF16),
            wo[n_att + n_gdn:].astype(BF16),
            wbd.astype(BF16),
            pool_scale[l].reshape(1, POOL_WIDTH),
            norm2_g[l].reshape(1, D),
            moe_router[l].T,
        )

        p_lat = dict(zip(SEG, inproj(x, mod_lat, norm1_g[l], w_in_p)))
        p_ctx = dict(zip(SEG, inproj(xc, mod_ctx, norm1_g[l], w_in_p)))
        a_ctx = prep(p_ctx, prep_w, rope_ctx, 0, Tc + T)
        a_lat = prep(p_lat, prep_w, rope_lat, Tc, Tc + T, shared=a_ctx)
        att_l = attention(a_lat["Q"], a_lat["K"], a_lat["VT"], Tc + T)
        o_all = gdn_scan(a_lat["q"], a_lat["k"], a_lat["v"], a_lat["gb"], Tc)

        def channel_mix(att, o_off, p, xin, m):
            Tn = xin.shape[1]
            x1, h2, aff = mixout(att, o_all, o_off, p["gz"], p["pool"], xin, m, mix_w)
            cap = EC_CAPACITY_FACTOR * Tn // N_EXPERTS
            return moe(h2, route(aff, cap), x1, m, moe_w_gate, moe_w_up, moe_w_down, l, cap)

        x = channel_mix(att_l, Tc, p_lat, x, mod_lat)
        if need_ctx:
            att_c = attention(a_ctx["Q"], a_lat["K"], a_lat["VT"], Tc)
            xc = channel_mix(att_c, 0, p_ctx, xc, mod_ctx)
    return x
```

```python
import functools
import math

import numpy as np
import jax
import jax.numpy as jnp
from jax import lax
from jax.experimental import pallas as pl
from jax.experimental.pallas import tpu as pltpu

F32 = jnp.float32
BF16 = jnp.bfloat16
HI = lax.Precision.HIGHEST

EPS = 1e-6
GRID_W = 64
ADA_CHUNKS = 6
MLA_HEADS = 4
MLA_NOPE = 64
MLA_ROPE = 32
MLA_QK = MLA_NOPE + MLA_ROPE
MLA_V = 64
MLA_Q_LORA = 192
MLA_KV_LORA = 128
ROPE_THETA = 10000.0
GDN_HEADS = 4
GDN_DK = 128
GDN_DV = 128
GDN_CHUNK = 64
POOL_WINDOWS = (2, 4, 8, 16)
POOL_GROUP = 64
POOL_WIDTH = POOL_GROUP * len(POOL_WINDOWS)
N_EXPERTS = 16
EC_CAPACITY_FACTOR = 2

LANE = 128
SUBLANE = 8
HEAD_PAD = 128
VMEM_LIMIT = 48 * 1024 * 1024

NT = (((1,), (1,)), ((), ()))
TN = (((0,), (0,)), ((), ()))

SEG = {}
_off = 0
for _name, _w in (("pq", 256), ("pkv", 128), ("pkr", 128), ("gq", 512), ("gk", 512), ("gv", 512),
                  ("gz", 512), ("gab", 256), ("pool", 256)):
    SEG[_name] = (_off, _w)
    _off += _w
IN_PAD = _off


def _cp(*dims):
    return pltpu.CompilerParams(dimension_semantics=dims, vmem_limit_bytes=VMEM_LIMIT)


def _silu(v):
    return v / (1.0 + jnp.exp(-v))


def _ada_kernel(c_ref, w_ref, b_ref, o_ref):
    s = _silu(c_ref[...])
    o_ref[...] = jnp.dot(s, w_ref[...], precision=HI, preferred_element_type=F32) + b_ref[...]


def ada_mod(cvec, ada_w, ada_b):
    L, D, N = ada_w.shape
    tn = N // 4
    return pl.pallas_call(
        _ada_kernel, grid=(L, N // tn),
        in_specs=[pl.BlockSpec((SUBLANE, D), lambda l, j: (0, 0)),
                  pl.BlockSpec((None, D, tn), lambda l, j: (l, 0, j)),
                  pl.BlockSpec((None, 1, tn), lambda l, j: (l, 0, j))],
        out_specs=pl.BlockSpec((None, SUBLANE, tn), lambda l, j: (l, 0, j)),
        out_shape=jax.ShapeDtypeStruct((L, SUBLANE, N), F32),
        compiler_params=_cp("parallel", "parallel"), name="ada_mod",
    )(cvec, ada_w, ada_b.reshape(L, 1, N))


def _inproj_kernel(x_ref, sh_ref, sc_ref, g_ref, w_ref, *out_refs):
    x = x_ref[...]
    h = x * lax.rsqrt(jnp.mean(x * x, axis=-1, keepdims=True) + EPS) * g_ref[...]
    hb = (h * (1.0 + sc_ref[...]) + sh_ref[...]).astype(BF16)
    for (off, n), o_ref in zip(SEG.values(), out_refs):
        o_ref[...] = jnp.dot(hb, w_ref[:, off:off + n], preferred_element_type=F32)


def inproj(x, mod, norm_g, w_in_p):
    B, T, D = x.shape
    tm = min(512, T)
    modspec = lambda k: pl.BlockSpec((None, 1, D), lambda b, i, k=k: (b, 0, k))
    return pl.pallas_call(
        _inproj_kernel, grid=(B, T // tm),
        in_specs=[pl.BlockSpec((None, tm, D), lambda b, i: (b, i, 0)), modspec(0), modspec(1),
                  pl.BlockSpec((1, D), lambda b, i: (0, 0)),
                  pl.BlockSpec((D, IN_PAD), lambda b, i: (0, 0))],
        out_specs=[pl.BlockSpec((None, tm, n), lambda b, i: (b, i, 0)) for _, n in SEG.values()],
        out_shape=[jax.ShapeDtypeStruct((B, T, n), F32) for _, n in SEG.values()],
        compiler_params=_cp("parallel", "parallel"), name="inproj",
    )(x, mod, mod, norm_g.reshape(1, D), w_in_p)


def _prep_kernel(*refs, nt, tm):
    (pq_ref, pkv_ref, pkr_ref, gq_ref, gk_ref, gv_ref, gqp_ref, gkp_ref, gvp_ref,
     gqn_ref, gkn_ref, gvn_ref, gab_ref, qan_ref, wuq_ref, kvan_ref, wuk_ref, wuv_ref,
     qn_ref, kn_ref, cos_ref, sa_ref, sb_ref, cw_ref, alog_ref, dt_ref) = refs[:26]
    Q_ref, K_ref, VT_ref, q_ref, k_ref, v_ref, gb_ref = refs[-7:]
    i = pl.program_id(1)
    cos, sa, sb = cos_ref[...], sa_ref[...], sb_ref[...]

    def rope(xh):
        return xh * cos + pltpu.roll(xh, LANE - 8, 1) * sa + pltpu.roll(xh, 8, 1) * sb

    pq = pq_ref[...]
    qa = pq * lax.rsqrt(jnp.sum(pq * pq, axis=-1, keepdims=True) * (1.0 / MLA_Q_LORA) + EPS) * qan_ref[...]
    qall = jnp.dot(qa.astype(BF16), wuq_ref[...], preferred_element_type=F32)
    pkv = pkv_ref[...]
    kva = (pkv * lax.rsqrt(jnp.mean(pkv * pkv, axis=-1, keepdims=True) + EPS) * kvan_ref[...]).astype(BF16)
    kall = jnp.dot(kva, wuk_ref[...], preferred_element_type=F32)
    vt = lax.dot_general(wuv_ref[...], kva, NT, preferred_element_type=F32)
    ones_row = lax.broadcasted_iota(jnp.int32, vt.shape, 0) % HEAD_PAD == MLA_V
    VT_ref[...] = jnp.where(ones_row, 1.0, vt).astype(BF16)
    pkr = pkr_ref[...]
    for h in range(MLA_HEADS):
        sl = slice(h * HEAD_PAD, (h + 1) * HEAD_PAD)
        qh = qall[:, sl]
        qh = qh * lax.rsqrt(jnp.sum(qh * qh, axis=-1, keepdims=True) * (1.0 / MLA_QK) + EPS) * qn_ref[...]
        Q_ref[:, sl] = rope(qh).astype(BF16)
        kh = kall[:, sl] + pkr
        kh = kh * lax.rsqrt(jnp.sum(kh * kh, axis=-1, keepdims=True) * (1.0 / MLA_QK) + EPS) * kn_ref[...]
        K_ref[:, sl] = rope(kh).astype(BF16)

    rid = lax.broadcasted_iota(jnp.int32, (tm, GDN_HEADS * GDN_DK), 0)

    def conv_silu(u_ref, up_ref, un_ref, c0):
        u = u_ref[...]
        n = u.shape[1]
        prev_row = jnp.where(i > 0, up_ref[SUBLANE - 1:SUBLANE, :], 0.0)
        next_row = jnp.where(i < nt - 1, un_ref[0:1, :], 0.0)
        um = jnp.where(rid == 0, prev_row, pltpu.roll(u, 1, 0))
        up = jnp.where(rid == tm - 1, next_row, pltpu.roll(u, tm - 1, 0))
        y = um * cw_ref[0:1, c0:c0 + n] + u * cw_ref[1:2, c0:c0 + n] + up * cw_ref[2:3, c0:c0 + n]
        return _silu(y)

    cq = conv_silu(gq_ref, gqp_ref, gqn_ref, 0)
    ck = conv_silu(gk_ref, gkp_ref, gkn_ref, GDN_HEADS * GDN_DK)
    v_ref[...] = conv_silu(gv_ref, gvp_ref, gvn_ref, 2 * GDN_HEADS * GDN_DK)
    for h in range(GDN_HEADS):
        sl = slice(h * GDN_DK, (h + 1) * GDN_DK)
        qh = cq[:, sl]
        q_ref[:, sl] = qh * lax.rsqrt(jnp.sum(qh * qh, axis=-1, keepdims=True) + EPS) * (GDN_DK ** -0.5)
        kh = ck[:, sl]
        k_ref[:, sl] = kh * lax.rsqrt(jnp.sum(kh * kh, axis=-1, keepdims=True) + EPS)

    pre = gab_ref[...]
    lane = lax.broadcasted_iota(jnp.int32, pre.shape, 1) % LANE
    sp_in = pre + dt_ref[...]
    softplus = jnp.maximum(sp_in, 0.0) + jnp.log(1.0 + jnp.exp(-jnp.abs(sp_in)))
    g = -jnp.exp(alog_ref[...]) * softplus
    beta = 1.0 / (1.0 + jnp.exp(-pre))
    gb_ref[...] = jnp.where(lane < GDN_HEADS, g, jnp.where(lane < 2 * GDN_HEADS, beta, 0.0))


def prep(p, wts, rope_tabs, row_off, t_all, shared=None):
    pq, pkv, pkr, gq, gk, gv, gab = (p[k] for k in ("pq", "pkv", "pkr", "gq", "gk", "gv", "gab"))
    B, T, _ = pq.shape
    tm = min(256, T)
    assert row_off % tm == 0
    nt = T // tm
    tb = tm // SUBLANE
    nb = T // SUBLANE
    ro = row_off // tm
    cur = lambda n: pl.BlockSpec((None, tm, n), lambda b, i: (b, i, 0))
    dst = lambda n: pl.BlockSpec((None, tm, n), lambda b, i: (b, i + ro, 0))
    prv = lambda n: pl.BlockSpec((None, SUBLANE, n), lambda b, i: (b, jnp.maximum(i * tb - 1, 0), 0))
    nxt = lambda n: pl.BlockSpec((None, SUBLANE, n), lambda b, i: (b, jnp.minimum((i + 1) * tb, nb - 1), 0))
    full = lambda a: pl.BlockSpec(a.shape, lambda b, i: (0,) * a.ndim)
    tab = pl.BlockSpec((tm, LANE), lambda b, i: (i, 0))
    W = GDN_HEADS * GDN_DK
    outs = [("Q", MLA_HEADS * HEAD_PAD, BF16), ("K", MLA_HEADS * HEAD_PAD, BF16), ("VT", None, BF16),
            ("q", W, F32), ("k", W, F32), ("v", W, F32), ("gb", 2 * LANE, F32)]
    HP = MLA_HEADS * HEAD_PAD
    shared = [] if shared is None else [shared[n] for n, _, _ in outs[1:]]
    n_in = 26
    res = pl.pallas_call(
        functools.partial(_prep_kernel, nt=nt, tm=tm), grid=(B, nt),
        in_specs=[cur(256), cur(128), cur(128), cur(W), cur(W), cur(W), prv(W), prv(W), prv(W),
                  nxt(W), nxt(W), nxt(W), cur(256)] + [full(a) for a in wts[:7]] + [tab, tab, tab]
                 + [full(a) for a in wts[7:]] + [pl.BlockSpec(memory_space=pl.ANY)] * len(shared),
        out_specs=[cur(outs[0][1])] + [dst(n) if n else pl.BlockSpec((None, HP, tm), lambda b, i: (b, 0, i + ro))
                                       for _, n, _ in outs[1:]],
        out_shape=[jax.ShapeDtypeStruct((B, T, outs[0][1]), outs[0][2])]
                  + [jax.ShapeDtypeStruct((B, t_all, n) if n else (B, HP, t_all), dt) for _, n, dt in outs[1:]],
        input_output_aliases={n_in + k: 1 + k for k in range(len(shared))},
        compiler_params=_cp("parallel", "parallel"), name="prep",
    )(pq, pkv, pkr, gq, gk, gv, gq, gk, gv, gq, gk, gv, gab, *wts[:7], *rope_tabs, *wts[7:], *shared)
    return dict(zip([n for n, _, _ in outs], res))


def _attn_kernel(q_ref, k_ref, vt_ref, o_ref, sa_ref, sb_ref, *, ck, nk):
    q = q_ref[...]
    tq = q.shape[0]

    def scores(j):
        return lax.dot_general(k_ref[j * ck:(j + 1) * ck, :], q, NT, preferred_element_type=F32)

    nv = MLA_V + 16

    def update(carry, s_ref, j):
        m, acc = carry
        s = s_ref[...]
        m_new = jnp.maximum(m, jnp.max(s, axis=0, keepdims=True))
        p = jnp.exp2(s - m_new).astype(BF16)
        acc = jnp.exp2(m - m_new) * acc + jnp.dot(vt_ref[0:nv, j * ck:(j + 1) * ck], p, preferred_element_type=F32)
        return m_new, acc

    bufs = (sa_ref, sb_ref)
    bufs[0][...] = scores(0)
    carry = (jnp.full((1, tq), -1e30, F32), jnp.zeros((nv, tq), F32))
    for j in range(nk):
        if j + 1 < nk:
            bufs[(j + 1) % 2][...] = scores(j + 1)
        carry = update(carry, bufs[j % 2], j)
    acc = carry[1]
    o = acc[:MLA_V] / acc[MLA_V:MLA_V + 1]
    o_ref[...] = jnp.concatenate([o, jnp.zeros((HEAD_PAD - MLA_V, tq), F32)], axis=0).T.astype(o_ref.dtype)


def attention(Q, K, VT, Tk):
    B, Tq, _ = Q.shape
    tq = min(512, Tq)
    ck = next(c for c in (768, 512, 384, 256, 128) if Tk % c == 0)
    qo = pl.BlockSpec((None, tq, HEAD_PAD), lambda b, h, i: (b, i, h))
    return pl.pallas_call(
        functools.partial(_attn_kernel, ck=ck, nk=Tk // ck), grid=(B, MLA_HEADS, Tq // tq),
        in_specs=[qo, pl.BlockSpec((None, Tk, HEAD_PAD), lambda b, h, i: (b, 0, h)),
                  pl.BlockSpec((None, HEAD_PAD, Tk), lambda b, h, i: (b, h, 0))],
        out_specs=qo, out_shape=jax.ShapeDtypeStruct(Q.shape, BF16),
        scratch_shapes=[pltpu.VMEM((ck, tq), F32), pltpu.VMEM((ck, tq), F32)],
        compiler_params=_cp("parallel", "parallel", "parallel"), name="attention",
    )(Q, K, VT)


GDN_GROUP = 4


def _gdn_prep_kernel(q_ref, k_ref, v_ref, gb_ref, wq_ref, u_ref, qk_ref, kdt_ref, egl_ref, *, C, G):
    H, DK = GDN_HEADS, GDN_DK
    fwd = pl.program_id(1) == 0
    dot = functools.partial(jnp.dot, preferred_element_type=F32)
    row = lax.broadcasted_iota(jnp.int32, (C, H * C), 0)
    lane = lax.broadcasted_iota(jnp.int32, (C, H * C), 1)
    col = lane & (C - 1)
    hmask = [(lane >> int(math.log2(C))) == h for h in range(H)]
    wide = lax.broadcasted_iota(jnp.int32, (C, H * DK), 1)
    kmask = [(wide >> int(math.log2(DK))) == h for h in range(H)]
    ahead = jnp.where(fwd, row - col, col - row)
    incl = ahead >= 0
    strict = ahead > 0
    eye = (row == col).astype(F32)
    r1 = lax.broadcasted_iota(jnp.int32, (C, C), 0)
    c1 = lax.broadcasted_iota(jnp.int32, (C, C), 1)
    incl16 = (jnp.where(fwd, r1 - c1, c1 - r1) >= 0).astype(F32).astype(BF16)
    eye16 = (r1 == c1).astype(F32).astype(BF16)

    def blockdiag(m, masks):
        return jnp.concatenate([jnp.where(mk, m, 0.0) for mk in masks], axis=0).astype(BF16)

    def per_head(cols, width):
        n = cols.shape[0]
        if width == LANE:
            return jnp.concatenate([jnp.broadcast_to(cols[:, h:h + 1], (n, LANE)) for h in range(H)], axis=1)
        low = lax.broadcasted_iota(jnp.int32, (n, LANE), 1) < width
        return jnp.concatenate([jnp.where(low, cols[:, h:h + 1], cols[:, h + 1:h + 2]) for h in range(0, H, 2)], axis=1)

    def terms(v):
        hi = v.astype(BF16)
        rest = v - hi.astype(F32)
        mid = rest.astype(BF16)
        return jnp.concatenate([hi, mid, (rest - mid.astype(F32)).astype(BF16)], axis=1)

    fold = lambda a, axis: sum(jnp.split(a, 3, axis=axis)[1:], jnp.split(a, 3, axis=axis)[0])
    chunks = range(G)
    rows = [slice(g * C, (g + 1) * C) for g in chunks]
    gb = [gb_ref[r, :] for r in rows]
    gterms = [terms(v) for v in gb]
    gc = [fold(dot(incl16, t), 1) for t in gterms]
    gct = [fold(lax.dot_general(t, incl16, (((0,), (1,)), ((), ())), preferred_element_type=F32), 0)
           for t in gterms]
    glast = [jnp.where(fwd, v[C - 1:C, :], v[0:1, :]) for v in gc]
    k16 = [k_ref[r, :].astype(BF16) for r in rows]
    qkk = [lax.dot_general(jnp.concatenate([q_ref[rows[g], :].astype(BF16), k16[g]], axis=0),
                           blockdiag(k_ref[rows[g], :], kmask), NT, preferred_element_type=F32)
           for g in chunks]
    a, tinv = [], []
    same = lambda s: (row >> s) == (col >> s)
    pairs = jnp.where(same(1), 1.0, 0.0)
    for g in chunks:
        egl_ref[g] = jnp.broadcast_to(jnp.exp(glast[g]), (SUBLANE, LANE))
        grow = jnp.concatenate([gct[g][h:h + 1, :] for h in range(H)], axis=1)
        decay = jnp.exp(jnp.where(incl, per_head(gc[g], C) - grow, -1e30))
        qk_ref[g] = (qkk[g][:C] * decay).astype(BF16)
        a.append(jnp.where(strict, qkk[g][C:] * decay, 0.0) * per_head(gb[g][:, H:], C))
        tinv.append(eye - a[g] * pairs)
    for s in range(1, int(math.log2(C))):
        join = jnp.where(same(s + 1), jnp.where(same(s), 0.0, 1.0), 0.0)
        x16 = [tinv[g].astype(BF16) for g in chunks]
        xl = [dot(x16[g], blockdiag(a[g] * join, hmask)) for g in chunks]
        xlx = [dot(xl[g].astype(BF16), blockdiag(tinv[g], hmask)) for g in chunks]
        tinv = [tinv[g] - xlx[g] for g in chunks]
    gcw = [per_head(gc[g], DK) for g in chunks]
    bw = [per_head(gb[g][:, H:], DK) for g in chunks]
    wu = [dot(tinv[g].astype(BF16),
              jnp.concatenate([blockdiag(k_ref[rows[g], :] * (bw[g] * jnp.exp(gcw[g])), kmask),
                               blockdiag(v_ref[rows[g], :] * bw[g], kmask)], axis=1)) for g in chunks]
    kdt = [lax.dot_general((k_ref[rows[g], :] * jnp.exp(per_head(glast[g], DK) - gcw[g])).astype(BF16), eye16,
                           TN, preferred_element_type=F32) for g in chunks]
    for g in chunks:
        wq_ref[g, 0:C, :] = wu[g][:, :H * DK].astype(BF16)
        wq_ref[g, C:2 * C, :] = (q_ref[rows[g], :] * jnp.exp(gcw[g])).astype(BF16)
        u_ref[g] = wu[g][:, H * DK:]
        for h in range(H):
            kdt_ref[g, :, h * C:(h + 1) * C] = kdt[g][h * DK:(h + 1) * DK].astype(BF16)


def _gdn_rec_kernel(*refs, C, G, B):
    ins = (refs[0:5], refs[5:10])
    outs = refs[10:12]
    s_ref = refs[12]
    dot = functools.partial(jnp.dot, preferred_element_type=F32)

    @pl.when(pl.program_id(0) == 0)
    def _():
        s_ref[...] = jnp.zeros_like(s_ref)

    hsl = lambda h: slice(h * GDN_DK, (h + 1) * GDN_DK)
    csl = lambda h: slice(h * C, (h + 1) * C)
    for step in range(G):
        chains = [(d, b, h, step if d == 0 else G - 1 - step)
                  for d in range(2) for b in range(B) for h in range(GDN_HEADS)]
        S = {c: s_ref[c[0], c[1], c[2]] for c in chains}
        r = {(d, b, h, g): dot(ins[d][0][b, g, :, hsl(h)], S[(d, b, h, g)].astype(BF16))
             for (d, b, h, g) in chains}
        vn = {(d, b, h, g): (ins[d][1][b, g, :, hsl(h)] - r[(d, b, h, g)][:C]).astype(BF16) for (d, b, h, g) in chains}
        o = {(d, b, h, g): dot(ins[d][2][b, g, :, csl(h)], vn[(d, b, h, g)]) for (d, b, h, g) in chains}
        upd = {(d, b, h, g): dot(ins[d][3][b, g, :, csl(h)], vn[(d, b, h, g)]) for (d, b, h, g) in chains}
        for c in chains:
            d, b, h, g = c
            outs[d][b, g, :, hsl(h)] = r[c][C:] + o[c]
            s_ref[d, b, h] = S[c] * ins[d][4][b, g, 0:1, h:h + 1] + upd[c]


def gdn_scan(q, k, v, gb, n_ctx):
    B, Tt, W = q.shape
    C, G = GDN_CHUNK, GDN_GROUP
    n = Tt // C
    ng = n // G
    ncg = n_ctx // (C * G)
    assert n % G == 0 and n_ctx % (C * G) == 0
    gp = next(c for c in (12, 8, 6, 4, 3, 2, 1) if n % c == 0)
    tok = pl.BlockSpec((None, gp * C, W), lambda b, d, s: (b, s, 0))
    shapes = [((2 * C, W), BF16), ((C, W), F32), ((C, GDN_HEADS * C), BF16), ((GDN_DK, GDN_HEADS * C), BF16),
              ((SUBLANE, LANE), F32)]
    nat = lambda b, d, s: (b, d, s, 0, 0)
    mid = pl.pallas_call(
        functools.partial(_gdn_prep_kernel, C=C, G=gp), grid=(B, 2, n // gp),
        in_specs=[tok, tok, tok, pl.BlockSpec((None, gp * C, LANE), lambda b, d, s: (b, s, d))],
        out_specs=[pl.BlockSpec((None, None, gp, r, w), nat) for (r, w), _ in shapes],
        out_shape=[jax.ShapeDtypeStruct((B, 2, n, r, w), dt) for (r, w), dt in shapes],
        compiler_params=_cp("parallel", "parallel", "parallel"), name="gdn_prep",
    )(q, k, v, gb)

    bwd = lambda s: jnp.where(s < ncg, ncg - 1 - s, ng + ncg - 1 - s)
    both = lambda r, w, d: pl.BlockSpec((B, None, G, r, w), (lambda s: (0, 0, s, 0, 0)) if d == 0
                                        else (lambda s: (0, 1, bwd(s), 0, 0)))
    o_spec = lambda d: pl.BlockSpec((B, G, C, W), (lambda s: (0, s, 0, 0)) if d == 0 else (lambda s: (0, bwd(s), 0, 0)))
    o_f, o_b = pl.pallas_call(
        functools.partial(_gdn_rec_kernel, C=C, G=G, B=B), grid=(ng,),
        in_specs=[both(r, w, d) for d in range(2) for (r, w), _ in shapes],
        out_specs=[o_spec(0), o_spec(1)],
        out_shape=[jax.ShapeDtypeStruct((B, n, C, W), F32)] * 2,
        scratch_shapes=[pltpu.VMEM((2, B, GDN_HEADS, GDN_DK, GDN_DV), F32)],
        compiler_params=_cp("arbitrary"), name="gdn_rec",
    )(*mid, *mid)
    return o_f.reshape(B, Tt, W), o_b.reshape(B, Tt, W)


def _mixout_kernel(att_ref, *refs, tm, T, nt, n_o):
    o_refs, refs = refs[:2 * n_o], refs[2 * n_o:]
    (z_ref, u_ref, up_ref, un_ref, x_ref, g1_ref, sh2_ref, sc2_ref,
     gng_ref, wa_ref, wg_ref, wp_ref, wbd_ref, ps_ref, n2g_ref, wr_ref, x1_ref, h2_ref, aff_ref) = refs
    i = pl.program_id(1)
    o = jnp.concatenate([o_refs[k][...] + o_refs[n_o + k][...] for k in range(n_o)], axis=0)
    z = z_ref[...]
    parts = []
    for h in range(GDN_HEADS):
        sl = slice(h * GDN_DV, (h + 1) * GDN_DV)
        oh = o[:, sl]
        oh = oh * lax.rsqrt(jnp.mean(oh * oh, axis=-1, keepdims=True) + EPS) * gng_ref[...]
        parts.append((oh * _silu(z[:, sl])).astype(BF16))
    gdn = jnp.concatenate(parts, axis=1)

    u = u_ref[...]
    halo = SUBLANE
    ext = jnp.concatenate([jnp.where(i > 0, up_ref[...], 0.0), u, jnp.where(i < nt - 1, un_ref[...], 0.0)], axis=0)
    n_ext = tm + 2 * halo
    back = lambda a, s: pltpu.roll(a, s, 0)
    ahead = lambda a, s: pltpu.roll(a, n_ext - s, 0)
    s2 = ext + back(ext, 1)
    s4 = back(s2, 1) + ahead(s2, 1)
    s8 = back(s4, 2) + ahead(s4, 2)
    s16 = back(s8, 4) + ahead(s8, 4)
    t = i * tm + lax.broadcasted_iota(jnp.int32, (tm, 1), 0)
    lane = lax.broadcasted_iota(jnp.int32, (tm, POOL_WIDTH), 1)
    mean = None
    for gi, (win, sw) in reversed(list(enumerate(zip(POOL_WINDOWS, (s2, s4, s8, s16))))):
        lo = jnp.maximum(t - win // 2, 0)
        hi = jnp.minimum(t - win // 2 + win, T)
        m = sw[halo:halo + tm, :] / (hi - lo).astype(F32)
        mean = m if mean is None else jnp.where(lane < (gi + 1) * POOL_GROUP, m, mean)
    yp = jnp.dot((mean - u).astype(BF16), wbd_ref[...], preferred_element_type=F32) * ps_ref[...]

    y = (jnp.dot(att_ref[...], wa_ref[...], preferred_element_type=F32)
         + jnp.dot(gdn, wg_ref[...], preferred_element_type=F32)
         + jnp.dot(yp.astype(BF16), wp_ref[...], preferred_element_type=F32))
    x1 = x_ref[...] + g1_ref[...] * y
    x1_ref[...] = x1
    h2 = x1 * lax.rsqrt(jnp.mean(x1 * x1, axis=-1, keepdims=True) + EPS) * n2g_ref[...]
    h2 = h2 * (1.0 + sc2_ref[...]) + sh2_ref[...]
    h2_ref[...] = h2.astype(BF16)
    lg = lax.dot_general(wr_ref[...], h2, NT, precision=HI, preferred_element_type=F32)
    e = jnp.exp(lg - jnp.max(lg, axis=0, keepdims=True))
    aff_ref[...] = e / jnp.sum(e, axis=0, keepdims=True)


def mixout(att, o, o_off, z, u, x, mod, wts):
    B, T, D = x.shape
    tm = min(512, T)
    nt = T // tm
    tb = tm // SUBLANE
    nb = T // SUBLANE
    to = math.gcd(tm, o_off) if o_off else tm
    n_o = tm // to
    W = GDN_HEADS * GDN_DV
    cur = lambda n: pl.BlockSpec((None, tm, n), lambda b, i: (b, i, 0))
    odir = [pl.BlockSpec((None, to, W), lambda b, i, k=k: (b, i * n_o + o_off // to + k, 0)) for k in range(n_o)]
    modspec = lambda k: pl.BlockSpec((None, 1, D), lambda b, i: (b, 0, k))
    full = lambda a: pl.BlockSpec(a.shape, lambda b, i: (0,) * a.ndim)
    return pl.pallas_call(
        functools.partial(_mixout_kernel, tm=tm, T=T, nt=nt, n_o=n_o), grid=(B, nt),
        in_specs=[cur(MLA_HEADS * HEAD_PAD)] + odir + odir + [cur(W), cur(POOL_WIDTH),
                  pl.BlockSpec((None, SUBLANE, POOL_WIDTH), lambda b, i: (b, jnp.maximum(i * tb - 1, 0), 0)),
                  pl.BlockSpec((None, SUBLANE, POOL_WIDTH), lambda b, i: (b, jnp.minimum((i + 1) * tb, nb - 1), 0)),
                  cur(D), modspec(2), modspec(3), modspec(4)] + [full(a) for a in wts],
        out_specs=[cur(D), cur(D), pl.BlockSpec((None, N_EXPERTS, tm), lambda b, i: (b, 0, i))],
        out_shape=[jax.ShapeDtypeStruct((B, T, D), F32), jax.ShapeDtypeStruct((B, T, D), BF16),
                   jax.ShapeDtypeStruct((B, N_EXPERTS, T), F32)],
        compiler_params=_cp("parallel", "parallel"), name="mixout",
    )(att, *([o[0]] * n_o), *([o[1]] * n_o), z, u, u, u, x, mod, mod, mod, *wts)


MOE_SUB = 256
MOE_SLOTS = 128


def _route_kernel(aff_ref, gate_ref, slot_ref, starts_ref, *, cap, T):
    aff = aff_ref[...]

    def body(it, res):
        cand = res | jnp.left_shift(jnp.int32(1), 30 - it)
        cnt = jnp.sum((aff >= pltpu.bitcast(cand, F32)).astype(jnp.int32), axis=-1, keepdims=True)
        return jnp.where(cnt >= cap, cand, res)

    bits = lax.fori_loop(0, 31, body, jnp.zeros((N_EXPERTS, 1), jnp.int32))
    thr = pltpu.bitcast(bits, F32)
    above = pltpu.bitcast(bits + 1, F32)
    n_gt = jnp.sum((aff >= above).astype(jnp.int32), axis=-1, keepdims=True)
    need = (cap - n_gt).astype(F32)
    upper = (lax.broadcasted_iota(jnp.int32, (LANE, LANE), 0)
             < lax.broadcasted_iota(jnp.int32, (LANE, LANE), 1)).astype(BF16)
    seen = jnp.zeros((N_EXPERTS, 1), F32)
    taken = jnp.zeros((N_EXPERTS, 1), F32)
    lane = lax.broadcasted_iota(jnp.int32, (N_EXPERTS, LANE), 1)
    starts = jnp.zeros((N_EXPERTS, LANE), jnp.int32)
    per_sub = MOE_SUB // LANE
    for j in range(T // LANE):
        if j % per_sub == 0:
            starts = jnp.where(lane == j // per_sub, taken.astype(jnp.int32), starts)
        sl = slice(j * LANE, (j + 1) * LANE)
        aj = aff[:, sl]
        eq = jnp.where(aj >= thr, jnp.where(aj < above, 1.0, 0.0), 0.0)
        rank = jnp.dot(eq.astype(BF16), upper, preferred_element_type=F32) + seen
        sel = jnp.where(aj >= above, 1.0, jnp.where(rank < need, eq, 0.0))
        gate_ref[:, sl] = sel * aj
        slot = jnp.dot(sel.astype(BF16), upper, preferred_element_type=F32) + taken
        slot_ref[:, sl] = jnp.where(sel > 0.0, slot, -1.0)
        seen = seen + jnp.sum(eq, axis=-1, keepdims=True)
        taken = taken + jnp.sum(sel, axis=-1, keepdims=True)
    starts_ref[...] = jnp.where(lane == T // MOE_SUB, taken.astype(jnp.int32), starts)


def route(aff, cap):
    B, E, T = aff.shape
    assert T % MOE_SUB == 0 and T // MOE_SUB < LANE
    spec = pl.BlockSpec((None, E, T), lambda b: (b, 0, 0))
    return pl.pallas_call(
        functools.partial(_route_kernel, cap=cap, T=T), grid=(B,), in_specs=[spec],
        out_specs=[spec, spec, pl.BlockSpec((None, E, LANE), lambda b: (b, 0, 0))],
        out_shape=[jax.ShapeDtypeStruct(aff.shape, F32), jax.ShapeDtypeStruct(aff.shape, F32),
                   jax.ShapeDtypeStruct((B, E, LANE), jnp.int32)],
        compiler_params=_cp("parallel"), name="route",
    )(aff)


def _slot_blocks(starts_ref, b, e, sub, R):
    s0 = starts_ref[b, e, sub]
    s1 = starts_ref[b, e, sub + 1]
    return s0 // R, (s1 + R - 1) // R


def _moe_ffn_kernel(starts_ref, h_ref, slot_ref, wg_ref, wu_ref, wd_ref, y_ref, xs_ref, *, n_sub, R):
    b, e, j = pl.program_id(0), pl.program_id(1), pl.program_id(2)

    @pl.when(j == 0)
    def _():
        xs_ref[...] = jnp.zeros_like(xs_ref)

    rows = lax.broadcasted_iota(jnp.int32, (R, 1), 0)
    for sub in range(n_sub):
        tsl = slice(sub * MOE_SUB, (sub + 1) * MOE_SUB)
        h = h_ref[tsl, :]
        srow = slot_ref[:, tsl]

        def gather(i, carry):
            base = pl.multiple_of(i * R, R)
            onehot = jnp.where(srow == (base + rows).astype(F32), 1.0, 0.0).astype(BF16)
            xs_ref[pl.ds(base, R), :] += jnp.dot(onehot, h, preferred_element_type=F32)
            return carry

        lax.fori_loop(*_slot_blocks(starts_ref, b, e, j * n_sub + sub, R), gather, 0)

    @pl.when(j == pl.num_programs(2) - 1)
    def _():
        xs = xs_ref[...].astype(BF16)
        a = jnp.dot(xs, wg_ref[...].astype(BF16), preferred_element_type=F32)
        hid = (_silu(a) * jnp.dot(xs, wu_ref[...].astype(BF16), preferred_element_type=F32)).astype(BF16)
        y_ref[...] = jnp.dot(hid, wd_ref[...].astype(BF16), preferred_element_type=F32).astype(BF16)


def _moe_combine_kernel(starts_ref, y_ref, slot_ref, gate_ref, x1_ref, g2_ref, o_ref, acc_ref, *, n_sub, R):
    b, j, e = pl.program_id(0), pl.program_id(1), pl.program_id(2)

    @pl.when(e == 0)
    def _():
        acc_ref[...] = jnp.zeros_like(acc_ref)

    lane = lax.broadcasted_iota(jnp.int32, slot_ref.shape, 1)
    scol = jnp.sum(jnp.where(lane == e, slot_ref[...], 0.0), axis=-1, keepdims=True)
    gcol = jnp.sum(jnp.where(lane == e, gate_ref[...], 0.0), axis=-1, keepdims=True)
    cols = lax.broadcasted_iota(jnp.int32, (1, R), 1)
    for sub in range(n_sub):
        tsl = slice(sub * MOE_SUB, (sub + 1) * MOE_SUB)
        sc, gc = scol[tsl], gcol[tsl]

        def scatter(i, carry):
            base = pl.multiple_of(i * R, R)
            onehot = jnp.where(sc == (base + cols).astype(F32), 1.0, 0.0).astype(BF16)
            acc_ref[tsl, :] += gc * jnp.dot(onehot, y_ref[pl.ds(base, R), :], preferred_element_type=F32)
            return carry

        lax.fori_loop(*_slot_blocks(starts_ref, b, e, j * n_sub + sub, R), scatter, 0)

    @pl.when(e == N_EXPERTS - 1)
    def _():
        o_ref[...] = x1_ref[...] + g2_ref[...] * acc_ref[...]


def moe(h2, routed, x1, mod, wg, wu, wd, layer, cap):
    gate, slot, starts = routed
    B, T, D = x1.shape
    E, F = N_EXPERTS, wg.shape[-1]
    R = min(MOE_SLOTS, cap)
    assert cap % R == 0
    tt = min(2048, T)
    n_sub = tt // MOE_SUB
    y = pl.pallas_call(
        functools.partial(_moe_ffn_kernel, n_sub=n_sub, R=R),
        grid_spec=pltpu.PrefetchScalarGridSpec(
            num_scalar_prefetch=1, grid=(B, E, T // tt),
            in_specs=[pl.BlockSpec((None, tt, D), lambda b, e, j, st: (b, j, 0)),
                      pl.BlockSpec((None, None, 1, tt), lambda b, e, j, st: (b, e, 0, j)),
                      pl.BlockSpec((None, None, D, F), lambda b, e, j, st: (layer, e, 0, 0)),
                      pl.BlockSpec((None, None, D, F), lambda b, e, j, st: (layer, e, 0, 0)),
                      pl.BlockSpec((None, None, F, D), lambda b, e, j, st: (layer, e, 0, 0))],
            out_specs=pl.BlockSpec((None, None, cap, D), lambda b, e, j, st: (b, e, 0, 0)),
            scratch_shapes=[pltpu.VMEM((cap, D), F32)]),
        out_shape=jax.ShapeDtypeStruct((B, E, cap, D), BF16),
        compiler_params=_cp("parallel", "parallel", "arbitrary"), name="moe_ffn",
    )(starts, h2, slot.reshape(B, E, 1, T), wg, wu, wd)
    tc = min(1024, T)
    tok = lambda n: pl.BlockSpec((None, tc, n), lambda b, j, e, st: (b, j, 0))
    return pl.pallas_call(
        functools.partial(_moe_combine_kernel, n_sub=tc // MOE_SUB, R=min(2 * R, cap)),
        grid_spec=pltpu.PrefetchScalarGridSpec(
            num_scalar_prefetch=1, grid=(B, T // tc, E),
            in_specs=[pl.BlockSpec((None, None, cap, D), lambda b, j, e, st: (b, e, 0, 0)),
                      tok(E), tok(E), tok(D), pl.BlockSpec((None, 1, D), lambda b, j, e, st: (b, 0, 5))],
            out_specs=tok(D),
            scratch_shapes=[pltpu.VMEM((tc, D), F32)]),
        out_shape=jax.ShapeDtypeStruct((B, T, D), F32),
        compiler_params=_cp("parallel", "parallel", "arbitrary"), name="moe_combine",
    )(starts, y, jnp.swapaxes(slot, 1, 2), jnp.swapaxes(gate, 1, 2), x1, mod)


def _in_cols():
    src = np.full((IN_PAD,), -1, np.int64)
    splits = (MLA_Q_LORA, MLA_KV_LORA, MLA_ROPE, 512, 512, 512, 512, 2 * GDN_HEADS, 2 * GDN_HEADS, POOL_WIDTH)
    o = np.concatenate([[0], np.cumsum(splits)])
    put = lambda name, at, lo, n: src.__setitem__(slice(SEG[name][0] + at, SEG[name][0] + at + n), np.arange(lo, lo + n))
    put("pq", 0, o[0], MLA_Q_LORA)
    put("pkv", 0, o[1], MLA_KV_LORA)
    put("pkr", MLA_NOPE, o[2], MLA_ROPE)
    for name, k in (("gq", 3), ("gk", 4), ("gv", 5), ("gz", 6)):
        put(name, 0, o[k], 512)
    for d in range(2):
        put("gab", d * LANE, o[7] + d * GDN_HEADS, GDN_HEADS)
        put("gab", d * LANE + GDN_HEADS, o[8] + d * GDN_HEADS, GDN_HEADS)
    put("pool", 0, o[9], POOL_WIDTH)
    return src


def _take_cols(w, src, axis):
    idx = jnp.asarray(np.maximum(src, 0), jnp.int32)
    mask = jnp.asarray(src >= 0)
    shape = [1] * w.ndim
    shape[axis] = -1
    return jnp.where(mask.reshape(shape), jnp.take(w, idx, axis=axis), 0.0)


def _head_pad_src(per_head, lo, n):
    src = np.full((MLA_HEADS * HEAD_PAD,), -1, np.int64)
    for h in range(MLA_HEADS):
        src[h * HEAD_PAD:h * HEAD_PAD + n] = h * per_head + lo + np.arange(n)
    return src


def _rope_tables(T, rotate):
    cos = np.ones((T, LANE), np.float32)
    sa = np.zeros((T, LANE), np.float32)
    sb = np.zeros((T, LANE), np.float32)
    if rotate:
        n_freq = MLA_ROPE // 4
        inv = ROPE_THETA ** (-np.arange(n_freq, dtype=np.float64) / n_freq)
        pos_r = np.repeat(np.arange(T // GRID_W, dtype=np.float64), GRID_W)
        pos_c = np.tile(np.arange(GRID_W, dtype=np.float64), T // GRID_W)
        for base, pos in ((MLA_NOPE, pos_r), (MLA_NOPE + 2 * n_freq, pos_c)):
            ang = pos[:, None] * inv[None, :]
            c, s = np.cos(ang), np.sin(ang)
            cos[:, base:base + n_freq] = c
            cos[:, base + n_freq:base + 2 * n_freq] = c
            sa[:, base:base + n_freq] = -s
            sb[:, base + n_freq:base + 2 * n_freq] = s
    return jnp.asarray(cos), jnp.asarray(sa), jnp.asarray(sb)


def _lane_vec(vals_by_dir, at):
    v = jnp.zeros((2, LANE), F32).at[:, at:at + GDN_HEADS].set(vals_by_dir)
    return v.reshape(1, 2 * LANE)


def kernel(x, c, ctx, c_ctx, ada_w, ada_b, norm1_g, norm2_g, w_in, mla_q_a_norm, mla_w_uq, mla_kv_a_norm, mla_w_ukv, mla_q_norm, mla_k_norm, gdn_conv_w, gdn_a_log, gdn_dt_bias, gdn_norm_g, pool_w, pool_scale, w_out, moe_router, moe_w_gate, moe_w_up, moe_w_down):
    B, T, D = x.shape
    Tc = ctx.shape[1]
    L = ada_w.shape[0]
    cvec = jnp.concatenate([c, c_ctx[None, :], jnp.zeros((SUBLANE - B - 1, D), F32)], axis=0)
    mod = ada_mod(cvec, ada_w, ada_b)
    rope_lat = _rope_tables(T, True)
    rope_ctx = _rope_tables(Tc, False)
    in_src = _in_cols()
    uq_src = _head_pad_src(MLA_QK, 0, MLA_QK)
    uk_src = _head_pad_src(MLA_NOPE + MLA_V, 0, MLA_NOPE)
    uv_src = _head_pad_src(MLA_NOPE + MLA_V, MLA_NOPE, MLA_V)
    att_src = _head_pad_src(MLA_V, 0, MLA_V)
    pad_to = lambda v, n: jnp.pad(v, (0, n - v.shape[0])).reshape(1, n)

    xc = ctx
    for l in range(L):
        need_ctx = l < L - 1
        mod_lat = mod[l, :B].reshape(B, 1, ADA_CHUNKS * D)
        mod_ctx = jnp.broadcast_to(mod[l, B].reshape(1, 1, ADA_CHUNKS * D), (B, 1, ADA_CHUNKS * D))
        w_in_p = _take_cols(w_in[l], in_src, 1).astype(BF16)
        prep_w = (
            pad_to(mla_q_a_norm[l], 256),
            jnp.pad(_take_cols(mla_w_uq[l], uq_src, 1), ((0, 256 - MLA_Q_LORA), (0, 0))).astype(BF16),
            mla_kv_a_norm[l].reshape(1, MLA_KV_LORA),
            _take_cols(mla_w_ukv[l], uk_src, 1).astype(BF16),
            _take_cols(mla_w_ukv[l], uv_src, 1).T.astype(BF16),
            pad_to(mla_q_norm[l] * (MLA_QK ** -0.5 * math.log2(math.e)), HEAD_PAD),
            pad_to(mla_k_norm[l], HEAD_PAD),
            gdn_conv_w[l],
            _lane_vec(gdn_a_log[l], 0),
            _lane_vec(gdn_dt_bias[l], 0),
        )
        wo = w_out[l]
        n_att = MLA_HEADS * MLA_V
        n_gdn = GDN_HEADS * GDN_DV
        wbd = jnp.zeros((POOL_WIDTH, POOL_WIDTH), F32)
        for gi in range(len(POOL_WINDOWS)):
            wbd = wbd.at[gi * POOL_GROUP:(gi + 1) * POOL_GROUP, gi * POOL_GROUP:(gi + 1) * POOL_GROUP].set(pool_w[l, gi])
        mix_w = (
            gdn_norm_g[l].reshape(1, GDN_DV),
            _take_cols(wo[:n_att], att_src, 0).astype(BF16),
            wo[n_att:n_att + n_gdn].astype(BF16),
            wo[n_att + n_gdn:].astype(BF16),
            wbd.astype(BF16),
            pool_scale[l].reshape(1, POOL_WIDTH),
            norm2_g[l].reshape(1, D),
            moe_router[l].T,
        )

        p_lat = dict(zip(SEG, inproj(x, mod_lat, norm1_g[l], w_in_p)))
        p_ctx = dict(zip(SEG, inproj(xc, mod_ctx, norm1_g[l], w_in_p)))
        a_ctx = prep(p_ctx, prep_w, rope_ctx, 0, Tc + T)
        a_lat = prep(p_lat, prep_w, rope_lat, Tc, Tc + T, shared=a_ctx)
        att_l = attention(a_lat["Q"], a_lat["K"], a_lat["VT"], Tc + T)
        o_all = gdn_scan(a_lat["q"], a_lat["k"], a_lat["v"], a_lat["gb"], Tc)

        def channel_mix(att, o_off, p, xin, m):
            Tn = xin.shape[1]
            x1, h2, aff = mixout(att, o_all, o_off, p["gz"], p["pool"], xin, m, mix_w)
            cap = EC_CAPACITY_FACTOR * Tn // N_EXPERTS
            return moe(h2, route(aff, cap), x1, m, moe_w_gate, moe_w_up, moe_w_down, l, cap)

        x = channel_mix(att_l, Tc, p_lat, x, mod_lat)
        if need_ctx:
            att_c = attention(a_ctx["Q"], a_lat["K"], a_lat["VT"], Tc)
            xc = channel_mix(att_c, 0, p_ctx, xc, mod_ctx)
    return x
```

```python
import functools
import math

import numpy as np
import jax
import jax.numpy as jnp
from jax import lax
from jax.experimental import pallas as pl
from jax.experimental.pallas import tpu as pltpu

F32 = jnp.float32
BF16 = jnp.bfloat16
HI = lax.Precision.HIGHEST

EPS = 1e-6
GRID_W = 64
ADA_CHUNKS = 6
MLA_HEADS = 4
MLA_NOPE = 64
MLA_ROPE = 32
MLA_QK = MLA_NOPE + MLA_ROPE
MLA_V = 64
MLA_Q_LORA = 192
MLA_KV_LORA = 128
ROPE_THETA = 10000.0
GDN_HEADS = 4
GDN_DK = 128
GDN_DV = 128
GDN_CHUNK = 64
POOL_WINDOWS = (2, 4, 8, 16)
POOL_GROUP = 64
POOL_WIDTH = POOL_GROUP * len(POOL_WINDOWS)
N_EXPERTS = 16
EC_CAPACITY_FACTOR = 2

LANE = 128
SUBLANE = 8
HEAD_PAD = 128
VMEM_LIMIT = 48 * 1024 * 1024

NT = (((1,), (1,)), ((), ()))
TN = (((0,), (0,)), ((), ()))

SEG = {}
_off = 0
for _name, _w in (("pq", 256), ("pkv", 128), ("pkr", 128), ("gq", 512), ("gk", 512), ("gv", 512),
                  ("gz", 512), ("gab", 256), ("pool", 256)):
    SEG[_name] = (_off, _w)
    _off += _w
IN_PAD = _off


def _cp(*dims):
    return pltpu.CompilerParams(dimension_semantics=dims, vmem_limit_bytes=VMEM_LIMIT)


def _silu(v):
    return v / (1.0 + jnp.exp(-v))


def _ada_kernel(c_ref, w_ref, b_ref, o_ref):
    s = _silu(c_ref[...])
    o_ref[...] = jnp.dot(s, w_ref[...], precision=HI, preferred_element_type=F32) + b_ref[...]


def ada_mod(cvec, ada_w, ada_b):
    L, D, N = ada_w.shape
    tn = N // 4
    return pl.pallas_call(
        _ada_kernel, grid=(L, N // tn),
        in_specs=[pl.BlockSpec((SUBLANE, D), lambda l, j: (0, 0)),
                  pl.BlockSpec((None, D, tn), lambda l, j: (l, 0, j)),
                  pl.BlockSpec((None, 1, tn), lambda l, j: (l, 0, j))],
        out_specs=pl.BlockSpec((None, SUBLANE, tn), lambda l, j: (l, 0, j)),
        out_shape=jax.ShapeDtypeStruct((L, SUBLANE, N), F32),
        compiler_params=_cp("parallel", "parallel"), name="ada_mod",
    )(cvec, ada_w, ada_b.reshape(L, 1, N))


def _inproj_kernel(x_ref, sh_ref, sc_ref, g_ref, w_ref, *out_refs):
    x = x_ref[...]
    h = x * lax.rsqrt(jnp.mean(x * x, axis=-1, keepdims=True) + EPS) * g_ref[...]
    hb = (h * (1.0 + sc_ref[...]) + sh_ref[...]).astype(BF16)
    for (off, n), o_ref in zip(SEG.values(), out_refs):
        o_ref[...] = jnp.dot(hb, w_ref[:, off:off + n], preferred_element_type=F32)


def inproj(x, mod, norm_g, w_in_p):
    B, T, D = x.shape
    tm = min(512, T)
    modspec = lambda k: pl.BlockSpec((None, 1, D), lambda b, i, k=k: (b, 0, k))
    return pl.pallas_call(
        _inproj_kernel, grid=(B, T // tm),
        in_specs=[pl.BlockSpec((None, tm, D), lambda b, i: (b, i, 0)), modspec(0), modspec(1),
                  pl.BlockSpec((1, D), lambda b, i: (0, 0)),
                  pl.BlockSpec((D, IN_PAD), lambda b, i: (0, 0))],
        out_specs=[pl.BlockSpec((None, tm, n), lambda b, i: (b, i, 0)) for _, n in SEG.values()],
        out_shape=[jax.ShapeDtypeStruct((B, T, n), F32) for _, n in SEG.values()],
        compiler_params=_cp("parallel", "parallel"), name="inproj",
    )(x, mod, mod, norm_g.reshape(1, D), w_in_p)


def _prep_kernel(*refs, nt, tm):
    (pq_ref, pkv_ref, pkr_ref, gq_ref, gk_ref, gv_ref, gqp_ref, gkp_ref, gvp_ref,
     gqn_ref, gkn_ref, gvn_ref, gab_ref, qan_ref, wuq_ref, kvan_ref, wuk_ref, wuv_ref,
     qn_ref, kn_ref, cos_ref, sa_ref, sb_ref, cw_ref, alog_ref, dt_ref) = refs[:26]
    Q_ref, K_ref, VT_ref, q_ref, k_ref, v_ref, gb_ref = refs[-7:]
    i = pl.program_id(1)
    cos, sa, sb = cos_ref[...], sa_ref[...], sb_ref[...]

    def rope(xh):
        return xh * cos + pltpu.roll(xh, LANE - 8, 1) * sa + pltpu.roll(xh, 8, 1) * sb

    pq = pq_ref[...]
    qa = pq * lax.rsqrt(jnp.sum(pq * pq, axis=-1, keepdims=True) * (1.0 / MLA_Q_LORA) + EPS) * qan_ref[...]
    qall = jnp.dot(qa.astype(BF16), wuq_ref[...], preferred_element_type=F32)
    pkv = pkv_ref[...]
    kva = (pkv * lax.rsqrt(jnp.mean(pkv * pkv, axis=-1, keepdims=True) + EPS) * kvan_ref[...]).astype(BF16)
    kall = jnp.dot(kva, wuk_ref[...], preferred_element_type=F32)
    vt = lax.dot_general(wuv_ref[...], kva, NT, preferred_element_type=F32)
    ones_row = lax.broadcasted_iota(jnp.int32, vt.shape, 0) % HEAD_PAD == MLA_V
    VT_ref[...] = jnp.where(ones_row, 1.0, vt).astype(BF16)
    pkr = pkr_ref[...]
    for h in range(MLA_HEADS):
        sl = slice(h * HEAD_PAD, (h + 1) * HEAD_PAD)
        qh = qall[:, sl]
        qh = qh * lax.rsqrt(jnp.sum(qh * qh, axis=-1, keepdims=True) * (1.0 / MLA_QK) + EPS) * qn_ref[...]
        Q_ref[:, sl] = rope(qh).astype(BF16)
        kh = kall[:, sl] + pkr
        kh = kh * lax.rsqrt(jnp.sum(kh * kh, axis=-1, keepdims=True) * (1.0 / MLA_QK) + EPS) * kn_ref[...]
        K_ref[:, sl] = rope(kh).astype(BF16)

    rid = lax.broadcasted_iota(jnp.int32, (SUBLANE, GDN_HEADS * GDN_DK), 0)

    def conv_silu(u_ref, up_ref, un_ref, c0):
        u = u_ref[...]
        n = u.shape[1]
        prev_row = jnp.where(i > 0, up_ref[SUBLANE - 1:SUBLANE, :], 0.0)
        next_row = jnp.where(i < nt - 1, un_ref[0:1, :], 0.0)
        um = pltpu.roll(u, 1, 0)
        um = jnp.concatenate([jnp.where(rid == 0, prev_row, um[:SUBLANE]), um[SUBLANE:]], axis=0)
        up = pltpu.roll(u, tm - 1, 0)
        up = jnp.concatenate([up[:tm - SUBLANE], jnp.where(rid == SUBLANE - 1, next_row, up[tm - SUBLANE:])], axis=0)
        y = um * cw_ref[0:1, c0:c0 + n] + u * cw_ref[1:2, c0:c0 + n] + up * cw_ref[2:3, c0:c0 + n]
        return _silu(y)

    cq = conv_silu(gq_ref, gqp_ref, gqn_ref, 0)
    ck = conv_silu(gk_ref, gkp_ref, gkn_ref, GDN_HEADS * GDN_DK)
    v_ref[...] = conv_silu(gv_ref, gvp_ref, gvn_ref, 2 * GDN_HEADS * GDN_DK)
    for h in range(GDN_HEADS):
        sl = slice(h * GDN_DK, (h + 1) * GDN_DK)
        qh = cq[:, sl]
        q_ref[:, sl] = qh * lax.rsqrt(jnp.sum(qh * qh, axis=-1, keepdims=True) + EPS) * (GDN_DK ** -0.5)
        kh = ck[:, sl]
        k_ref[:, sl] = kh * lax.rsqrt(jnp.sum(kh * kh, axis=-1, keepdims=True) + EPS)

    pre = gab_ref[...]
    lane = lax.broadcasted_iota(jnp.int32, pre.shape, 1) % LANE
    sp_in = pre + dt_ref[...]
    softplus = jnp.maximum(sp_in, 0.0) + jnp.log(1.0 + jnp.exp(-jnp.abs(sp_in)))
    g = -jnp.exp(alog_ref[...]) * softplus
    beta = 1.0 / (1.0 + jnp.exp(-pre))
    gb_ref[...] = jnp.where(lane < GDN_HEADS, g, jnp.where(lane < 2 * GDN_HEADS, beta, 0.0))


def prep(p, wts, rope_tabs, row_off, t_all, shared=None):
    pq, pkv, pkr, gq, gk, gv, gab = (p[k] for k in ("pq", "pkv", "pkr", "gq", "gk", "gv", "gab"))
    B, T, _ = pq.shape
    tm = min(256, T)
    assert row_off % tm == 0
    nt = T // tm
    tb = tm // SUBLANE
    nb = T // SUBLANE
    ro = row_off // tm
    cur = lambda n: pl.BlockSpec((None, tm, n), lambda b, i: (b, i, 0))
    dst = lambda n: pl.BlockSpec((None, tm, n), lambda b, i: (b, i + ro, 0))
    prv = lambda n: pl.BlockSpec((None, SUBLANE, n), lambda b, i: (b, jnp.maximum(i * tb - 1, 0), 0))
    nxt = lambda n: pl.BlockSpec((None, SUBLANE, n), lambda b, i: (b, jnp.minimum((i + 1) * tb, nb - 1), 0))
    full = lambda a: pl.BlockSpec(a.shape, lambda b, i: (0,) * a.ndim)
    tab = pl.BlockSpec((tm, LANE), lambda b, i: (i, 0))
    W = GDN_HEADS * GDN_DK
    outs = [("Q", MLA_HEADS * HEAD_PAD, BF16), ("K", MLA_HEADS * HEAD_PAD, BF16), ("VT", None, BF16),
            ("q", W, F32), ("k", W, F32), ("v", W, F32), ("gb", 2 * LANE, F32)]
    HP = MLA_HEADS * HEAD_PAD
    shared = [] if shared is None else [shared[n] for n, _, _ in outs[1:]]
    n_in = 26
    res = pl.pallas_call(
        functools.partial(_prep_kernel, nt=nt, tm=tm), grid=(B, nt),
        in_specs=[cur(256), cur(128), cur(128), cur(W), cur(W), cur(W), prv(W), prv(W), prv(W),
                  nxt(W), nxt(W), nxt(W), cur(256)] + [full(a) for a in wts[:7]] + [tab, tab, tab]
                 + [full(a) for a in wts[7:]] + [pl.BlockSpec(memory_space=pl.ANY)] * len(shared),
        out_specs=[cur(outs[0][1])] + [dst(n) if n else pl.BlockSpec((None, HP, tm), lambda b, i: (b, 0, i + ro))
                                       for _, n, _ in outs[1:]],
        out_shape=[jax.ShapeDtypeStruct((B, T, outs[0][1]), outs[0][2])]
                  + [jax.ShapeDtypeStruct((B, t_all, n) if n else (B, HP, t_all), dt) for _, n, dt in outs[1:]],
        input_output_aliases={n_in + k: 1 + k for k in range(len(shared))},
        compiler_params=_cp("parallel", "parallel"), name="prep",
    )(pq, pkv, pkr, gq, gk, gv, gq, gk, gv, gq, gk, gv, gab, *wts[:7], *rope_tabs, *wts[7:], *shared)
    return dict(zip([n for n, _, _ in outs], res))


def _attn_kernel(q_ref, k_ref, vt_ref, o_ref, sa_ref, sb_ref, *, ck, nk):
    q = q_ref[...]
    tq = q.shape[0]

    def scores(j):
        return lax.dot_general(k_ref[j * ck:(j + 1) * ck, :], q, NT, preferred_element_type=F32)

    nv = MLA_V + 16

    def update(carry, s_ref, j):
        m, acc = carry
        s = s_ref[...]
        m_new = jnp.maximum(m, jnp.max(s, axis=0, keepdims=True))
        p = jnp.exp2(s - m_new).astype(BF16)
        acc = jnp.exp2(m - m_new) * acc + jnp.dot(vt_ref[0:nv, j * ck:(j + 1) * ck], p, preferred_element_type=F32)
        return m_new, acc

    bufs = (sa_ref, sb_ref)
    bufs[0][...] = scores(0)
    carry = (jnp.full((1, tq), -1e30, F32), jnp.zeros((nv, tq), F32))
    for j in range(nk):
        if j + 1 < nk:
            bufs[(j + 1) % 2][...] = scores(j + 1)
        carry = update(carry, bufs[j % 2], j)
    acc = carry[1]
    o = acc[:MLA_V] / acc[MLA_V:MLA_V + 1]
    o_ref[...] = o.astype(o_ref.dtype)


def attention(Q, K, VT, k_off, Tk):
    B, Tq, _ = Q.shape
    assert k_off % Tk == 0
    kb = k_off // Tk
    tq = min(512, Tq)
    ck = next(c for c in (768, 512, 384, 256, 128) if Tk % c == 0)
    qo = pl.BlockSpec((None, tq, HEAD_PAD), lambda b, h, i: (b, i, h))
    return pl.pallas_call(
        functools.partial(_attn_kernel, ck=ck, nk=Tk // ck), grid=(B, MLA_HEADS, Tq // tq),
        in_specs=[qo, pl.BlockSpec((None, Tk, HEAD_PAD), lambda b, h, i: (b, kb, h)),
                  pl.BlockSpec((None, HEAD_PAD, Tk), lambda b, h, i: (b, h, kb))],
        out_specs=pl.BlockSpec((None, MLA_V, tq), lambda b, h, i: (b, h, i)),
        out_shape=jax.ShapeDtypeStruct((B, MLA_HEADS * MLA_V, Tq), BF16),
        scratch_shapes=[pltpu.VMEM((ck, tq), F32), pltpu.VMEM((ck, tq), F32)],
        compiler_params=_cp("parallel", "parallel", "parallel"), name="attention",
    )(Q, K, VT)


GDN_GROUP = 4


def _gdn_prep_kernel(q_ref, k_ref, v_ref, gb_ref, wq_ref, u_ref, qk_ref, kdt_ref, egl_ref, *, C, G):
    H, DK = GDN_HEADS, GDN_DK
    fwd = pl.program_id(1) == 0
    dot = functools.partial(jnp.dot, preferred_element_type=F32)
    row = lax.broadcasted_iota(jnp.int32, (C, H * C), 0)
    lane = lax.broadcasted_iota(jnp.int32, (C, H * C), 1)
    col = lane & (C - 1)
    hmask = [(lane >> int(math.log2(C))) == h for h in range(H)]
    wide = lax.broadcasted_iota(jnp.int32, (C, H * DK), 1)
    kmask = [(wide >> int(math.log2(DK))) == h for h in range(H)]
    ahead = jnp.where(fwd, row - col, col - row)
    incl = ahead >= 0
    strict = ahead > 0
    eye = (row == col).astype(F32)
    r1 = lax.broadcasted_iota(jnp.int32, (C, C), 0)
    c1 = lax.broadcasted_iota(jnp.int32, (C, C), 1)
    incl16 = (jnp.where(fwd, r1 - c1, c1 - r1) >= 0).astype(F32).astype(BF16)
    eye16 = (r1 == c1).astype(F32).astype(BF16)

    def blockdiag(m, masks):
        return jnp.concatenate([jnp.where(mk, m, 0.0) for mk in masks], axis=0).astype(BF16)

    def per_head(cols, width):
        n = cols.shape[0]
        if width == LANE:
            return jnp.concatenate([jnp.broadcast_to(cols[:, h:h + 1], (n, LANE)) for h in range(H)], axis=1)
        low = lax.broadcasted_iota(jnp.int32, (n, LANE), 1) < width
        return jnp.concatenate([jnp.where(low, cols[:, h:h + 1], cols[:, h + 1:h + 2]) for h in range(0, H, 2)], axis=1)

    def terms(v):
        hi = v.astype(BF16)
        rest = v - hi.astype(F32)
        mid = rest.astype(BF16)
        return jnp.concatenate([hi, mid, (rest - mid.astype(F32)).astype(BF16)], axis=1)

    fold = lambda a, axis: sum(jnp.split(a, 3, axis=axis)[1:], jnp.split(a, 3, axis=axis)[0])
    chunks = range(G)
    rows = [slice(g * C, (g + 1) * C) for g in chunks]
    gb = [gb_ref[r, :] for r in rows]
    gterms = [terms(v) for v in gb]
    gc = [fold(dot(incl16, t), 1) for t in gterms]
    gct = [fold(lax.dot_general(t, incl16, (((0,), (1,)), ((), ())), preferred_element_type=F32), 0)
           for t in gterms]
    glast = [jnp.where(fwd, v[C - 1:C, :], v[0:1, :]) for v in gc]
    k16 = [k_ref[r, :].astype(BF16) for r in rows]
    qkk = [lax.dot_general(jnp.concatenate([q_ref[rows[g], :].astype(BF16), k16[g]], axis=0),
                           blockdiag(k_ref[rows[g], :], kmask), NT, preferred_element_type=F32)
           for g in chunks]
    a, tinv = [], []
    same = lambda s: (row >> s) == (col >> s)
    pairs = jnp.where(same(1), 1.0, 0.0)
    for g in chunks:
        egl_ref[g] = jnp.broadcast_to(jnp.exp(glast[g]), (SUBLANE, LANE))
        grow = jnp.concatenate([gct[g][h:h + 1, :] for h in range(H)], axis=1)
        decay = jnp.exp(jnp.where(incl, per_head(gc[g], C) - grow, -1e30))
        qk_ref[g] = (qkk[g][:C] * decay).astype(BF16)
        a.append(jnp.where(strict, qkk[g][C:] * decay, 0.0) * per_head(gb[g][:, H:], C))
        tinv.append(eye - a[g] * pairs)
    for s in range(1, int(math.log2(C))):
        join = jnp.where(same(s + 1), jnp.where(same(s), 0.0, 1.0), 0.0)
        x16 = [tinv[g].astype(BF16) for g in chunks]
        xl = [dot(x16[g], blockdiag(a[g] * join, hmask)) for g in chunks]
        xlx = [dot(xl[g].astype(BF16), blockdiag(tinv[g], hmask)) for g in chunks]
        tinv = [tinv[g] - xlx[g] for g in chunks]
    gcw = [per_head(gc[g], DK) for g in chunks]
    bw = [per_head(gb[g][:, H:], DK) for g in chunks]
    wu = [dot(tinv[g].astype(BF16),
              jnp.concatenate([blockdiag(k_ref[rows[g], :] * (bw[g] * jnp.exp(gcw[g])), kmask),
                               blockdiag(v_ref[rows[g], :] * bw[g], kmask)], axis=1)) for g in chunks]
    kdt = [lax.dot_general((k_ref[rows[g], :] * jnp.exp(per_head(glast[g], DK) - gcw[g])).astype(BF16), eye16,
                           TN, preferred_element_type=F32) for g in chunks]
    for g in chunks:
        wq_ref[g, 0:C, :] = wu[g][:, :H * DK].astype(BF16)
        wq_ref[g, C:2 * C, :] = (q_ref[rows[g], :] * jnp.exp(gcw[g])).astype(BF16)
        u_ref[g] = wu[g][:, H * DK:]
        for h in range(H):
            kdt_ref[g, :, h * C:(h + 1) * C] = kdt[g][h * DK:(h + 1) * DK].astype(BF16)


def _gdn_rec_kernel(*refs, C, G, B):
    ins = (refs[0:5], refs[5:10])
    outs = refs[10:12]
    s_ref = refs[12]
    dot = functools.partial(jnp.dot, preferred_element_type=F32)

    @pl.when(pl.program_id(0) == 0)
    def _():
        s_ref[...] = jnp.zeros_like(s_ref)

    hsl = lambda h: slice(h * GDN_DK, (h + 1) * GDN_DK)
    csl = lambda h: slice(h * C, (h + 1) * C)
    for step in range(G):
        chains = [(d, b, h, step if d == 0 else G - 1 - step)
                  for d in range(2) for b in range(B) for h in range(GDN_HEADS)]
        S = {c: s_ref[c[0], c[1], c[2]] for c in chains}
        r = {(d, b, h, g): dot(ins[d][0][b, g, :, hsl(h)], S[(d, b, h, g)].astype(BF16))
             for (d, b, h, g) in chains}
        vn = {(d, b, h, g): (ins[d][1][b, g, :, hsl(h)] - r[(d, b, h, g)][:C]).astype(BF16) for (d, b, h, g) in chains}
        o = {(d, b, h, g): dot(ins[d][2][b, g, :, csl(h)], vn[(d, b, h, g)]) for (d, b, h, g) in chains}
        upd = {(d, b, h, g): dot(ins[d][3][b, g, :, csl(h)], vn[(d, b, h, g)]) for (d, b, h, g) in chains}
        for c in chains:
            d, b, h, g = c
            outs[d][b, g, :, hsl(h)] = r[c][C:] + o[c]
            s_ref[d, b, h] = S[c] * ins[d][4][b, g, 0:1, h:h + 1] + upd[c]


def gdn_scan(q, k, v, gb, n_ctx):
    B, Tt, W = q.shape
    C, G = GDN_CHUNK, GDN_GROUP
    n = Tt // C
    ng = n // G
    ncg = n_ctx // (C * G)
    assert n % G == 0 and n_ctx % (C * G) == 0
    gp = next(c for c in (12, 8, 6, 4, 3, 2, 1) if n % c == 0)
    tok = pl.BlockSpec((None, gp * C, W), lambda b, d, s: (b, s, 0))
    shapes = [((2 * C, W), BF16), ((C, W), F32), ((C, GDN_HEADS * C), BF16), ((GDN_DK, GDN_HEADS * C), BF16),
              ((SUBLANE, LANE), F32)]
    nat = lambda b, d, s: (b, d, s, 0, 0)
    mid = pl.pallas_call(
        functools.partial(_gdn_prep_kernel, C=C, G=gp), grid=(B, 2, n // gp),
        in_specs=[tok, tok, tok, pl.BlockSpec((None, gp * C, LANE), lambda b, d, s: (b, s, d))],
        out_specs=[pl.BlockSpec((None, None, gp, r, w), nat) for (r, w), _ in shapes],
        out_shape=[jax.ShapeDtypeStruct((B, 2, n, r, w), dt) for (r, w), dt in shapes],
        compiler_params=_cp("parallel", "parallel", "parallel"), name="gdn_prep",
    )(q, k, v, gb)

    fwd = lambda s: jnp.where(s < ncg, ng - ncg + s, s - ncg)
    bwd = lambda s: ng - 1 - s
    both = lambda r, w, d: pl.BlockSpec((B, None, G, r, w), (lambda s: (0, 0, fwd(s), 0, 0)) if d == 0
                                        else (lambda s: (0, 1, bwd(s), 0, 0)))
    o_spec = lambda d: pl.BlockSpec((B, G, C, W), (lambda s: (0, fwd(s), 0, 0)) if d == 0
                                    else (lambda s: (0, bwd(s), 0, 0)))
    o_f, o_b = pl.pallas_call(
        functools.partial(_gdn_rec_kernel, C=C, G=G, B=B), grid=(ng,),
        in_specs=[both(r, w, d) for d in range(2) for (r, w), _ in shapes],
        out_specs=[o_spec(0), o_spec(1)],
        out_shape=[jax.ShapeDtypeStruct((B, n, C, W), F32)] * 2,
        scratch_shapes=[pltpu.VMEM((2, B, GDN_HEADS, GDN_DK, GDN_DV), F32)],
        compiler_params=_cp("arbitrary"), name="gdn_rec",
    )(*mid, *mid)
    return o_f.reshape(B, Tt, W), o_b.reshape(B, Tt, W)


def _mixout_kernel(att_ref, *refs, tm, T, nt, n_o):
    o_refs, refs = refs[:2 * n_o], refs[2 * n_o:]
    (z_ref, u_ref, up_ref, un_ref, x_ref, g1_ref, sh2_ref, sc2_ref,
     gng_ref, wa_ref, wg_ref, wp_ref, wbd_ref, ps_ref, n2g_ref, wr_ref, x1_ref, h2_ref, aff_ref) = refs
    i = pl.program_id(1)
    o = jnp.concatenate([o_refs[k][...] + o_refs[n_o + k][...] for k in range(n_o)], axis=0)
    z = z_ref[...]
    parts = []
    for h in range(GDN_HEADS):
        sl = slice(h * GDN_DV, (h + 1) * GDN_DV)
        oh = o[:, sl]
        oh = oh * lax.rsqrt(jnp.mean(oh * oh, axis=-1, keepdims=True) + EPS) * gng_ref[...]
        parts.append((oh * _silu(z[:, sl])).astype(BF16))
    gdn = jnp.concatenate(parts, axis=1)

    u = u_ref[...]
    halo = SUBLANE
    ext = jnp.concatenate([jnp.where(i > 0, up_ref[...], 0.0), u, jnp.where(i < nt - 1, un_ref[...], 0.0)], axis=0)
    n_ext = tm + 2 * halo
    back = lambda a, s: pltpu.roll(a, s, 0)
    ahead = lambda a, s: pltpu.roll(a, n_ext - s, 0)
    s2 = ext + back(ext, 1)
    s4 = back(s2, 1) + ahead(s2, 1)
    s8 = back(s4, 2) + ahead(s4, 2)
    s16 = back(s8, 4) + ahead(s8, 4)
    t = i * tm + lax.broadcasted_iota(jnp.int32, (tm, 1), 0)
    lane = lax.broadcasted_iota(jnp.int32, (tm, POOL_WIDTH), 1)
    mean = None
    for gi, (win, sw) in reversed(list(enumerate(zip(POOL_WINDOWS, (s2, s4, s8, s16))))):
        lo = jnp.maximum(t - win // 2, 0)
        hi = jnp.minimum(t - win // 2 + win, T)
        m = sw[halo:halo + tm, :] / (hi - lo).astype(F32)
        mean = m if mean is None else jnp.where(lane < (gi + 1) * POOL_GROUP, m, mean)
    yp = jnp.dot((mean - u).astype(BF16), wbd_ref[...], preferred_element_type=F32) * ps_ref[...]

    y = (lax.dot_general(att_ref[...], wa_ref[...], TN, preferred_element_type=F32)
         + jnp.dot(gdn, wg_ref[...], preferred_element_type=F32)
         + jnp.dot(yp.astype(BF16), wp_ref[...], preferred_element_type=F32))
    x1 = x_ref[...] + g1_ref[...] * y
    x1_ref[...] = x1
    h2 = x1 * lax.rsqrt(jnp.mean(x1 * x1, axis=-1, keepdims=True) + EPS) * n2g_ref[...]
    h2 = h2 * (1.0 + sc2_ref[...]) + sh2_ref[...]
    h2_ref[...] = h2.astype(BF16)
    lg = lax.dot_general(wr_ref[...], h2, NT, precision=HI, preferred_element_type=F32)
    e = jnp.exp(lg - jnp.max(lg, axis=0, keepdims=True))
    aff_ref[...] = e / jnp.sum(e, axis=0, keepdims=True)


def mixout(att, o, o_off, z, u, x, mod, wts):
    B, T, D = x.shape
    tm = min(512, T)
    nt = T // tm
    tb = tm // SUBLANE
    nb = T // SUBLANE
    to = math.gcd(tm, o_off) if o_off else tm
    n_o = tm // to
    W = GDN_HEADS * GDN_DV
    cur = lambda n: pl.BlockSpec((None, tm, n), lambda b, i: (b, i, 0))
    odir = [pl.BlockSpec((None, to, W), lambda b, i, k=k: (b, i * n_o + o_off // to + k, 0)) for k in range(n_o)]
    modspec = lambda k: pl.BlockSpec((None, 1, D), lambda b, i: (b, 0, k))
    full = lambda a: pl.BlockSpec(a.shape, lambda b, i: (0,) * a.ndim)
    return pl.pallas_call(
        functools.partial(_mixout_kernel, tm=tm, T=T, nt=nt, n_o=n_o), grid=(B, nt),
        in_specs=[pl.BlockSpec((None, MLA_HEADS * MLA_V, tm), lambda b, i: (b, 0, i))] + odir + odir + [cur(W), cur(POOL_WIDTH),
                  pl.BlockSpec((None, SUBLANE, POOL_WIDTH), lambda b, i: (b, jnp.maximum(i * tb - 1, 0), 0)),
                  pl.BlockSpec((None, SUBLANE, POOL_WIDTH), lambda b, i: (b, jnp.minimum((i + 1) * tb, nb - 1), 0)),
                  cur(D), modspec(2), modspec(3), modspec(4)] + [full(a) for a in wts],
        out_specs=[cur(D), cur(D), pl.BlockSpec((None, N_EXPERTS, tm), lambda b, i: (b, 0, i))],
        out_shape=[jax.ShapeDtypeStruct((B, T, D), F32), jax.ShapeDtypeStruct((B, T, D), BF16),
                   jax.ShapeDtypeStruct((B, N_EXPERTS, T), F32)],
        compiler_params=_cp("parallel", "parallel"), name="mixout",
    )(att, *([o[0]] * n_o), *([o[1]] * n_o), z, u, u, u, x, mod, mod, mod, *wts)


MOE_SUB = 256
MOE_SLOTS = 128


def _route_kernel(aff_ref, gate_ref, slot_ref, starts_ref, *, cap, T):
    aff = aff_ref[...]

    def body(it, res):
        cand = res | jnp.left_shift(jnp.int32(1), 30 - it)
        cnt = jnp.sum((aff >= pltpu.bitcast(cand, F32)).astype(jnp.int32), axis=-1, keepdims=True)
        return jnp.where(cnt >= cap, cand, res)

    bits = lax.fori_loop(0, 31, body, jnp.zeros((N_EXPERTS, 1), jnp.int32))
    thr = pltpu.bitcast(bits, F32)
    above = pltpu.bitcast(bits + 1, F32)
    n_gt = jnp.sum((aff >= above).astype(jnp.int32), axis=-1, keepdims=True)
    need = (cap - n_gt).astype(F32)
    upper = (lax.broadcasted_iota(jnp.int32, (LANE, LANE), 0)
             < lax.broadcasted_iota(jnp.int32, (LANE, LANE), 1)).astype(BF16)
    seen = jnp.zeros((N_EXPERTS, 1), F32)
    taken = jnp.zeros((N_EXPERTS, 1), F32)
    lane = lax.broadcasted_iota(jnp.int32, (N_EXPERTS, LANE), 1)
    starts = jnp.zeros((N_EXPERTS, LANE), jnp.int32)
    per_sub = MOE_SUB // LANE
    for j in range(T // LANE):
        if j % per_sub == 0:
            starts = jnp.where(lane == j // per_sub, taken.astype(jnp.int32), starts)
        sl = slice(j * LANE, (j + 1) * LANE)
        aj = aff[:, sl]
        eq = jnp.where(aj >= thr, jnp.where(aj < above, 1.0, 0.0), 0.0)
        rank = jnp.dot(eq.astype(BF16), upper, preferred_element_type=F32) + seen
        sel = jnp.where(aj >= above, 1.0, jnp.where(rank < need, eq, 0.0))
        gate_ref[:, sl] = sel * aj
        slot = jnp.dot(sel.astype(BF16), upper, preferred_element_type=F32) + taken
        slot_ref[:, sl] = jnp.where(sel > 0.0, slot, -1.0)
        seen = seen + jnp.sum(eq, axis=-1, keepdims=True)
        taken = taken + jnp.sum(sel, axis=-1, keepdims=True)
    starts_ref[...] = jnp.where(lane == T // MOE_SUB, taken.astype(jnp.int32), starts)


def route(aff, cap):
    B, E, T = aff.shape
    assert T % MOE_SUB == 0 and T // MOE_SUB < LANE
    spec = pl.BlockSpec((None, E, T), lambda b: (b, 0, 0))
    return pl.pallas_call(
        functools.partial(_route_kernel, cap=cap, T=T), grid=(B,), in_specs=[spec],
        out_specs=[spec, spec, pl.BlockSpec((None, E, LANE), lambda b: (b, 0, 0))],
        out_shape=[jax.ShapeDtypeStruct(aff.shape, F32), jax.ShapeDtypeStruct(aff.shape, F32),
                   jax.ShapeDtypeStruct((B, E, LANE), jnp.int32)],
        compiler_params=_cp("parallel"), name="route",
    )(aff)


def _slot_blocks(starts_ref, b, e, sub, R):
    s0 = starts_ref[b, e, sub]
    s1 = starts_ref[b, e, sub + 1]
    return s0 // R, (s1 + R - 1) // R


def _moe_ffn_kernel(starts_ref, h_ref, slot_ref, wg_ref, wu_ref, wd_ref, y_ref, xs_ref, *, n_sub, R):
    e, b, j = pl.program_id(0), pl.program_id(1), pl.program_id(2)

    @pl.when(j == 0)
    def _():
        xs_ref[...] = jnp.zeros_like(xs_ref)

    rows = lax.broadcasted_iota(jnp.int32, (R, 1), 0)
    for sub in range(n_sub):
        tsl = slice(sub * MOE_SUB, (sub + 1) * MOE_SUB)
        h = h_ref[tsl, :]
        srow = slot_ref[:, tsl]

        def gather(i, carry):
            base = pl.multiple_of(i * R, R)
            onehot = jnp.where(srow == (base + rows).astype(F32), 1.0, 0.0).astype(BF16)
            xs_ref[pl.ds(base, R), :] += jnp.dot(onehot, h, preferred_element_type=F32)
            return carry

        lax.fori_loop(*_slot_blocks(starts_ref, b, e, j * n_sub + sub, R), gather, 0)

    @pl.when(j == pl.num_programs(2) - 1)
    def _():
        xs = xs_ref[...].astype(BF16)
        a = jnp.dot(xs, wg_ref[...].astype(BF16), preferred_element_type=F32)
        hid = (_silu(a) * jnp.dot(xs, wu_ref[...].astype(BF16), preferred_element_type=F32)).astype(BF16)
        y_ref[...] = jnp.dot(hid, wd_ref[...].astype(BF16), preferred_element_type=F32).astype(BF16)


def _moe_combine_kernel(starts_ref, y_ref, slot_ref, gate_ref, x1_ref, g2_ref, o_ref, acc_ref, *, n_sub, R):
    b, j, e = pl.program_id(0), pl.program_id(1), pl.program_id(2)

    @pl.when(e == 0)
    def _():
        acc_ref[...] = jnp.zeros_like(acc_ref)

    lane = lax.broadcasted_iota(jnp.int32, slot_ref.shape, 1)
    scol = jnp.sum(jnp.where(lane == e, slot_ref[...], 0.0), axis=-1, keepdims=True)
    gcol = jnp.sum(jnp.where(lane == e, gate_ref[...], 0.0), axis=-1, keepdims=True)
    cols = lax.broadcasted_iota(jnp.int32, (1, R), 1)
    for sub in range(n_sub):
        tsl = slice(sub * MOE_SUB, (sub + 1) * MOE_SUB)
        sc, gc = scol[tsl], gcol[tsl]

        def scatter(i, carry):
            base = pl.multiple_of(i * R, R)
            onehot = jnp.where(sc == (base + cols).astype(F32), 1.0, 0.0).astype(BF16)
            acc_ref[tsl, :] += gc * jnp.dot(onehot, y_ref[pl.ds(base, R), :], preferred_element_type=F32)
            return carry

        lax.fori_loop(*_slot_blocks(starts_ref, b, e, j * n_sub + sub, R), scatter, 0)

    @pl.when(e == N_EXPERTS - 1)
    def _():
        o_ref[...] = x1_ref[...] + g2_ref[...] * acc_ref[...]


def moe(h2, routed, x1, mod, wg, wu, wd, layer, cap):
    gate, slot, starts = routed
    B, T, D = x1.shape
    E, F = N_EXPERTS, wg.shape[-1]
    R = min(MOE_SLOTS, cap)
    assert cap % R == 0
    tt = min(2048, T)
    n_sub = tt // MOE_SUB
    y = pl.pallas_call(
        functools.partial(_moe_ffn_kernel, n_sub=n_sub, R=R),
        grid_spec=pltpu.PrefetchScalarGridSpec(
            num_scalar_prefetch=1, grid=(E, B, T // tt),
            in_specs=[pl.BlockSpec((None, tt, D), lambda e, b, j, st: (b, j, 0)),
                      pl.BlockSpec((None, None, 1, tt), lambda e, b, j, st: (b, e, 0, j)),
                      pl.BlockSpec((None, None, D, F), lambda e, b, j, st: (layer, e, 0, 0)),
                      pl.BlockSpec((None, None, D, F), lambda e, b, j, st: (layer, e, 0, 0)),
                      pl.BlockSpec((None, None, F, D), lambda e, b, j, st: (layer, e, 0, 0))],
            out_specs=pl.BlockSpec((None, None, cap, D), lambda e, b, j, st: (b, e, 0, 0)),
            scratch_shapes=[pltpu.VMEM((cap, D), F32)]),
        out_shape=jax.ShapeDtypeStruct((B, E, cap, D), BF16),
        compiler_params=_cp("parallel", "parallel", "arbitrary"), name="moe_ffn",
    )(starts, h2, slot.reshape(B, E, 1, T), wg, wu, wd)
    tc = min(1024, T)
    tok = lambda n: pl.BlockSpec((None, tc, n), lambda b, j, e, st: (b, j, 0))
    return pl.pallas_call(
        functools.partial(_moe_combine_kernel, n_sub=tc // MOE_SUB, R=min(2 * R, cap)),
        grid_spec=pltpu.PrefetchScalarGridSpec(
            num_scalar_prefetch=1, grid=(B, T // tc, E),
            in_specs=[pl.BlockSpec((None, None, cap, D), lambda b, j, e, st: (b, e, 0, 0)),
                      tok(E), tok(E), tok(D), pl.BlockSpec((None, 1, D), lambda b, j, e, st: (b, 0, 5))],
            out_specs=tok(D),
            scratch_shapes=[pltpu.VMEM((tc, D), F32)]),
        out_shape=jax.ShapeDtypeStruct((B, T, D), F32),
        compiler_params=_cp("parallel", "parallel", "arbitrary"), name="moe_combine",
    )(starts, y, jnp.swapaxes(slot, 1, 2), jnp.swapaxes(gate, 1, 2), x1, mod)


def _in_cols():
    src = np.full((IN_PAD,), -1, np.int64)
    splits = (MLA_Q_LORA, MLA_KV_LORA, MLA_ROPE, 512, 512, 512, 512, 2 * GDN_HEADS, 2 * GDN_HEADS, POOL_WIDTH)
    o = np.concatenate([[0], np.cumsum(splits)])
    put = lambda name, at, lo, n: src.__setitem__(slice(SEG[name][0] + at, SEG[name][0] + at + n), np.arange(lo, lo + n))
    put("pq", 0, o[0], MLA_Q_LORA)
    put("pkv", 0, o[1], MLA_KV_LORA)
    put("pkr", MLA_NOPE, o[2], MLA_ROPE)
    for name, k in (("gq", 3), ("gk", 4), ("gv", 5), ("gz", 6)):
        put(name, 0, o[k], 512)
    for d in range(2):
        put("gab", d * LANE, o[7] + d * GDN_HEADS, GDN_HEADS)
        put("gab", d * LANE + GDN_HEADS, o[8] + d * GDN_HEADS, GDN_HEADS)
    put("pool", 0, o[9], POOL_WIDTH)
    return src


def _take_cols(w, src, axis):
    pieces, p, n = [], 0, len(src)
    while p < n:
        q = p + 1
        while q < n and ((src[q] < 0 and src[p] < 0) or (src[p] >= 0 and src[q] == src[q - 1] + 1)):
            q += 1
        if src[p] < 0:
            shape = list(w.shape)
            shape[axis] = q - p
            pieces.append(jnp.zeros(shape, w.dtype))
        else:
            pieces.append(lax.slice_in_dim(w, int(src[p]), int(src[p]) + q - p, axis=axis))
        p = q
    return jnp.concatenate(pieces, axis=axis)


def _head_pad_src(per_head, lo, n):
    src = np.full((MLA_HEADS * HEAD_PAD,), -1, np.int64)
    for h in range(MLA_HEADS):
        src[h * HEAD_PAD:h * HEAD_PAD + n] = h * per_head + lo + np.arange(n)
    return src


def _rope_tables(T, rotate):
    cos = np.ones((T, LANE), np.float32)
    sa = np.zeros((T, LANE), np.float32)
    sb = np.zeros((T, LANE), np.float32)
    if rotate:
        n_freq = MLA_ROPE // 4
        inv = ROPE_THETA ** (-np.arange(n_freq, dtype=np.float64) / n_freq)
        pos_r = np.repeat(np.arange(T // GRID_W, dtype=np.float64), GRID_W)
        pos_c = np.tile(np.arange(GRID_W, dtype=np.float64), T // GRID_W)
        for base, pos in ((MLA_NOPE, pos_r), (MLA_NOPE + 2 * n_freq, pos_c)):
            ang = pos[:, None] * inv[None, :]
            c, s = np.cos(ang), np.sin(ang)
            cos[:, base:base + n_freq] = c
            cos[:, base + n_freq:base + 2 * n_freq] = c
            sa[:, base:base + n_freq] = -s
            sb[:, base + n_freq:base + 2 * n_freq] = s
    return jnp.asarray(cos), jnp.asarray(sa), jnp.asarray(sb)


def _lane_vec(vals_by_dir, at):
    v = jnp.zeros((2, LANE), F32).at[:, at:at + GDN_HEADS].set(vals_by_dir)
    return v.reshape(1, 2 * LANE)


def kernel(x, c, ctx, c_ctx, ada_w, ada_b, norm1_g, norm2_g, w_in, mla_q_a_norm, mla_w_uq, mla_kv_a_norm, mla_w_ukv, mla_q_norm, mla_k_norm, gdn_conv_w, gdn_a_log, gdn_dt_bias, gdn_norm_g, pool_w, pool_scale, w_out, moe_router, moe_w_gate, moe_w_up, moe_w_down):
    B, T, D = x.shape
    Tc = ctx.shape[1]
    L = ada_w.shape[0]
    cvec = jnp.concatenate([c, c_ctx[None, :], jnp.zeros((SUBLANE - B - 1, D), F32)], axis=0)
    mod = ada_mod(cvec, ada_w, ada_b)
    rope_lat = _rope_tables(T, True)
    rope_ctx = _rope_tables(Tc, False)
    in_src = _in_cols()
    uq_src = _head_pad_src(MLA_QK, 0, MLA_QK)
    uk_src = _head_pad_src(MLA_NOPE + MLA_V, 0, MLA_NOPE)
    uv_src = _head_pad_src(MLA_NOPE + MLA_V, MLA_NOPE, MLA_V)
    pad_to = lambda v, n: jnp.pad(v, (0, n - v.shape[0])).reshape(1, n)

    xc = ctx
    for l in range(L):
        need_ctx = l < L - 1
        mod_lat = mod[l, :B].reshape(B, 1, ADA_CHUNKS * D)
        mod_ctx = jnp.broadcast_to(mod[l, B].reshape(1, 1, ADA_CHUNKS * D), (B, 1, ADA_CHUNKS * D))
        w_in_p = _take_cols(w_in[l], in_src, 1).astype(BF16)
        prep_w = (
            pad_to(mla_q_a_norm[l], 256),
            jnp.pad(_take_cols(mla_w_uq[l], uq_src, 1), ((0, 256 - MLA_Q_LORA), (0, 0))).astype(BF16),
            mla_kv_a_norm[l].reshape(1, MLA_KV_LORA),
            _take_cols(mla_w_ukv[l], uk_src, 1).astype(BF16),
            _take_cols(mla_w_ukv[l], uv_src, 1).T.astype(BF16),
            pad_to(mla_q_norm[l] * (MLA_QK ** -0.5 * math.log2(math.e)), HEAD_PAD),
            pad_to(mla_k_norm[l], HEAD_PAD),
            gdn_conv_w[l],
            _lane_vec(gdn_a_log[l], 0),
            _lane_vec(gdn_dt_bias[l], 0),
        )
        wo = w_out[l]
        n_att = MLA_HEADS * MLA_V
        n_gdn = GDN_HEADS * GDN_DV
        wbd = jnp.zeros((POOL_WIDTH, POOL_WIDTH), F32)
        for gi in range(len(POOL_WINDOWS)):
            wbd = wbd.at[gi * POOL_GROUP:(gi + 1) * POOL_GROUP, gi * POOL_GROUP:(gi + 1) * POOL_GROUP].set(pool_w[l, gi])
        mix_w = (
            gdn_norm_g[l].reshape(1, GDN_DV),
            wo[:n_att].astype(BF16),
            wo[n_att:n_att + n_gdn].astype(BF16),
            wo[n_att + n_gdn:].astype(BF16),
            wbd.astype(BF16),
            pool_scale[l].reshape(1, POOL_WIDTH),
            norm2_g[l].reshape(1, D),
            moe_router[l].T,
        )

        p_lat = dict(zip(SEG, inproj(x, mod_lat, norm1_g[l], w_in_p)))
        p_ctx = dict(zip(SEG, inproj(xc, mod_ctx, norm1_g[l], w_in_p)))
        a_ctx = prep(p_ctx, prep_w, rope_ctx, T, T + Tc)
        a_lat = prep(p_lat, prep_w, rope_lat, 0, T + Tc, shared=a_ctx)
        att_l = attention(a_lat["Q"], a_lat["K"], a_lat["VT"], 0, T + Tc)
        o_all = gdn_scan(a_lat["q"], a_lat["k"], a_lat["v"], a_lat["gb"], Tc)

        def channel_mix(att, o_off, p, xin, m):
            Tn = xin.shape[1]
            x1, h2, aff = mixout(att, o_all, o_off, p["gz"], p["pool"], xin, m, mix_w)
            cap = EC_CAPACITY_FACTOR * Tn // N_EXPERTS
            return moe(h2, route(aff, cap), x1, m, moe_w_gate, moe_w_up, moe_w_down, l, cap)

        x = channel_mix(att_l, 0, p_lat, x, mod_lat)
        if need_ctx:
            att_c = attention(a_ctx["Q"], a_lat["K"], a_lat["VT"], T, Tc)
            xc = channel_mix(att_c, T, p_ctx, xc, mod_ctx)
    return x
```

```python
import functools
import math

import numpy as np
import jax
import jax.numpy as jnp
from jax import lax
from jax.experimental import pallas as pl
from jax.experimental.pallas import tpu as pltpu

F32 = jnp.float32
BF16 = jnp.bfloat16
HI = lax.Precision.HIGHEST

EPS = 1e-6
GRID_W = 64
ADA_CHUNKS = 6
MLA_HEADS = 4
MLA_NOPE = 64
MLA_ROPE = 32
MLA_QK = MLA_NOPE + MLA_ROPE
MLA_V = 64
MLA_Q_LORA = 192
MLA_KV_LORA = 128
ROPE_THETA = 10000.0
GDN_HEADS = 4
GDN_DK = 128
GDN_DV = 128
GDN_CHUNK = 64
POOL_WINDOWS = (2, 4, 8, 16)
POOL_GROUP = 64
POOL_WIDTH = POOL_GROUP * len(POOL_WINDOWS)
N_EXPERTS = 16
EC_CAPACITY_FACTOR = 2

LANE = 128
SUBLANE = 8
HEAD_PAD = 128
VMEM_LIMIT = 48 * 1024 * 1024

NT = (((1,), (1,)), ((), ()))
TN = (((0,), (0,)), ((), ()))

SEG = {}
_off = 0
for _name, _w in (("pq", 256), ("pkv", 128), ("pkr", 128), ("gq", 512), ("gk", 512), ("gv", 512),
                  ("gz", 512), ("gab", 256), ("pool", 256)):
    SEG[_name] = (_off, _w)
    _off += _w
IN_PAD = _off


def _cp(*dims):
    return pltpu.CompilerParams(dimension_semantics=dims, vmem_limit_bytes=VMEM_LIMIT)


def _silu(v):
    return v / (1.0 + jnp.exp(-v))


def _ada_kernel(c_ref, w_ref, b_ref, o_ref):
    s = _silu(c_ref[...])
    o_ref[...] = jnp.dot(s, w_ref[...], precision=HI, preferred_element_type=F32) + b_ref[...]


def ada_mod(cvec, ada_w, ada_b):
    L, D, N = ada_w.shape
    tn = N // 4
    return pl.pallas_call(
        _ada_kernel, grid=(L, N // tn),
        in_specs=[pl.BlockSpec((SUBLANE, D), lambda l, j: (0, 0)),
                  pl.BlockSpec((None, D, tn), lambda l, j: (l, 0, j)),
                  pl.BlockSpec((None, 1, tn), lambda l, j: (l, 0, j))],
        out_specs=pl.BlockSpec((None, SUBLANE, tn), lambda l, j: (l, 0, j)),
        out_shape=jax.ShapeDtypeStruct((L, SUBLANE, N), F32),
        compiler_params=_cp("parallel", "parallel"), name="ada_mod",
    )(cvec, ada_w, ada_b.reshape(L, 1, N))


def _inproj_kernel(x_ref, sh_ref, sc_ref, g_ref, w_ref, *out_refs):
    x = x_ref[...]
    h = x * lax.rsqrt(jnp.mean(x * x, axis=-1, keepdims=True) + EPS) * g_ref[...]
    hb = (h * (1.0 + sc_ref[...]) + sh_ref[...]).astype(BF16)
    for (off, n), o_ref in zip(SEG.values(), out_refs):
        o_ref[...] = jnp.dot(hb, w_ref[:, off:off + n], preferred_element_type=F32)


def inproj(x, mod, norm_g, w_in_p):
    B, T, D = x.shape
    tm = min(512, T)
    modspec = lambda k: pl.BlockSpec((None, 1, D), lambda b, i, k=k: (b, 0, k))
    return pl.pallas_call(
        _inproj_kernel, grid=(B, T // tm),
        in_specs=[pl.BlockSpec((None, tm, D), lambda b, i: (b, i, 0)), modspec(0), modspec(1),
                  pl.BlockSpec((1, D), lambda b, i: (0, 0)),
                  pl.BlockSpec((D, IN_PAD), lambda b, i: (0, 0))],
        out_specs=[pl.BlockSpec((None, tm, n), lambda b, i: (b, i, 0)) for _, n in SEG.values()],
        out_shape=[jax.ShapeDtypeStruct((B, T, n), F32) for _, n in SEG.values()],
        compiler_params=_cp("parallel", "parallel"), name="inproj",
    )(x, mod, mod, norm_g.reshape(1, D), w_in_p)


def _prep_kernel(*refs, nt, tm):
    (pq_ref, pkv_ref, pkr_ref, gq_ref, gk_ref, gv_ref, gqp_ref, gkp_ref, gvp_ref,
     gqn_ref, gkn_ref, gvn_ref, gab_ref, qan_ref, wuq_ref, kvan_ref, wuk_ref, wuv_ref,
     qn_ref, kn_ref, cos_ref, sa_ref, sb_ref, cw_ref, alog_ref, dt_ref) = refs[:26]
    Q_ref, K_ref, VT_ref, q_ref, k_ref, v_ref, gb_ref = refs[-7:]
    i = pl.program_id(1)
    cos, sa, sb = cos_ref[...], sa_ref[...], sb_ref[...]

    def rope(xh):
        return xh * cos + pltpu.roll(xh, LANE - 8, 1) * sa + pltpu.roll(xh, 8, 1) * sb

    pq = pq_ref[...]
    qa = pq * lax.rsqrt(jnp.sum(pq * pq, axis=-1, keepdims=True) * (1.0 / MLA_Q_LORA) + EPS) * qan_ref[...]
    qall = jnp.dot(qa.astype(BF16), wuq_ref[...], preferred_element_type=F32)
    pkv = pkv_ref[...]
    kva = (pkv * lax.rsqrt(jnp.mean(pkv * pkv, axis=-1, keepdims=True) + EPS) * kvan_ref[...]).astype(BF16)
    kall = jnp.dot(kva, wuk_ref[...], preferred_element_type=F32)
    vt = lax.dot_general(wuv_ref[...], kva, NT, preferred_element_type=F32)
    ones_row = lax.broadcasted_iota(jnp.int32, vt.shape, 0) % HEAD_PAD == MLA_V
    VT_ref[...] = jnp.where(ones_row, 1.0, vt).astype(BF16)
    pkr = pkr_ref[...]
    for h in range(MLA_HEADS):
        sl = slice(h * HEAD_PAD, (h + 1) * HEAD_PAD)
        qh = qall[:, sl]
        qh = qh * lax.rsqrt(jnp.sum(qh * qh, axis=-1, keepdims=True) * (1.0 / MLA_QK) + EPS) * qn_ref[...]
        Q_ref[:, sl] = rope(qh).astype(BF16)
        kh = kall[:, sl] + pkr
        kh = kh * lax.rsqrt(jnp.sum(kh * kh, axis=-1, keepdims=True) * (1.0 / MLA_QK) + EPS) * kn_ref[...]
        K_ref[:, sl] = rope(kh).astype(BF16)

    rid = lax.broadcasted_iota(jnp.int32, (SUBLANE, GDN_HEADS * GDN_DK), 0)

    def conv_silu(u_ref, up_ref, un_ref, c0):
        u = u_ref[...]
        n = u.shape[1]
        prev_row = jnp.where(i > 0, up_ref[SUBLANE - 1:SUBLANE, :], 0.0)
        next_row = jnp.where(i < nt - 1, un_ref[0:1, :], 0.0)
        um = pltpu.roll(u, 1, 0)
        um = jnp.concatenate([jnp.where(rid == 0, prev_row, um[:SUBLANE]), um[SUBLANE:]], axis=0)
        up = pltpu.roll(u, tm - 1, 0)
        up = jnp.concatenate([up[:tm - SUBLANE], jnp.where(rid == SUBLANE - 1, next_row, up[tm - SUBLANE:])], axis=0)
        y = um * cw_ref[0:1, c0:c0 + n] + u * cw_ref[1:2, c0:c0 + n] + up * cw_ref[2:3, c0:c0 + n]
        return _silu(y)

    cq = conv_silu(gq_ref, gqp_ref, gqn_ref, 0)
    ck = conv_silu(gk_ref, gkp_ref, gkn_ref, GDN_HEADS * GDN_DK)
    v_ref[...] = conv_silu(gv_ref, gvp_ref, gvn_ref, 2 * GDN_HEADS * GDN_DK)
    for h in range(GDN_HEADS):
        sl = slice(h * GDN_DK, (h + 1) * GDN_DK)
        qh = cq[:, sl]
        q_ref[:, sl] = qh * lax.rsqrt(jnp.sum(qh * qh, axis=-1, keepdims=True) + EPS) * (GDN_DK ** -0.5)
        kh = ck[:, sl]
        k_ref[:, sl] = kh * lax.rsqrt(jnp.sum(kh * kh, axis=-1, keepdims=True) + EPS)

    pre = gab_ref[...]
    lane = lax.broadcasted_iota(jnp.int32, pre.shape, 1) % LANE
    sp_in = pre + dt_ref[...]
    softplus = jnp.maximum(sp_in, 0.0) + jnp.log(1.0 + jnp.exp(-jnp.abs(sp_in)))
    g = -jnp.exp(alog_ref[...]) * softplus
    beta = 1.0 / (1.0 + jnp.exp(-pre))
    gb_ref[...] = jnp.where(lane < GDN_HEADS, g, jnp.where(lane < 2 * GDN_HEADS, beta, 0.0))


def prep(p, wts, rope_tabs, row_off, t_all, shared=None):
    pq, pkv, pkr, gq, gk, gv, gab = (p[k] for k in ("pq", "pkv", "pkr", "gq", "gk", "gv", "gab"))
    B, T, _ = pq.shape
    tm = min(256, T)
    assert row_off % tm == 0
    nt = T // tm
    tb = tm // SUBLANE
    nb = T // SUBLANE
    ro = row_off // tm
    cur = lambda n: pl.BlockSpec((None, tm, n), lambda b, i: (b, i, 0))
    dst = lambda n: pl.BlockSpec((None, tm, n), lambda b, i: (b, i + ro, 0))
    prv = lambda n: pl.BlockSpec((None, SUBLANE, n), lambda b, i: (b, jnp.maximum(i * tb - 1, 0), 0))
    nxt = lambda n: pl.BlockSpec((None, SUBLANE, n), lambda b, i: (b, jnp.minimum((i + 1) * tb, nb - 1), 0))
    full = lambda a: pl.BlockSpec(a.shape, lambda b, i: (0,) * a.ndim)
    tab = pl.BlockSpec((tm, LANE), lambda b, i: (i, 0))
    W = GDN_HEADS * GDN_DK
    outs = [("Q", MLA_HEADS * HEAD_PAD, BF16), ("K", MLA_HEADS * HEAD_PAD, BF16), ("VT", None, BF16),
            ("q", W, F32), ("k", W, F32), ("v", W, F32), ("gb", 2 * LANE, F32)]
    HP = MLA_HEADS * HEAD_PAD
    shared = [] if shared is None else [shared[n] for n, _, _ in outs[1:]]
    n_in = 26
    res = pl.pallas_call(
        functools.partial(_prep_kernel, nt=nt, tm=tm), grid=(B, nt),
        in_specs=[cur(256), cur(128), cur(128), cur(W), cur(W), cur(W), prv(W), prv(W), prv(W),
                  nxt(W), nxt(W), nxt(W), cur(256)] + [full(a) for a in wts[:7]] + [tab, tab, tab]
                 + [full(a) for a in wts[7:]] + [pl.BlockSpec(memory_space=pl.ANY)] * len(shared),
        out_specs=[cur(outs[0][1])] + [dst(n) if n else pl.BlockSpec((None, HP, tm), lambda b, i: (b, 0, i + ro))
                                       for _, n, _ in outs[1:]],
        out_shape=[jax.ShapeDtypeStruct((B, T, outs[0][1]), outs[0][2])]
                  + [jax.ShapeDtypeStruct((B, t_all, n) if n else (B, HP, t_all), dt) for _, n, dt in outs[1:]],
        input_output_aliases={n_in + k: 1 + k for k in range(len(shared))},
        compiler_params=_cp("parallel", "parallel"), name="prep",
    )(pq, pkv, pkr, gq, gk, gv, gq, gk, gv, gq, gk, gv, gab, *wts[:7], *rope_tabs, *wts[7:], *shared)
    return dict(zip([n for n, _, _ in outs], res))


def _attn_kernel(q_ref, k_ref, vt_ref, o_ref, sa_ref, sb_ref, *, ck, nk):
    q = q_ref[...]
    tq = q.shape[0]

    def scores(j):
        return lax.dot_general(k_ref[j * ck:(j + 1) * ck, :], q, NT, preferred_element_type=F32)

    nv = MLA_V + 16

    def update(carry, s_ref, j):
        m, acc = carry
        s = s_ref[...]
        m_new = jnp.maximum(m, jnp.max(s, axis=0, keepdims=True))
        p = jnp.exp2(s - m_new).astype(BF16)
        acc = jnp.exp2(m - m_new) * acc + jnp.dot(vt_ref[0:nv, j * ck:(j + 1) * ck], p, preferred_element_type=F32)
        return m_new, acc

    bufs = (sa_ref, sb_ref)
    bufs[0][...] = scores(0)
    carry = (jnp.full((1, tq), -1e30, F32), jnp.zeros((nv, tq), F32))
    for j in range(nk):
        if j + 1 < nk:
            bufs[(j + 1) % 2][...] = scores(j + 1)
        carry = update(carry, bufs[j % 2], j)
    acc = carry[1]
    o = acc[:MLA_V] / acc[MLA_V:MLA_V + 1]
    o_ref[...] = o.astype(o_ref.dtype)


def attention(Q, K, VT, k_off, Tk):
    B, Tq, _ = Q.shape
    assert k_off % Tk == 0
    kb = k_off // Tk
    tq = min(512, Tq)
    ck = next(c for c in (768, 512, 384, 256, 128) if Tk % c == 0)
    qo = pl.BlockSpec((None, tq, HEAD_PAD), lambda b, h, i: (b, i, h))
    return pl.pallas_call(
        functools.partial(_attn_kernel, ck=ck, nk=Tk // ck), grid=(B, MLA_HEADS, Tq // tq),
        in_specs=[qo, pl.BlockSpec((None, Tk, HEAD_PAD), lambda b, h, i: (b, kb, h)),
                  pl.BlockSpec((None, HEAD_PAD, Tk), lambda b, h, i: (b, h, kb))],
        out_specs=pl.BlockSpec((None, MLA_V, tq), lambda b, h, i: (b, h, i)),
        out_shape=jax.ShapeDtypeStruct((B, MLA_HEADS * MLA_V, Tq), BF16),
        scratch_shapes=[pltpu.VMEM((ck, tq), F32), pltpu.VMEM((ck, tq), F32)],
        compiler_params=_cp("parallel", "parallel", "parallel"), name="attention",
    )(Q, K, VT)


GDN_GROUP = 4


def _gdn_prep_kernel(q_ref, k_ref, v_ref, gb_ref, wq_ref, u_ref, qk_ref, kdt_ref, egl_ref, *, C, G):
    H, DK = GDN_HEADS, GDN_DK
    fwd = pl.program_id(1) == 0
    dot = functools.partial(jnp.dot, preferred_element_type=F32)
    row = lax.broadcasted_iota(jnp.int32, (C, H * C), 0)
    lane = lax.broadcasted_iota(jnp.int32, (C, H * C), 1)
    col = lane & (C - 1)
    hmask = [(lane >> int(math.log2(C))) == h for h in range(H)]
    wide = lax.broadcasted_iota(jnp.int32, (C, H * DK), 1)
    kmask = [(wide >> int(math.log2(DK))) == h for h in range(H)]
    ahead = jnp.where(fwd, row - col, col - row)
    incl = ahead >= 0
    strict = ahead > 0
    eye = (row == col).astype(F32)
    r1 = lax.broadcasted_iota(jnp.int32, (C, C), 0)
    c1 = lax.broadcasted_iota(jnp.int32, (C, C), 1)
    incl16 = (jnp.where(fwd, r1 - c1, c1 - r1) >= 0).astype(F32).astype(BF16)
    eye16 = (r1 == c1).astype(F32).astype(BF16)

    def blockdiag(m, masks):
        return jnp.concatenate([jnp.where(mk, m, 0.0) for mk in masks], axis=0).astype(BF16)

    def per_head(cols, width):
        n = cols.shape[0]
        if width == LANE:
            return jnp.concatenate([jnp.broadcast_to(cols[:, h:h + 1], (n, LANE)) for h in range(H)], axis=1)
        low = lax.broadcasted_iota(jnp.int32, (n, LANE), 1) < width
        return jnp.concatenate([jnp.where(low, cols[:, h:h + 1], cols[:, h + 1:h + 2]) for h in range(0, H, 2)], axis=1)

    def terms(v):
        hi = v.astype(BF16)
        rest = v - hi.astype(F32)
        mid = rest.astype(BF16)
        return jnp.concatenate([hi, mid, (rest - mid.astype(F32)).astype(BF16)], axis=1)

    fold = lambda a, axis: sum(jnp.split(a, 3, axis=axis)[1:], jnp.split(a, 3, axis=axis)[0])
    chunks = range(G)
    rows = [slice(g * C, (g + 1) * C) for g in chunks]
    gb = [gb_ref[r, :] for r in rows]
    gterms = [terms(v) for v in gb]
    gc = [fold(dot(incl16, t), 1) for t in gterms]
    gct = [fold(lax.dot_general(t, incl16, (((0,), (1,)), ((), ())), preferred_element_type=F32), 0)
           for t in gterms]
    glast = [jnp.where(fwd, v[C - 1:C, :], v[0:1, :]) for v in gc]
    k16 = [k_ref[r, :].astype(BF16) for r in rows]
    qkk = [lax.dot_general(jnp.concatenate([q_ref[rows[g], :].astype(BF16), k16[g]], axis=0),
                           blockdiag(k_ref[rows[g], :], kmask), NT, preferred_element_type=F32)
           for g in chunks]
    a, tinv = [], []
    same = lambda s: (row >> s) == (col >> s)
    pairs = jnp.where(same(1), 1.0, 0.0)
    for g in chunks:
        egl_ref[g] = jnp.broadcast_to(jnp.exp(glast[g]), (SUBLANE, LANE))
        grow = jnp.concatenate([gct[g][h:h + 1, :] for h in range(H)], axis=1)
        decay = jnp.exp(jnp.where(incl, per_head(gc[g], C) - grow, -1e30))
        qk_ref[g] = (qkk[g][:C] * decay).astype(BF16)
        a.append(jnp.where(strict, qkk[g][C:] * decay, 0.0) * per_head(gb[g][:, H:], C))
        tinv.append(eye - a[g] * pairs)
    for s in range(1, int(math.log2(C))):
        join = jnp.where(same(s + 1), jnp.where(same(s), 0.0, 1.0), 0.0)
        x16 = [tinv[g].astype(BF16) for g in chunks]
        xl = [dot(x16[g], blockdiag(a[g] * join, hmask)) for g in chunks]
        xlx = [dot(xl[g].astype(BF16), blockdiag(tinv[g], hmask)) for g in chunks]
        tinv = [tinv[g] - xlx[g] for g in chunks]
    gcw = [per_head(gc[g], DK) for g in chunks]
    bw = [per_head(gb[g][:, H:], DK) for g in chunks]
    wu = [dot(tinv[g].astype(BF16),
              jnp.concatenate([blockdiag(k_ref[rows[g], :] * (bw[g] * jnp.exp(gcw[g])), kmask),
                               blockdiag(v_ref[rows[g], :] * bw[g], kmask)], axis=1)) for g in chunks]
    kdt = [lax.dot_general((k_ref[rows[g], :] * jnp.exp(per_head(glast[g], DK) - gcw[g])).astype(BF16), eye16,
                           TN, preferred_element_type=F32) for g in chunks]
    for g in chunks:
        wq_ref[g, 0:C, :] = wu[g][:, :H * DK].astype(BF16)
        wq_ref[g, C:2 * C, :] = (q_ref[rows[g], :] * jnp.exp(gcw[g])).astype(BF16)
        u_ref[g] = wu[g][:, H * DK:]
        for h in range(H):
            kdt_ref[g, :, h * C:(h + 1) * C] = kdt[g][h * DK:(h + 1) * DK].astype(BF16)


def _gdn_rec_kernel(*refs, C, G, B):
    ins = (refs[0:5], refs[5:10])
    outs = refs[10:12]
    s_ref = refs[12]
    dot = functools.partial(jnp.dot, preferred_element_type=F32)

    @pl.when(pl.program_id(0) == 0)
    def _():
        s_ref[...] = jnp.zeros_like(s_ref)

    hsl = lambda h: slice(h * GDN_DK, (h + 1) * GDN_DK)
    csl = lambda h: slice(h * C, (h + 1) * C)
    for step in range(G):
        chains = [(d, b, h, step if d == 0 else G - 1 - step)
                  for d in range(2) for b in range(B) for h in range(GDN_HEADS)]
        S = {c: s_ref[c[0], c[1], c[2]] for c in chains}
        r = {(d, b, h, g): dot(ins[d][0][b, g, :, hsl(h)], S[(d, b, h, g)].astype(BF16))
             for (d, b, h, g) in chains}
        vn = {(d, b, h, g): (ins[d][1][b, g, :, hsl(h)] - r[(d, b, h, g)][:C]).astype(BF16) for (d, b, h, g) in chains}
        o = {(d, b, h, g): dot(ins[d][2][b, g, :, csl(h)], vn[(d, b, h, g)]) for (d, b, h, g) in chains}
        upd = {(d, b, h, g): dot(ins[d][3][b, g, :, csl(h)], vn[(d, b, h, g)]) for (d, b, h, g) in chains}
        for c in chains:
            d, b, h, g = c
            outs[d][b, g, :, hsl(h)] = r[c][C:] + o[c]
            s_ref[d, b, h] = S[c] * ins[d][4][b, g, 0:1, h:h + 1] + upd[c]


def gdn_scan(q, k, v, gb, n_ctx):
    B, Tt, W = q.shape
    C, G = GDN_CHUNK, GDN_GROUP
    n = Tt // C
    ng = n // G
    ncg = n_ctx // (C * G)
    assert n % G == 0 and n_ctx % (C * G) == 0
    gp = next(c for c in (12, 8, 6, 4, 3, 2, 1) if n % c == 0)
    tok = pl.BlockSpec((None, gp * C, W), lambda b, d, s: (b, s, 0))
    shapes = [((2 * C, W), BF16), ((C, W), F32), ((C, GDN_HEADS * C), BF16), ((GDN_DK, GDN_HEADS * C), BF16),
              ((SUBLANE, LANE), F32)]
    nat = lambda b, d, s: (b, d, s, 0, 0)
    mid = pl.pallas_call(
        functools.partial(_gdn_prep_kernel, C=C, G=gp), grid=(B, 2, n // gp),
        in_specs=[tok, tok, tok, pl.BlockSpec((None, gp * C, LANE), lambda b, d, s: (b, s, d))],
        out_specs=[pl.BlockSpec((None, None, gp, r, w), nat) for (r, w), _ in shapes],
        out_shape=[jax.ShapeDtypeStruct((B, 2, n, r, w), dt) for (r, w), dt in shapes],
        compiler_params=_cp("parallel", "parallel", "parallel"), name="gdn_prep",
    )(q, k, v, gb)

    fwd = lambda s: jnp.where(s < ncg, ng - ncg + s, s - ncg)
    bwd = lambda s: ng - 1 - s
    both = lambda r, w, d: pl.BlockSpec((B, None, G, r, w), (lambda s: (0, 0, fwd(s), 0, 0)) if d == 0
                                        else (lambda s: (0, 1, bwd(s), 0, 0)))
    o_spec = lambda d: pl.BlockSpec((B, G, C, W), (lambda s: (0, fwd(s), 0, 0)) if d == 0
                                    else (lambda s: (0, bwd(s), 0, 0)))
    o_f, o_b = pl.pallas_call(
        functools.partial(_gdn_rec_kernel, C=C, G=G, B=B), grid=(ng,),
        in_specs=[both(r, w, d) for d in range(2) for (r, w), _ in shapes],
        out_specs=[o_spec(0), o_spec(1)],
        out_shape=[jax.ShapeDtypeStruct((B, n, C, W), F32)] * 2,
        scratch_shapes=[pltpu.VMEM((2, B, GDN_HEADS, GDN_DK, GDN_DV), F32)],
        compiler_params=_cp("arbitrary"), name="gdn_rec",
    )(*mid, *mid)
    return o_f.reshape(B, Tt, W), o_b.reshape(B, Tt, W)


def _mixout_kernel(att_ref, *refs, tm, T, nt, n_o):
    o_refs, refs = refs[:2 * n_o], refs[2 * n_o:]
    (z_ref, u_ref, up_ref, un_ref, x_ref, g1_ref, sh2_ref, sc2_ref,
     gng_ref, wa_ref, wg_ref, wp_ref, wbd_ref, ps_ref, n2g_ref, wr_ref, x1_ref, h2_ref, aff_ref) = refs
    i = pl.program_id(1)
    o = jnp.concatenate([o_refs[k][...] + o_refs[n_o + k][...] for k in range(n_o)], axis=0)
    z = z_ref[...]
    parts = []
    for h in range(GDN_HEADS):
        sl = slice(h * GDN_DV, (h + 1) * GDN_DV)
        oh = o[:, sl]
        oh = oh * lax.rsqrt(jnp.mean(oh * oh, axis=-1, keepdims=True) + EPS) * gng_ref[...]
        parts.append((oh * _silu(z[:, sl])).astype(BF16))
    gdn = jnp.concatenate(parts, axis=1)

    u = u_ref[...]
    halo = SUBLANE
    ext = jnp.concatenate([jnp.where(i > 0, up_ref[...], 0.0), u, jnp.where(i < nt - 1, un_ref[...], 0.0)], axis=0)
    n_ext = tm + 2 * halo
    back = lambda a, s: pltpu.roll(a, s, 0)
    ahead = lambda a, s: pltpu.roll(a, n_ext - s, 0)
    s2 = ext + back(ext, 1)
    s4 = back(s2, 1) + ahead(s2, 1)
    s8 = back(s4, 2) + ahead(s4, 2)
    s16 = back(s8, 4) + ahead(s8, 4)
    t = i * tm + lax.broadcasted_iota(jnp.int32, (tm, 1), 0)
    lane = lax.broadcasted_iota(jnp.int32, (tm, POOL_WIDTH), 1)
    mean = None
    for gi, (win, sw) in reversed(list(enumerate(zip(POOL_WINDOWS, (s2, s4, s8, s16))))):
        lo = jnp.maximum(t - win // 2, 0)
        hi = jnp.minimum(t - win // 2 + win, T)
        m = sw[halo:halo + tm, :] / (hi - lo).astype(F32)
        mean = m if mean is None else jnp.where(lane < (gi + 1) * POOL_GROUP, m, mean)
    yp = jnp.dot((mean - u).astype(BF16), wbd_ref[...], preferred_element_type=F32) * ps_ref[...]

    y = (lax.dot_general(att_ref[...], wa_ref[...], TN, preferred_element_type=F32)
         + jnp.dot(gdn, wg_ref[...], preferred_element_type=F32)
         + jnp.dot(yp.astype(BF16), wp_ref[...], preferred_element_type=F32))
    x1 = x_ref[...] + g1_ref[...] * y
    x1_ref[...] = x1
    h2 = x1 * lax.rsqrt(jnp.mean(x1 * x1, axis=-1, keepdims=True) + EPS) * n2g_ref[...]
    h2 = h2 * (1.0 + sc2_ref[...]) + sh2_ref[...]
    h2_ref[...] = h2.astype(BF16)
    lg = lax.dot_general(wr_ref[...], h2, NT, precision=HI, preferred_element_type=F32)
    e = jnp.exp(lg - jnp.max(lg, axis=0, keepdims=True))
    aff_ref[...] = e / jnp.sum(e, axis=0, keepdims=True)


def mixout(att, o, o_off, z, u, x, mod, wts):
    B, T, D = x.shape
    tm = min(512, T)
    nt = T // tm
    tb = tm // SUBLANE
    nb = T // SUBLANE
    to = math.gcd(tm, o_off) if o_off else tm
    n_o = tm // to
    W = GDN_HEADS * GDN_DV
    cur = lambda n: pl.BlockSpec((None, tm, n), lambda b, i: (b, i, 0))
    odir = [pl.BlockSpec((None, to, W), lambda b, i, k=k: (b, i * n_o + o_off // to + k, 0)) for k in range(n_o)]
    modspec = lambda k: pl.BlockSpec((None, 1, D), lambda b, i: (b, 0, k))
    full = lambda a: pl.BlockSpec(a.shape, lambda b, i: (0,) * a.ndim)
    return pl.pallas_call(
        functools.partial(_mixout_kernel, tm=tm, T=T, nt=nt, n_o=n_o), grid=(B, nt),
        in_specs=[pl.BlockSpec((None, MLA_HEADS * MLA_V, tm), lambda b, i: (b, 0, i))] + odir + odir + [cur(W), cur(POOL_WIDTH),
                  pl.BlockSpec((None, SUBLANE, POOL_WIDTH), lambda b, i: (b, jnp.maximum(i * tb - 1, 0), 0)),
                  pl.BlockSpec((None, SUBLANE, POOL_WIDTH), lambda b, i: (b, jnp.minimum((i + 1) * tb, nb - 1), 0)),
                  cur(D), modspec(2), modspec(3), modspec(4)] + [full(a) for a in wts],
        out_specs=[cur(D), cur(D), pl.BlockSpec((None, N_EXPERTS, tm), lambda b, i: (b, 0, i))],
        out_shape=[jax.ShapeDtypeStruct((B, T, D), F32), jax.ShapeDtypeStruct((B, T, D), BF16),
                   jax.ShapeDtypeStruct((B, N_EXPERTS, T), F32)],
        compiler_params=_cp("parallel", "parallel"), name="mixout",
    )(att, *([o[0]] * n_o), *([o[1]] * n_o), z, u, u, u, x, mod, mod, mod, *wts)


MOE_SUB = 256
MOE_SLOTS = 128


def _route_kernel(aff_ref, gate_ref, slot_ref, starts_ref, *, cap, T):
    aff = aff_ref[...]

    def body(it, res):
        cand = res | jnp.left_shift(jnp.int32(1), 30 - it)
        cnt = jnp.sum((aff >= pltpu.bitcast(cand, F32)).astype(jnp.int32), axis=-1, keepdims=True)
        return jnp.where(cnt >= cap, cand, res)

    bits = lax.fori_loop(0, 31, body, jnp.zeros((N_EXPERTS, 1), jnp.int32))
    thr = pltpu.bitcast(bits, F32)
    above = pltpu.bitcast(bits + 1, F32)
    n_gt = jnp.sum((aff >= above).astype(jnp.int32), axis=-1, keepdims=True)
    need = (cap - n_gt).astype(F32)
    upper = (lax.broadcasted_iota(jnp.int32, (LANE, LANE), 0)
             < lax.broadcasted_iota(jnp.int32, (LANE, LANE), 1)).astype(BF16)
    seen = jnp.zeros((N_EXPERTS, 1), F32)
    taken = jnp.zeros((N_EXPERTS, 1), F32)
    lane = lax.broadcasted_iota(jnp.int32, (N_EXPERTS, LANE), 1)
    starts = jnp.zeros((N_EXPERTS, LANE), jnp.int32)
    per_sub = MOE_SUB // LANE
    for j in range(T // LANE):
        if j % per_sub == 0:
            starts = jnp.where(lane == j // per_sub, taken.astype(jnp.int32), starts)
        sl = slice(j * LANE, (j + 1) * LANE)
        aj = aff[:, sl]
        eq = jnp.where(aj >= thr, jnp.where(aj < above, 1.0, 0.0), 0.0)
        rank = jnp.dot(eq.astype(BF16), upper, preferred_element_type=F32) + seen
        sel = jnp.where(aj >= above, 1.0, jnp.where(rank < need, eq, 0.0))
        gate_ref[:, sl] = sel * aj
        slot = jnp.dot(sel.astype(BF16), upper, preferred_element_type=F32) + taken
        slot_ref[:, sl] = jnp.where(sel > 0.0, slot, -1.0)
        seen = seen + jnp.sum(eq, axis=-1, keepdims=True)
        taken = taken + jnp.sum(sel, axis=-1, keepdims=True)
    starts_ref[...] = jnp.where(lane == T // MOE_SUB, taken.astype(jnp.int32), starts)


def route(aff, cap):
    B, E, T = aff.shape
    assert T % MOE_SUB == 0 and T // MOE_SUB < LANE
    spec = pl.BlockSpec((None, E, T), lambda b: (b, 0, 0))
    return pl.pallas_call(
        functools.partial(_route_kernel, cap=cap, T=T), grid=(B,), in_specs=[spec],
        out_specs=[spec, spec, pl.BlockSpec((None, E, LANE), lambda b: (b, 0, 0))],
        out_shape=[jax.ShapeDtypeStruct(aff.shape, F32), jax.ShapeDtypeStruct(aff.shape, F32),
                   jax.ShapeDtypeStruct((B, E, LANE), jnp.int32)],
        compiler_params=_cp("parallel"), name="route",
    )(aff)


def _slot_blocks(starts_ref, b, e, sub, R):
    s0 = starts_ref[b, e, sub]
    s1 = starts_ref[b, e, sub + 1]
    return s0 // R, (s1 + R - 1) // R


def _moe_ffn_kernel(starts_ref, h_ref, slot_ref, gate_ref, wg_ref, wu_ref, wd_ref, y_ref, xs_ref, gs_ref, *, n_sub, R):
    e, b, j = pl.program_id(0), pl.program_id(1), pl.program_id(2)

    @pl.when(j == 0)
    def _():
        xs_ref[...] = jnp.zeros_like(xs_ref)
        gs_ref[...] = jnp.zeros_like(gs_ref)

    rows = lax.broadcasted_iota(jnp.int32, (R, 1), 0)
    for sub in range(n_sub):
        tsl = slice(sub * MOE_SUB, (sub + 1) * MOE_SUB)
        h = h_ref[tsl, :]
        srow = slot_ref[:, tsl]
        grow = gate_ref[:, tsl]

        def gather(i, carry):
            base = pl.multiple_of(i * R, R)
            match = srow == (base + rows).astype(F32)
            xs_ref[pl.ds(base, R), :] += jnp.dot(jnp.where(match, 1.0, 0.0).astype(BF16), h,
                                                 preferred_element_type=F32)
            gs_ref[pl.ds(base, R), :] += jnp.sum(jnp.where(match, grow, 0.0), axis=1, keepdims=True)
            return carry

        lax.fori_loop(*_slot_blocks(starts_ref, b, e, j * n_sub + sub, R), gather, 0)

    @pl.when(j == pl.num_programs(2) - 1)
    def _():
        xs = xs_ref[...].astype(BF16)
        a = jnp.dot(xs, wg_ref[...].astype(BF16), preferred_element_type=F32)
        hid = (_silu(a) * jnp.dot(xs, wu_ref[...].astype(BF16), preferred_element_type=F32)).astype(BF16)
        y_ref[...] = (jnp.dot(hid, wd_ref[...].astype(BF16), preferred_element_type=F32) * gs_ref[...]).astype(BF16)


def _moe_combine_kernel(starts_ref, y_ref, slot_ref, x1_ref, g2_ref, o_ref, acc_ref, *, n_sub, R, NE, n_blk):
    b, j, eg = pl.program_id(0), pl.program_id(1), pl.program_id(2)

    @pl.when(eg == 0)
    def _():
        acc_ref[...] = jnp.zeros_like(acc_ref)

    lane = lax.broadcasted_iota(jnp.int32, slot_ref.shape, 1)
    slots = slot_ref[...]
    scol = [jnp.sum(jnp.where(lane == eg * NE + k, slots, 0.0), axis=-1, keepdims=True) for k in range(NE)]
    wide = lax.broadcasted_iota(jnp.int32, (1, NE * R), 1)
    which = [wide // R == k for k in range(NE)]
    within = wide % R
    for sub in range(n_sub):
        tsl = slice(sub * MOE_SUB, (sub + 1) * MOE_SUB)
        tok_slot = jnp.zeros((MOE_SUB, NE * R), F32)
        for k in range(NE):
            tok_slot = jnp.where(which[k], scol[k][tsl], tok_slot)
        lo, hi = zip(*[_slot_blocks(starts_ref, b, eg * NE + k, j * n_sub + sub, R) for k in range(NE)])
        trips = functools.reduce(jnp.maximum, [hi[k] - lo[k] for k in range(NE)])

        def scatter(it, carry):
            target = jnp.full((1, NE * R), -2, jnp.int32)
            rows = []
            for k in range(NE):
                blk = jnp.minimum(lo[k] + it, n_blk - 1)
                target = jnp.where(which[k], jnp.where(lo[k] + it < hi[k], blk * R + within, -2), target)
                rows.append(y_ref[k, pl.ds(pl.multiple_of(blk * R, R), R), :])
            onehot = jnp.where(tok_slot == target.astype(F32), 1.0, 0.0).astype(BF16)
            acc_ref[tsl, :] += jnp.dot(onehot, jnp.concatenate(rows, axis=0), preferred_element_type=F32)
            return carry

        lax.fori_loop(0, trips, scatter, 0)

    @pl.when(eg == pl.num_programs(2) - 1)
    def _():
        o_ref[...] = x1_ref[...] + g2_ref[...] * acc_ref[...]


def moe(h2, routed, x1, mod, wg, wu, wd, layer, cap):
    gate, slot, starts = routed
    B, T, D = x1.shape
    E, F = N_EXPERTS, wg.shape[-1]
    R = min(MOE_SLOTS, cap)
    assert cap % R == 0
    tt = min(2048, T)
    n_sub = tt // MOE_SUB
    row = pl.BlockSpec((None, None, 1, tt), lambda e, b, j, st: (b, e, 0, j))
    wspec = lambda r, c: pl.BlockSpec((None, None, r, c), lambda e, b, j, st: (layer, e, 0, 0))
    y = pl.pallas_call(
        functools.partial(_moe_ffn_kernel, n_sub=n_sub, R=R),
        grid_spec=pltpu.PrefetchScalarGridSpec(
            num_scalar_prefetch=1, grid=(E, B, T // tt),
            in_specs=[pl.BlockSpec((None, tt, D), lambda e, b, j, st: (b, j, 0)), row, row,
                      wspec(D, F), wspec(D, F), wspec(F, D)],
            out_specs=pl.BlockSpec((None, None, cap, D), lambda e, b, j, st: (b, e, 0, 0)),
            scratch_shapes=[pltpu.VMEM((cap, D), F32), pltpu.VMEM((cap, 1), F32)]),
        out_shape=jax.ShapeDtypeStruct((B, E, cap, D), BF16),
        compiler_params=_cp("parallel", "parallel", "arbitrary"), name="moe_ffn",
    )(starts, h2, slot.reshape(B, E, 1, T), gate.reshape(B, E, 1, T), wg, wu, wd)
    tc = min(1024, T)
    ne = 4
    rc = min(2 * LANE // ne, cap)
    tok = lambda n: pl.BlockSpec((None, tc, n), lambda b, j, e, st: (b, j, 0))
    return pl.pallas_call(
        functools.partial(_moe_combine_kernel, n_sub=tc // MOE_SUB, R=rc, NE=ne, n_blk=cap // rc),
        grid_spec=pltpu.PrefetchScalarGridSpec(
            num_scalar_prefetch=1, grid=(B, T // tc, E // ne),
            in_specs=[pl.BlockSpec((None, ne, cap, D), lambda b, j, e, st: (b, e, 0, 0)),
                      tok(E), tok(D), pl.BlockSpec((None, 1, D), lambda b, j, e, st: (b, 0, 5))],
            out_specs=tok(D),
            scratch_shapes=[pltpu.VMEM((tc, D), F32)]),
        out_shape=jax.ShapeDtypeStruct((B, T, D), F32),
        compiler_params=_cp("parallel", "parallel", "arbitrary"), name="moe_combine",
    )(starts, y, jnp.swapaxes(slot, 1, 2), x1, mod)


def _in_cols():
    src = np.full((IN_PAD,), -1, np.int64)
    splits = (MLA_Q_LORA, MLA_KV_LORA, MLA_ROPE, 512, 512, 512, 512, 2 * GDN_HEADS, 2 * GDN_HEADS, POOL_WIDTH)
    o = np.concatenate([[0], np.cumsum(splits)])
    put = lambda name, at, lo, n: src.__setitem__(slice(SEG[name][0] + at, SEG[name][0] + at + n), np.arange(lo, lo + n))
    put("pq", 0, o[0], MLA_Q_LORA)
    put("pkv", 0, o[1], MLA_KV_LORA)
    put("pkr", MLA_NOPE, o[2], MLA_ROPE)
    for name, k in (("gq", 3), ("gk", 4), ("gv", 5), ("gz", 6)):
        put(name, 0, o[k], 512)
    for d in range(2):
        put("gab", d * LANE, o[7] + d * GDN_HEADS, GDN_HEADS)
        put("gab", d * LANE + GDN_HEADS, o[8] + d * GDN_HEADS, GDN_HEADS)
    put("pool", 0, o[9], POOL_WIDTH)
    return src


def _take_cols(w, src, axis):
    pieces, p, n = [], 0, len(src)
    while p < n:
        q = p + 1
        while q < n and ((src[q] < 0 and src[p] < 0) or (src[p] >= 0 and src[q] == src[q - 1] + 1)):
            q += 1
        if src[p] < 0:
            shape = list(w.shape)
            shape[axis] = q - p
            pieces.append(jnp.zeros(shape, w.dtype))
        else:
            pieces.append(lax.slice_in_dim(w, int(src[p]), int(src[p]) + q - p, axis=axis))
        p = q
    return jnp.concatenate(pieces, axis=axis)


def _head_pad_src(per_head, lo, n):
    src = np.full((MLA_HEADS * HEAD_PAD,), -1, np.int64)
    for h in range(MLA_HEADS):
        src[h * HEAD_PAD:h * HEAD_PAD + n] = h * per_head + lo + np.arange(n)
    return src


def _rope_tables(T, rotate):
    cos = np.ones((T, LANE), np.float32)
    sa = np.zeros((T, LANE), np.float32)
    sb = np.zeros((T, LANE), np.float32)
    if rotate:
        n_freq = MLA_ROPE // 4
        inv = ROPE_THETA ** (-np.arange(n_freq, dtype=np.float64) / n_freq)
        pos_r = np.repeat(np.arange(T // GRID_W, dtype=np.float64), GRID_W)
        pos_c = np.tile(np.arange(GRID_W, dtype=np.float64), T // GRID_W)
        for base, pos in ((MLA_NOPE, pos_r), (MLA_NOPE + 2 * n_freq, pos_c)):
            ang = pos[:, None] * inv[None, :]
            c, s = np.cos(ang), np.sin(ang)
            cos[:, base:base + n_freq] = c
            cos[:, base + n_freq:base + 2 * n_freq] = c
            sa[:, base:base + n_freq] = -s
            sb[:, base + n_freq:base + 2 * n_freq] = s
    return jnp.asarray(cos), jnp.asarray(sa), jnp.asarray(sb)


def _lane_vec(vals_by_dir, at):
    v = jnp.zeros((2, LANE), F32).at[:, at:at + GDN_HEADS].set(vals_by_dir)
    return v.reshape(1, 2 * LANE)


def kernel(x, c, ctx, c_ctx, ada_w, ada_b, norm1_g, norm2_g, w_in, mla_q_a_norm, mla_w_uq, mla_kv_a_norm, mla_w_ukv, mla_q_norm, mla_k_norm, gdn_conv_w, gdn_a_log, gdn_dt_bias, gdn_norm_g, pool_w, pool_scale, w_out, moe_router, moe_w_gate, moe_w_up, moe_w_down):
    B, T, D = x.shape
    Tc = ctx.shape[1]
    L = ada_w.shape[0]
    cvec = jnp.concatenate([c, c_ctx[None, :], jnp.zeros((SUBLANE - B - 1, D), F32)], axis=0)
    mod = ada_mod(cvec, ada_w, ada_b)
    rope_lat = _rope_tables(T, True)
    rope_ctx = _rope_tables(Tc, False)
    in_src = _in_cols()
    uq_src = _head_pad_src(MLA_QK, 0, MLA_QK)
    uk_src = _head_pad_src(MLA_NOPE + MLA_V, 0, MLA_NOPE)
    uv_src = _head_pad_src(MLA_NOPE + MLA_V, MLA_NOPE, MLA_V)
    pad_to = lambda v, n: jnp.pad(v, (0, n - v.shape[0])).reshape(1, n)

    xc = ctx
    for l in range(L):
        need_ctx = l < L - 1
        mod_lat = mod[l, :B].reshape(B, 1, ADA_CHUNKS * D)
        mod_ctx = jnp.broadcast_to(mod[l, B].reshape(1, 1, ADA_CHUNKS * D), (B, 1, ADA_CHUNKS * D))
        w_in_p = _take_cols(w_in[l], in_src, 1).astype(BF16)
        prep_w = (
            pad_to(mla_q_a_norm[l], 256),
            jnp.pad(_take_cols(mla_w_uq[l], uq_src, 1), ((0, 256 - MLA_Q_LORA), (0, 0))).astype(BF16),
            mla_kv_a_norm[l].reshape(1, MLA_KV_LORA),
            _take_cols(mla_w_ukv[l], uk_src, 1).astype(BF16),
            _take_cols(mla_w_ukv[l], uv_src, 1).T.astype(BF16),
            pad_to(mla_q_norm[l] * (MLA_QK ** -0.5 * math.log2(math.e)), HEAD_PAD),
            pad_to(mla_k_norm[l], HEAD_PAD),
            gdn_conv_w[l],
            _lane_vec(gdn_a_log[l], 0),
            _lane_vec(gdn_dt_bias[l], 0),
        )
        wo = w_out[l]
        n_att = MLA_HEADS * MLA_V
        n_gdn = GDN_HEADS * GDN_DV
        wbd = jnp.zeros((POOL_WIDTH, POOL_WIDTH), F32)
        for gi in range(len(POOL_WINDOWS)):
            wbd = wbd.at[gi * POOL_GROUP:(gi + 1) * POOL_GROUP, gi * POOL_GROUP:(gi + 1) * POOL_GROUP].set(pool_w[l, gi])
        mix_w = (
            gdn_norm_g[l].reshape(1, GDN_DV),
            wo[:n_att].astype(BF16),
            wo[n_att:n_att + n_gdn].astype(BF16),
            wo[n_att + n_gdn:].astype(BF16),
            wbd.astype(BF16),
            pool_scale[l].reshape(1, POOL_WIDTH),
            norm2_g[l].reshape(1, D),
            moe_router[l].T,
        )

        p_lat = dict(zip(SEG, inproj(x, mod_lat, norm1_g[l], w_in_p)))
        p_ctx = dict(zip(SEG, inproj(xc, mod_ctx, norm1_g[l], w_in_p)))
        a_ctx = prep(p_ctx, prep_w, rope_ctx, T, T + Tc)
        a_lat = prep(p_lat, prep_w, rope_lat, 0, T + Tc, shared=a_ctx)
        att_l = attention(a_lat["Q"], a_lat["K"], a_lat["VT"], 0, T + Tc)
        o_all = gdn_scan(a_lat["q"], a_lat["k"], a_lat["v"], a_lat["gb"], Tc)

        def channel_mix(att, o_off, p, xin, m):
            Tn = xin.shape[1]
            x1, h2, aff = mixout(att, o_all, o_off, p["gz"], p["pool"], xin, m, mix_w)
            cap = EC_CAPACITY_FACTOR * Tn // N_EXPERTS
            return moe(h2, route(aff, cap), x1, m, moe_w_gate, moe_w_up, moe_w_down, l, cap)

        x = channel_mix(att_l, 0, p_lat, x, mod_lat)
        if need_ctx:
            att_c = attention(a_ctx["Q"], a_lat["K"], a_lat["VT"], T, Tc)
            xc = channel_mix(att_c, T, p_ctx, xc, mod_ctx)
    return x
```

```python
import functools
import math

import numpy as np
import jax
import jax.numpy as jnp
from jax import lax
from jax.experimental import pallas as pl
from jax.experimental.pallas import tpu as pltpu

F32 = jnp.float32
BF16 = jnp.bfloat16
HI = lax.Precision.HIGHEST

EPS = 1e-6
GRID_W = 64
ADA_CHUNKS = 6
MLA_HEADS = 4
MLA_NOPE = 64
MLA_ROPE = 32
MLA_QK = MLA_NOPE + MLA_ROPE
MLA_V = 64
MLA_Q_LORA = 192
MLA_KV_LORA = 128
ROPE_THETA = 10000.0
GDN_HEADS = 4
GDN_DK = 128
GDN_DV = 128
GDN_CHUNK = 64
POOL_WINDOWS = (2, 4, 8, 16)
POOL_GROUP = 64
POOL_WIDTH = POOL_GROUP * len(POOL_WINDOWS)
N_EXPERTS = 16
EC_CAPACITY_FACTOR = 2

LANE = 128
SUBLANE = 8
HEAD_PAD = 128
VMEM_LIMIT = 48 * 1024 * 1024

NT = (((1,), (1,)), ((), ()))
TN = (((0,), (0,)), ((), ()))

SEG = {}
_off = 0
for _name, _w in (("pq", 256), ("pkv", 128), ("pkr", 128), ("gq", 512), ("gk", 512), ("gv", 512),
                  ("gz", 512), ("gab", 256), ("pool", 256)):
    SEG[_name] = (_off, _w)
    _off += _w
IN_PAD = _off


def _cp(*dims):
    return pltpu.CompilerParams(dimension_semantics=dims, vmem_limit_bytes=VMEM_LIMIT)


def _silu(v):
    return v / (1.0 + jnp.exp(-v))


def _ada_kernel(c_ref, w_ref, b_ref, o_ref):
    s = _silu(c_ref[...])
    o_ref[...] = jnp.dot(s, w_ref[...], precision=HI, preferred_element_type=F32) + b_ref[...]


def ada_mod(cvec, ada_w, ada_b):
    L, D, N = ada_w.shape
    tn = N // 4
    return pl.pallas_call(
        _ada_kernel, grid=(L, N // tn),
        in_specs=[pl.BlockSpec((SUBLANE, D), lambda l, j: (0, 0)),
                  pl.BlockSpec((None, D, tn), lambda l, j: (l, 0, j)),
                  pl.BlockSpec((None, 1, tn), lambda l, j: (l, 0, j))],
        out_specs=pl.BlockSpec((None, SUBLANE, tn), lambda l, j: (l, 0, j)),
        out_shape=jax.ShapeDtypeStruct((L, SUBLANE, N), F32),
        compiler_params=_cp("parallel", "parallel"), name="ada_mod",
    )(cvec, ada_w, ada_b.reshape(L, 1, N))


def _inproj_kernel(x_ref, sh_ref, sc_ref, g_ref, w_ref, *out_refs):
    x = x_ref[...]
    h = x * lax.rsqrt(jnp.mean(x * x, axis=-1, keepdims=True) + EPS) * g_ref[...]
    hb = (h * (1.0 + sc_ref[...]) + sh_ref[...]).astype(BF16)
    for (off, n), o_ref in zip(SEG.values(), out_refs):
        o_ref[...] = jnp.dot(hb, w_ref[:, off:off + n], preferred_element_type=F32)


def inproj(x, mod, norm_g, w_in_p):
    B, T, D = x.shape
    tm = min(512, T)
    modspec = lambda k: pl.BlockSpec((None, 1, D), lambda b, i, k=k: (b, 0, k))
    return pl.pallas_call(
        _inproj_kernel, grid=(B, T // tm),
        in_specs=[pl.BlockSpec((None, tm, D), lambda b, i: (b, i, 0)), modspec(0), modspec(1),
                  pl.BlockSpec((1, D), lambda b, i: (0, 0)),
                  pl.BlockSpec((D, IN_PAD), lambda b, i: (0, 0))],
        out_specs=[pl.BlockSpec((None, tm, n), lambda b, i: (b, i, 0)) for _, n in SEG.values()],
        out_shape=[jax.ShapeDtypeStruct((B, T, n), F32) for _, n in SEG.values()],
        compiler_params=_cp("parallel", "parallel"), name="inproj",
    )(x, mod, mod, norm_g.reshape(1, D), w_in_p)


def _prep_kernel(*refs, nt, tm):
    (pq_ref, pkv_ref, pkr_ref, gq_ref, gk_ref, gv_ref, gqp_ref, gkp_ref, gvp_ref,
     gqn_ref, gkn_ref, gvn_ref, gab_ref, qan_ref, wuq_ref, kvan_ref, wuk_ref, wuv_ref,
     qn_ref, kn_ref, cos_ref, sa_ref, sb_ref, cw_ref, alog_ref, dt_ref) = refs[:26]
    Q_ref, K_ref, VT_ref, q_ref, k_ref, v_ref, gb_ref = refs[-7:]
    i = pl.program_id(1)
    cos, sa, sb = cos_ref[...], sa_ref[...], sb_ref[...]

    def rope(xh):
        return xh * cos + pltpu.roll(xh, LANE - 8, 1) * sa + pltpu.roll(xh, 8, 1) * sb

    pq = pq_ref[...]
    qa = pq * lax.rsqrt(jnp.sum(pq * pq, axis=-1, keepdims=True) * (1.0 / MLA_Q_LORA) + EPS) * qan_ref[...]
    qall = jnp.dot(qa.astype(BF16), wuq_ref[...], preferred_element_type=F32)
    pkv = pkv_ref[...]
    kva = (pkv * lax.rsqrt(jnp.mean(pkv * pkv, axis=-1, keepdims=True) + EPS) * kvan_ref[...]).astype(BF16)
    kall = jnp.dot(kva, wuk_ref[...], preferred_element_type=F32)
    vt = lax.dot_general(wuv_ref[...], kva, NT, preferred_element_type=F32)
    ones_row = lax.broadcasted_iota(jnp.int32, vt.shape, 0) % HEAD_PAD == MLA_V
    VT_ref[...] = jnp.where(ones_row, 1.0, vt).astype(BF16)
    pkr = pkr_ref[...]
    for h in range(MLA_HEADS):
        sl = slice(h * HEAD_PAD, (h + 1) * HEAD_PAD)
        qh = qall[:, sl]
        qh = qh * lax.rsqrt(jnp.sum(qh * qh, axis=-1, keepdims=True) * (1.0 / MLA_QK) + EPS) * qn_ref[...]
        Q_ref[:, sl] = rope(qh).astype(BF16)
        kh = kall[:, sl] + pkr
        kh = kh * lax.rsqrt(jnp.sum(kh * kh, axis=-1, keepdims=True) * (1.0 / MLA_QK) + EPS) * kn_ref[...]
        K_ref[:, sl] = rope(kh).astype(BF16)

    rid = lax.broadcasted_iota(jnp.int32, (SUBLANE, GDN_HEADS * GDN_DK), 0)

    def conv_silu(u_ref, up_ref, un_ref, c0):
        u = u_ref[...]
        n = u.shape[1]
        prev_row = jnp.where(i > 0, up_ref[SUBLANE - 1:SUBLANE, :], 0.0)
        next_row = jnp.where(i < nt - 1, un_ref[0:1, :], 0.0)
        um = pltpu.roll(u, 1, 0)
        um = jnp.concatenate([jnp.where(rid == 0, prev_row, um[:SUBLANE]), um[SUBLANE:]], axis=0)
        up = pltpu.roll(u, tm - 1, 0)
        up = jnp.concatenate([up[:tm - SUBLANE], jnp.where(rid == SUBLANE - 1, next_row, up[tm - SUBLANE:])], axis=0)
        y = um * cw_ref[0:1, c0:c0 + n] + u * cw_ref[1:2, c0:c0 + n] + up * cw_ref[2:3, c0:c0 + n]
        return _silu(y)

    cq = conv_silu(gq_ref, gqp_ref, gqn_ref, 0)
    ck = conv_silu(gk_ref, gkp_ref, gkn_ref, GDN_HEADS * GDN_DK)
    v_ref[...] = conv_silu(gv_ref, gvp_ref, gvn_ref, 2 * GDN_HEADS * GDN_DK)
    for h in range(GDN_HEADS):
        sl = slice(h * GDN_DK, (h + 1) * GDN_DK)
        qh = cq[:, sl]
        q_ref[:, sl] = qh * lax.rsqrt(jnp.sum(qh * qh, axis=-1, keepdims=True) + EPS) * (GDN_DK ** -0.5)
        kh = ck[:, sl]
        k_ref[:, sl] = kh * lax.rsqrt(jnp.sum(kh * kh, axis=-1, keepdims=True) + EPS)

    pre = gab_ref[...]
    lane = lax.broadcasted_iota(jnp.int32, pre.shape, 1) % LANE
    sp_in = pre + dt_ref[...]
    softplus = jnp.maximum(sp_in, 0.0) + jnp.log(1.0 + jnp.exp(-jnp.abs(sp_in)))
    g = -jnp.exp(alog_ref[...]) * softplus
    beta = 1.0 / (1.0 + jnp.exp(-pre))
    gb_ref[...] = jnp.where(lane < GDN_HEADS, g, jnp.where(lane < 2 * GDN_HEADS, beta, 0.0))


def prep(p, wts, rope_tabs, row_off, t_all, shared=None):
    pq, pkv, pkr, gq, gk, gv, gab = (p[k] for k in ("pq", "pkv", "pkr", "gq", "gk", "gv", "gab"))
    B, T, _ = pq.shape
    tm = min(256, T)
    assert row_off % tm == 0
    nt = T // tm
    tb = tm // SUBLANE
    nb = T // SUBLANE
    ro = row_off // tm
    cur = lambda n: pl.BlockSpec((None, tm, n), lambda b, i: (b, i, 0))
    dst = lambda n: pl.BlockSpec((None, tm, n), lambda b, i: (b, i + ro, 0))
    prv = lambda n: pl.BlockSpec((None, SUBLANE, n), lambda b, i: (b, jnp.maximum(i * tb - 1, 0), 0))
    nxt = lambda n: pl.BlockSpec((None, SUBLANE, n), lambda b, i: (b, jnp.minimum((i + 1) * tb, nb - 1), 0))
    full = lambda a: pl.BlockSpec(a.shape, lambda b, i: (0,) * a.ndim)
    tab = pl.BlockSpec((tm, LANE), lambda b, i: (i, 0))
    W = GDN_HEADS * GDN_DK
    outs = [("Q", MLA_HEADS * HEAD_PAD, BF16), ("K", MLA_HEADS * HEAD_PAD, BF16), ("VT", None, BF16),
            ("q", W, F32), ("k", W, F32), ("v", W, F32), ("gb", 2 * LANE, F32)]
    HP = MLA_HEADS * HEAD_PAD
    shared = [] if shared is None else [shared[n] for n, _, _ in outs[1:]]
    n_in = 26
    res = pl.pallas_call(
        functools.partial(_prep_kernel, nt=nt, tm=tm), grid=(B, nt),
        in_specs=[cur(256), cur(128), cur(128), cur(W), cur(W), cur(W), prv(W), prv(W), prv(W),
                  nxt(W), nxt(W), nxt(W), cur(256)] + [full(a) for a in wts[:7]] + [tab, tab, tab]
                 + [full(a) for a in wts[7:]] + [pl.BlockSpec(memory_space=pl.ANY)] * len(shared),
        out_specs=[cur(outs[0][1])] + [dst(n) if n else pl.BlockSpec((None, HP, tm), lambda b, i: (b, 0, i + ro))
                                       for _, n, _ in outs[1:]],
        out_shape=[jax.ShapeDtypeStruct((B, T, outs[0][1]), outs[0][2])]
                  + [jax.ShapeDtypeStruct((B, t_all, n) if n else (B, HP, t_all), dt) for _, n, dt in outs[1:]],
        input_output_aliases={n_in + k: 1 + k for k in range(len(shared))},
        compiler_params=_cp("parallel", "parallel"), name="prep",
    )(pq, pkv, pkr, gq, gk, gv, gq, gk, gv, gq, gk, gv, gab, *wts[:7], *rope_tabs, *wts[7:], *shared)
    return dict(zip([n for n, _, _ in outs], res))


def _attn_kernel(q_ref, k_ref, vt_ref, o_ref, sa_ref, sb_ref, *, ck, nk):
    q = q_ref[...]
    tq = q.shape[0]

    def scores(j):
        return lax.dot_general(k_ref[j * ck:(j + 1) * ck, :], q, NT, preferred_element_type=F32)

    nv = MLA_V + 16

    def update(carry, s_ref, j):
        m, acc = carry
        s = s_ref[...]
        m_new = jnp.maximum(m, jnp.max(s, axis=0, keepdims=True))
        p = jnp.exp2(s - m_new).astype(BF16)
        acc = jnp.exp2(m - m_new) * acc + jnp.dot(vt_ref[0:nv, j * ck:(j + 1) * ck], p, preferred_element_type=F32)
        return m_new, acc

    bufs = (sa_ref, sb_ref)
    bufs[0][...] = scores(0)
    carry = (jnp.full((1, tq), -1e30, F32), jnp.zeros((nv, tq), F32))
    for j in range(nk):
        if j + 1 < nk:
            bufs[(j + 1) % 2][...] = scores(j + 1)
        carry = update(carry, bufs[j % 2], j)
    acc = carry[1]
    o = acc[:MLA_V] / acc[MLA_V:MLA_V + 1]
    o_ref[...] = o.astype(o_ref.dtype)


def attention(Q, K, VT, k_off, Tk):
    B, Tq, _ = Q.shape
    assert k_off % Tk == 0
    kb = k_off // Tk
    tq = min(1024, Tq)
    ck = next(c for c in (384, 256, 128) if Tk % c == 0)
    qo = pl.BlockSpec((None, tq, HEAD_PAD), lambda b, h, i: (b, i, h))
    return pl.pallas_call(
        functools.partial(_attn_kernel, ck=ck, nk=Tk // ck), grid=(B, MLA_HEADS, Tq // tq),
        in_specs=[qo, pl.BlockSpec((None, Tk, HEAD_PAD), lambda b, h, i: (b, kb, h)),
                  pl.BlockSpec((None, HEAD_PAD, Tk), lambda b, h, i: (b, h, kb))],
        out_specs=pl.BlockSpec((None, MLA_V, tq), lambda b, h, i: (b, h, i)),
        out_shape=jax.ShapeDtypeStruct((B, MLA_HEADS * MLA_V, Tq), BF16),
        scratch_shapes=[pltpu.VMEM((ck, tq), F32), pltpu.VMEM((ck, tq), F32)],
        compiler_params=_cp("parallel", "parallel", "parallel"), name="attention",
    )(Q, K, VT)


GDN_GROUP = 4


def _gdn_prep_kernel(q_ref, k_ref, v_ref, gb_ref, wq_ref, u_ref, qk_ref, kdt_ref, egl_ref, *, C, G):
    H, DK = GDN_HEADS, GDN_DK
    fwd = pl.program_id(1) == 0
    dot = functools.partial(jnp.dot, preferred_element_type=F32)
    row = lax.broadcasted_iota(jnp.int32, (C, H * C), 0)
    lane = lax.broadcasted_iota(jnp.int32, (C, H * C), 1)
    col = lane & (C - 1)
    hmask = [(lane >> int(math.log2(C))) == h for h in range(H)]
    wide = lax.broadcasted_iota(jnp.int32, (C, H * DK), 1)
    kmask = [(wide >> int(math.log2(DK))) == h for h in range(H)]
    ahead = jnp.where(fwd, row - col, col - row)
    incl = ahead >= 0
    strict = ahead > 0
    eye = (row == col).astype(F32)
    r1 = lax.broadcasted_iota(jnp.int32, (C, C), 0)
    c1 = lax.broadcasted_iota(jnp.int32, (C, C), 1)
    incl16 = (jnp.where(fwd, r1 - c1, c1 - r1) >= 0).astype(F32).astype(BF16)
    eye16 = (r1 == c1).astype(F32).astype(BF16)

    def blockdiag(m, masks):
        return jnp.concatenate([jnp.where(mk, m, 0.0) for mk in masks], axis=0).astype(BF16)

    def per_head(cols, width):
        n = cols.shape[0]
        if width == LANE:
            return jnp.concatenate([jnp.broadcast_to(cols[:, h:h + 1], (n, LANE)) for h in range(H)], axis=1)
        low = lax.broadcasted_iota(jnp.int32, (n, LANE), 1) < width
        return jnp.concatenate([jnp.where(low, cols[:, h:h + 1], cols[:, h + 1:h + 2]) for h in range(0, H, 2)], axis=1)

    def terms(v):
        hi = v.astype(BF16)
        rest = v - hi.astype(F32)
        mid = rest.astype(BF16)
        return jnp.concatenate([hi, mid, (rest - mid.astype(F32)).astype(BF16)], axis=1)

    fold = lambda a, axis: sum(jnp.split(a, 3, axis=axis)[1:], jnp.split(a, 3, axis=axis)[0])
    chunks = range(G)
    rows = [slice(g * C, (g + 1) * C) for g in chunks]
    gb = [gb_ref[r, :] for r in rows]
    gterms = [terms(v) for v in gb]
    gc = [fold(dot(incl16, t), 1) for t in gterms]
    gct = [fold(lax.dot_general(t, incl16, (((0,), (1,)), ((), ())), preferred_element_type=F32), 0)
           for t in gterms]
    glast = [jnp.where(fwd, v[C - 1:C, :], v[0:1, :]) for v in gc]
    k16 = [k_ref[r, :].astype(BF16) for r in rows]
    qkk = [lax.dot_general(jnp.concatenate([q_ref[rows[g], :].astype(BF16), k16[g]], axis=0),
                           blockdiag(k_ref[rows[g], :], kmask), NT, preferred_element_type=F32)
           for g in chunks]
    a, tinv = [], []
    same = lambda s: (row >> s) == (col >> s)
    pairs = jnp.where(same(1), 1.0, 0.0)
    for g in chunks:
        egl_ref[g] = jnp.broadcast_to(jnp.exp(glast[g]), (SUBLANE, LANE))
        grow = jnp.concatenate([gct[g][h:h + 1, :] for h in range(H)], axis=1)
        decay = jnp.exp(jnp.where(incl, per_head(gc[g], C) - grow, -1e30))
        qk_ref[g] = (qkk[g][:C] * decay).astype(BF16)
        a.append(jnp.where(strict, qkk[g][C:] * decay, 0.0) * per_head(gb[g][:, H:], C))
        tinv.append(eye - a[g] * pairs)
    for s in range(1, int(math.log2(C))):
        join = jnp.where(same(s + 1), jnp.where(same(s), 0.0, 1.0), 0.0)
        x16 = [tinv[g].astype(BF16) for g in chunks]
        xl = [dot(x16[g], blockdiag(a[g] * join, hmask)) for g in chunks]
        xlx = [dot(xl[g].astype(BF16), blockdiag(tinv[g], hmask)) for g in chunks]
        tinv = [tinv[g] - xlx[g] for g in chunks]
    gcw = [per_head(gc[g], DK) for g in chunks]
    bw = [per_head(gb[g][:, H:], DK) for g in chunks]
    wu = [dot(tinv[g].astype(BF16),
              jnp.concatenate([blockdiag(k_ref[rows[g], :] * (bw[g] * jnp.exp(gcw[g])), kmask),
                               blockdiag(v_ref[rows[g], :] * bw[g], kmask)], axis=1)) for g in chunks]
    kdt = [lax.dot_general((k_ref[rows[g], :] * jnp.exp(per_head(glast[g], DK) - gcw[g])).astype(BF16), eye16,
                           TN, preferred_element_type=F32) for g in chunks]
    for g in chunks:
        wq_ref[g, 0:C, :] = wu[g][:, :H * DK].astype(BF16)
        wq_ref[g, C:2 * C, :] = (q_ref[rows[g], :] * jnp.exp(gcw[g])).astype(BF16)
        u_ref[g] = wu[g][:, H * DK:]
        for h in range(H):
            kdt_ref[g, :, h * C:(h + 1) * C] = kdt[g][h * DK:(h + 1) * DK].astype(BF16)


def _gdn_rec_kernel(*refs, C, G, B):
    ins = (refs[0:5], refs[5:10])
    outs = refs[10:12]
    s_ref = refs[12]
    dot = functools.partial(jnp.dot, preferred_element_type=F32)

    @pl.when(pl.program_id(0) == 0)
    def _():
        s_ref[...] = jnp.zeros_like(s_ref)

    hsl = lambda h: slice(h * GDN_DK, (h + 1) * GDN_DK)
    csl = lambda h: slice(h * C, (h + 1) * C)
    for step in range(G):
        chains = [(d, b, h, step if d == 0 else G - 1 - step)
                  for d in range(2) for b in range(B) for h in range(GDN_HEADS)]
        S = {c: s_ref[c[0], c[1], c[2]] for c in chains}
        r = {(d, b, h, g): dot(ins[d][0][b, g, :, hsl(h)], S[(d, b, h, g)].astype(BF16))
             for (d, b, h, g) in chains}
        vn = {(d, b, h, g): (ins[d][1][b, g, :, hsl(h)] - r[(d, b, h, g)][:C]).astype(BF16) for (d, b, h, g) in chains}
        o = {(d, b, h, g): dot(ins[d][2][b, g, :, csl(h)], vn[(d, b, h, g)]) for (d, b, h, g) in chains}
        upd = {(d, b, h, g): dot(ins[d][3][b, g, :, csl(h)], vn[(d, b, h, g)]) for (d, b, h, g) in chains}
        for c in chains:
            d, b, h, g = c
            outs[d][b, g, :, hsl(h)] = r[c][C:] + o[c]
            s_ref[d, b, h] = S[c] * ins[d][4][b, g, 0:1, h:h + 1] + upd[c]


def gdn_scan(q, k, v, gb, n_ctx):
    B, Tt, W = q.shape
    C, G = GDN_CHUNK, GDN_GROUP
    n = Tt // C
    ng = n // G
    ncg = n_ctx // (C * G)
    assert n % G == 0 and n_ctx % (C * G) == 0
    gp = next(c for c in (12, 8, 6, 4, 3, 2, 1) if n % c == 0)
    tok = pl.BlockSpec((None, gp * C, W), lambda b, d, s: (b, s, 0))
    shapes = [((2 * C, W), BF16), ((C, W), F32), ((C, GDN_HEADS * C), BF16), ((GDN_DK, GDN_HEADS * C), BF16),
              ((SUBLANE, LANE), F32)]
    nat = lambda b, d, s: (b, d, s, 0, 0)
    mid = pl.pallas_call(
        functools.partial(_gdn_prep_kernel, C=C, G=gp), grid=(B, 2, n // gp),
        in_specs=[tok, tok, tok, pl.BlockSpec((None, gp * C, LANE), lambda b, d, s: (b, s, d))],
        out_specs=[pl.BlockSpec((None, None, gp, r, w), nat) for (r, w), _ in shapes],
        out_shape=[jax.ShapeDtypeStruct((B, 2, n, r, w), dt) for (r, w), dt in shapes],
        compiler_params=_cp("parallel", "parallel", "parallel"), name="gdn_prep",
    )(q, k, v, gb)

    fwd = lambda s: jnp.where(s < ncg, ng - ncg + s, s - ncg)
    bwd = lambda s: ng - 1 - s
    both = lambda r, w, d: pl.BlockSpec((B, None, G, r, w), (lambda s: (0, 0, fwd(s), 0, 0)) if d == 0
                                        else (lambda s: (0, 1, bwd(s), 0, 0)))
    o_spec = lambda d: pl.BlockSpec((B, G, C, W), (lambda s: (0, fwd(s), 0, 0)) if d == 0
                                    else (lambda s: (0, bwd(s), 0, 0)))
    o_f, o_b = pl.pallas_call(
        functools.partial(_gdn_rec_kernel, C=C, G=G, B=B), grid=(ng,),
        in_specs=[both(r, w, d) for d in range(2) for (r, w), _ in shapes],
        out_specs=[o_spec(0), o_spec(1)],
        out_shape=[jax.ShapeDtypeStruct((B, n, C, W), F32)] * 2,
        scratch_shapes=[pltpu.VMEM((2, B, GDN_HEADS, GDN_DK, GDN_DV), F32)],
        compiler_params=_cp("arbitrary"), name="gdn_rec",
    )(*mid, *mid)
    return o_f.reshape(B, Tt, W), o_b.reshape(B, Tt, W)


def _mixout_kernel(att_ref, *refs, tm, T, nt, n_o):
    o_refs, refs = refs[:2 * n_o], refs[2 * n_o:]
    (z_ref, u_ref, up_ref, un_ref, x_ref, g1_ref, sh2_ref, sc2_ref,
     gng_ref, wa_ref, wg_ref, wp_ref, wbd_ref, ps_ref, n2g_ref, wr_ref, x1_ref, h2_ref, aff_ref) = refs
    i = pl.program_id(1)
    o = jnp.concatenate([o_refs[k][...] + o_refs[n_o + k][...] for k in range(n_o)], axis=0)
    z = z_ref[...]
    parts = []
    for h in range(GDN_HEADS):
        sl = slice(h * GDN_DV, (h + 1) * GDN_DV)
        oh = o[:, sl]
        oh = oh * lax.rsqrt(jnp.mean(oh * oh, axis=-1, keepdims=True) + EPS) * gng_ref[...]
        parts.append((oh * _silu(z[:, sl])).astype(BF16))
    gdn = jnp.concatenate(parts, axis=1)

    u = u_ref[...]
    halo = SUBLANE
    ext = jnp.concatenate([jnp.where(i > 0, up_ref[...], 0.0), u, jnp.where(i < nt - 1, un_ref[...], 0.0)], axis=0)
    n_ext = tm + 2 * halo
    back = lambda a, s: pltpu.roll(a, s, 0)
    ahead = lambda a, s: pltpu.roll(a, n_ext - s, 0)
    s2 = ext + back(ext, 1)
    s4 = back(s2, 1) + ahead(s2, 1)
    s8 = back(s4, 2) + ahead(s4, 2)
    s16 = back(s8, 4) + ahead(s8, 4)
    t = i * tm + lax.broadcasted_iota(jnp.int32, (tm, 1), 0)
    lane = lax.broadcasted_iota(jnp.int32, (tm, POOL_WIDTH), 1)
    mean = None
    for gi, (win, sw) in reversed(list(enumerate(zip(POOL_WINDOWS, (s2, s4, s8, s16))))):
        lo = jnp.maximum(t - win // 2, 0)
        hi = jnp.minimum(t - win // 2 + win, T)
        m = sw[halo:halo + tm, :] / (hi - lo).astype(F32)
        mean = m if mean is None else jnp.where(lane < (gi + 1) * POOL_GROUP, m, mean)
    yp = jnp.dot((mean - u).astype(BF16), wbd_ref[...], preferred_element_type=F32) * ps_ref[...]

    y = (lax.dot_general(att_ref[...], wa_ref[...], TN, preferred_element_type=F32)
         + jnp.dot(gdn, wg_ref[...], preferred_element_type=F32)
         + jnp.dot(yp.astype(BF16), wp_ref[...], preferred_element_type=F32))
    x1 = x_ref[...] + g1_ref[...] * y
    x1_ref[...] = x1
    h2 = x1 * lax.rsqrt(jnp.mean(x1 * x1, axis=-1, keepdims=True) + EPS) * n2g_ref[...]
    h2 = h2 * (1.0 + sc2_ref[...]) + sh2_ref[...]
    h2_ref[...] = h2.astype(BF16)
    lg = lax.dot_general(wr_ref[...], h2, NT, precision=HI, preferred_element_type=F32)
    e = jnp.exp(lg - jnp.max(lg, axis=0, keepdims=True))
    aff_ref[...] = e / jnp.sum(e, axis=0, keepdims=True)


def mixout(att, o, o_off, z, u, x, mod, wts):
    B, T, D = x.shape
    tm = min(512, T)
    nt = T // tm
    tb = tm // SUBLANE
    nb = T // SUBLANE
    to = math.gcd(tm, o_off) if o_off else tm
    n_o = tm // to
    W = GDN_HEADS * GDN_DV
    cur = lambda n: pl.BlockSpec((None, tm, n), lambda b, i: (b, i, 0))
    odir = [pl.BlockSpec((None, to, W), lambda b, i, k=k: (b, i * n_o + o_off // to + k, 0)) for k in range(n_o)]
    modspec = lambda k: pl.BlockSpec((None, 1, D), lambda b, i: (b, 0, k))
    full = lambda a: pl.BlockSpec(a.shape, lambda b, i: (0,) * a.ndim)
    return pl.pallas_call(
        functools.partial(_mixout_kernel, tm=tm, T=T, nt=nt, n_o=n_o), grid=(B, nt),
        in_specs=[pl.BlockSpec((None, MLA_HEADS * MLA_V, tm), lambda b, i: (b, 0, i))] + odir + odir + [cur(W), cur(POOL_WIDTH),
                  pl.BlockSpec((None, SUBLANE, POOL_WIDTH), lambda b, i: (b, jnp.maximum(i * tb - 1, 0), 0)),
                  pl.BlockSpec((None, SUBLANE, POOL_WIDTH), lambda b, i: (b, jnp.minimum((i + 1) * tb, nb - 1), 0)),
                  cur(D), modspec(2), modspec(3), modspec(4)] + [full(a) for a in wts],
        out_specs=[cur(D), cur(D), pl.BlockSpec((None, N_EXPERTS, tm), lambda b, i: (b, 0, i))],
        out_shape=[jax.ShapeDtypeStruct((B, T, D), F32), jax.ShapeDtypeStruct((B, T, D), BF16),
                   jax.ShapeDtypeStruct((B, N_EXPERTS, T), F32)],
        compiler_params=_cp("parallel", "parallel"), name="mixout",
    )(att, *([o[0]] * n_o), *([o[1]] * n_o), z, u, u, u, x, mod, mod, mod, *wts)


MOE_SUB = 256
MOE_SLOTS = 128


def _route_kernel(aff_ref, gate_ref, slot_ref, starts_ref, *, cap, T):
    aff = aff_ref[...]

    def body(it, res):
        cand = res | jnp.left_shift(jnp.int32(1), 30 - it)
        cnt = jnp.sum((aff >= pltpu.bitcast(cand, F32)).astype(jnp.int32), axis=-1, keepdims=True)
        return jnp.where(cnt >= cap, cand, res)

    bits = lax.fori_loop(0, 31, body, jnp.zeros((N_EXPERTS, 1), jnp.int32))
    thr = pltpu.bitcast(bits, F32)
    above = pltpu.bitcast(bits + 1, F32)
    n_gt = jnp.sum((aff >= above).astype(jnp.int32), axis=-1, keepdims=True)
    need = (cap - n_gt).astype(F32)
    upper = (lax.broadcasted_iota(jnp.int32, (LANE, LANE), 0)
             < lax.broadcasted_iota(jnp.int32, (LANE, LANE), 1)).astype(BF16)
    seen = jnp.zeros((N_EXPERTS, 1), F32)
    taken = jnp.zeros((N_EXPERTS, 1), F32)
    lane = lax.broadcasted_iota(jnp.int32, (N_EXPERTS, LANE), 1)
    starts = jnp.zeros((N_EXPERTS, LANE), jnp.int32)
    per_sub = MOE_SUB // LANE
    for j in range(T // LANE):
        if j % per_sub == 0:
            starts = jnp.where(lane == j // per_sub, taken.astype(jnp.int32), starts)
        sl = slice(j * LANE, (j + 1) * LANE)
        aj = aff[:, sl]
        eq = jnp.where(aj >= thr, jnp.where(aj < above, 1.0, 0.0), 0.0)
        rank = jnp.dot(eq.astype(BF16), upper, preferred_element_type=F32) + seen
        sel = jnp.where(aj >= above, 1.0, jnp.where(rank < need, eq, 0.0))
        gate_ref[:, sl] = sel * aj
        slot = jnp.dot(sel.astype(BF16), upper, preferred_element_type=F32) + taken
        slot_ref[:, sl] = jnp.where(sel > 0.0, slot, -1.0)
        seen = seen + jnp.sum(eq, axis=-1, keepdims=True)
        taken = taken + jnp.sum(sel, axis=-1, keepdims=True)
    starts_ref[...] = jnp.where(lane == T // MOE_SUB, taken.astype(jnp.int32), starts)


def route(aff, cap):
    B, E, T = aff.shape
    assert T % MOE_SUB == 0 and T // MOE_SUB < LANE
    spec = pl.BlockSpec((None, E, T), lambda b: (b, 0, 0))
    return pl.pallas_call(
        functools.partial(_route_kernel, cap=cap, T=T), grid=(B,), in_specs=[spec],
        out_specs=[spec, spec, pl.BlockSpec((None, E, LANE), lambda b: (b, 0, 0))],
        out_shape=[jax.ShapeDtypeStruct(aff.shape, F32), jax.ShapeDtypeStruct(aff.shape, F32),
                   jax.ShapeDtypeStruct((B, E, LANE), jnp.int32)],
        compiler_params=_cp("parallel"), name="route",
    )(aff)


def _slot_blocks(starts_ref, b, e, sub, R):
    s0 = starts_ref[b, e, sub]
    s1 = starts_ref[b, e, sub + 1]
    return s0 // R, (s1 + R - 1) // R


def _moe_ffn_kernel(starts_ref, h_ref, slot_ref, gate_ref, wg_ref, wu_ref, wd_ref, y_ref, xs_ref, gs_ref, *, n_sub, R):
    e, b, j = pl.program_id(0), pl.program_id(1), pl.program_id(2)

    @pl.when(j == 0)
    def _():
        xs_ref[...] = jnp.zeros_like(xs_ref)
        gs_ref[...] = jnp.zeros_like(gs_ref)

    rows = lax.broadcasted_iota(jnp.int32, (R, 1), 0)
    for sub in range(n_sub):
        tsl = slice(sub * MOE_SUB, (sub + 1) * MOE_SUB)
        h = h_ref[tsl, :]
        srow = slot_ref[:, tsl]
        grow = gate_ref[:, tsl]

        def gather(i, carry):
            base = pl.multiple_of(i * R, R)
            match = srow == (base + rows).astype(F32)
            xs_ref[pl.ds(base, R), :] += jnp.dot(jnp.where(match, 1.0, 0.0).astype(BF16), h,
                                                 preferred_element_type=F32)
            gs_ref[pl.ds(base, R), :] += jnp.sum(jnp.where(match, grow, 0.0), axis=1, keepdims=True)
            return carry

        lax.fori_loop(*_slot_blocks(starts_ref, b, e, j * n_sub + sub, R), gather, 0)

    @pl.when(j == pl.num_programs(2) - 1)
    def _():
        xs = xs_ref[...].astype(BF16)
        a = jnp.dot(xs, wg_ref[...].astype(BF16), preferred_element_type=F32)
        hid = (_silu(a) * jnp.dot(xs, wu_ref[...].astype(BF16), preferred_element_type=F32)).astype(BF16)
        y_ref[...] = (jnp.dot(hid, wd_ref[...].astype(BF16), preferred_element_type=F32) * gs_ref[...]).astype(BF16)


def _moe_combine_kernel(starts_ref, y_ref, slot_ref, x1_ref, g2_ref, o_ref, acc_ref, *, n_sub, R, NE, n_blk):
    b, j, eg = pl.program_id(0), pl.program_id(1), pl.program_id(2)

    @pl.when(eg == 0)
    def _():
        acc_ref[...] = jnp.zeros_like(acc_ref)

    lane = lax.broadcasted_iota(jnp.int32, slot_ref.shape, 1)
    slots = slot_ref[...]
    scol = [jnp.sum(jnp.where(lane == eg * NE + k, slots, 0.0), axis=-1, keepdims=True) for k in range(NE)]
    wide = lax.broadcasted_iota(jnp.int32, (1, NE * R), 1)
    which = [wide // R == k for k in range(NE)]
    within = wide % R
    for sub in range(n_sub):
        tsl = slice(sub * MOE_SUB, (sub + 1) * MOE_SUB)
        tok_slot = jnp.zeros((MOE_SUB, NE * R), F32)
        for k in range(NE):
            tok_slot = jnp.where(which[k], scol[k][tsl], tok_slot)
        lo, hi = zip(*[_slot_blocks(starts_ref, b, eg * NE + k, j * n_sub + sub, R) for k in range(NE)])
        trips = functools.reduce(jnp.maximum, [hi[k] - lo[k] for k in range(NE)])

        def scatter(it, carry):
            target = jnp.full((1, NE * R), -2, jnp.int32)
            rows = []
            for k in range(NE):
                blk = jnp.minimum(lo[k] + it, n_blk - 1)
                target = jnp.where(which[k], jnp.where(lo[k] + it < hi[k], blk * R + within, -2), target)
                rows.append(y_ref[k, pl.ds(pl.multiple_of(blk * R, R), R), :])
            onehot = jnp.where(tok_slot == target.astype(F32), 1.0, 0.0).astype(BF16)
            acc_ref[tsl, :] += jnp.dot(onehot, jnp.concatenate(rows, axis=0), preferred_element_type=F32)
            return carry

        lax.fori_loop(0, trips, scatter, 0)

    @pl.when(eg == pl.num_programs(2) - 1)
    def _():
        o_ref[...] = x1_ref[...] + g2_ref[...] * acc_ref[...]


def moe(h2, routed, x1, mod, wg, wu, wd, layer, cap):
    gate, slot, starts = routed
    B, T, D = x1.shape
    E, F = N_EXPERTS, wg.shape[-1]
    R = min(MOE_SLOTS, cap)
    assert cap % R == 0
    tt = min(2048, T)
    n_sub = tt // MOE_SUB
    row = pl.BlockSpec((None, None, 1, tt), lambda e, b, j, st: (b, e, 0, j))
    wspec = lambda r, c: pl.BlockSpec((None, None, r, c), lambda e, b, j, st: (layer, e, 0, 0))
    y = pl.pallas_call(
        functools.partial(_moe_ffn_kernel, n_sub=n_sub, R=R),
        grid_spec=pltpu.PrefetchScalarGridSpec(
            num_scalar_prefetch=1, grid=(E, B, T // tt),
            in_specs=[pl.BlockSpec((None, tt, D), lambda e, b, j, st: (b, j, 0)), row, row,
                      wspec(D, F), wspec(D, F), wspec(F, D)],
            out_specs=pl.BlockSpec((None, None, cap, D), lambda e, b, j, st: (b, e, 0, 0)),
            scratch_shapes=[pltpu.VMEM((cap, D), F32), pltpu.VMEM((cap, 1), F32)]),
        out_shape=jax.ShapeDtypeStruct((B, E, cap, D), BF16),
        compiler_params=_cp("parallel", "parallel", "arbitrary"), name="moe_ffn",
    )(starts, h2, slot.reshape(B, E, 1, T), gate.reshape(B, E, 1, T), wg, wu, wd)
    tc = min(1024, T)
    ne = 4
    rc = min(2 * LANE // ne, cap)
    tok = lambda n: pl.BlockSpec((None, tc, n), lambda b, j, e, st: (b, j, 0))
    return pl.pallas_call(
        functools.partial(_moe_combine_kernel, n_sub=tc // MOE_SUB, R=rc, NE=ne, n_blk=cap // rc),
        grid_spec=pltpu.PrefetchScalarGridSpec(
            num_scalar_prefetch=1, grid=(B, T // tc, E // ne),
            in_specs=[pl.BlockSpec((None, ne, cap, D), lambda b, j, e, st: (b, e, 0, 0)),
                      tok(E), tok(D), pl.BlockSpec((None, 1, D), lambda b, j, e, st: (b, 0, 5))],
            out_specs=tok(D),
            scratch_shapes=[pltpu.VMEM((tc, D), F32)]),
        out_shape=jax.ShapeDtypeStruct((B, T, D), F32),
        compiler_params=_cp("parallel", "parallel", "arbitrary"), name="moe_combine",
    )(starts, y, jnp.swapaxes(slot, 1, 2), x1, mod)


def _in_cols():
    src = np.full((IN_PAD,), -1, np.int64)
    splits = (MLA_Q_LORA, MLA_KV_LORA, MLA_ROPE, 512, 512, 512, 512, 2 * GDN_HEADS, 2 * GDN_HEADS, POOL_WIDTH)
    o = np.concatenate([[0], np.cumsum(splits)])
    put = lambda name, at, lo, n: src.__setitem__(slice(SEG[name][0] + at, SEG[name][0] + at + n), np.arange(lo, lo + n))
    put("pq", 0, o[0], MLA_Q_LORA)
    put("pkv", 0, o[1], MLA_KV_LORA)
    put("pkr", MLA_NOPE, o[2], MLA_ROPE)
    for name, k in (("gq", 3), ("gk", 4), ("gv", 5), ("gz", 6)):
        put(name, 0, o[k], 512)
    for d in range(2):
        put("gab", d * LANE, o[7] + d * GDN_HEADS, GDN_HEADS)
        put("gab", d * LANE + GDN_HEADS, o[8] + d * GDN_HEADS, GDN_HEADS)
    put("pool", 0, o[9], POOL_WIDTH)
    return src


def _take_cols(w, src, axis):
    pieces, p, n = [], 0, len(src)
    while p < n:
        q = p + 1
        while q < n and ((src[q] < 0 and src[p] < 0) or (src[p] >= 0 and src[q] == src[q - 1] + 1)):
            q += 1
        if src[p] < 0:
            shape = list(w.shape)
            shape[axis] = q - p
            pieces.append(jnp.zeros(shape, w.dtype))
        else:
            pieces.append(lax.slice_in_dim(w, int(src[p]), int(src[p]) + q - p, axis=axis))
        p = q
    return jnp.concatenate(pieces, axis=axis)


def _head_pad_src(per_head, lo, n):
    src = np.full((MLA_HEADS * HEAD_PAD,), -1, np.int64)
    for h in range(MLA_HEADS):
        src[h * HEAD_PAD:h * HEAD_PAD + n] = h * per_head + lo + np.arange(n)
    return src


def _rope_tables(T, rotate):
    cos = np.ones((T, LANE), np.float32)
    sa = np.zeros((T, LANE), np.float32)
    sb = np.zeros((T, LANE), np.float32)
    if rotate:
        n_freq = MLA_ROPE // 4
        inv = ROPE_THETA ** (-np.arange(n_freq, dtype=np.float64) / n_freq)
        pos_r = np.repeat(np.arange(T // GRID_W, dtype=np.float64), GRID_W)
        pos_c = np.tile(np.arange(GRID_W, dtype=np.float64), T // GRID_W)
        for base, pos in ((MLA_NOPE, pos_r), (MLA_NOPE + 2 * n_freq, pos_c)):
            ang = pos[:, None] * inv[None, :]
            c, s = np.cos(ang), np.sin(ang)
            cos[:, base:base + n_freq] = c
            cos[:, base + n_freq:base + 2 * n_freq] = c
            sa[:, base:base + n_freq] = -s
            sb[:, base + n_freq:base + 2 * n_freq] = s
    return jnp.asarray(cos), jnp.asarray(sa), jnp.asarray(sb)


def _lane_vec(vals_by_dir, at):
    v = jnp.zeros((2, LANE), F32).at[:, at:at + GDN_HEADS].set(vals_by_dir)
    return v.reshape(1, 2 * LANE)


def kernel(x, c, ctx, c_ctx, ada_w, ada_b, norm1_g, norm2_g, w_in, mla_q_a_norm, mla_w_uq, mla_kv_a_norm, mla_w_ukv, mla_q_norm, mla_k_norm, gdn_conv_w, gdn_a_log, gdn_dt_bias, gdn_norm_g, pool_w, pool_scale, w_out, moe_router, moe_w_gate, moe_w_up, moe_w_down):
    B, T, D = x.shape
    Tc = ctx.shape[1]
    L = ada_w.shape[0]
    cvec = jnp.concatenate([c, c_ctx[None, :], jnp.zeros((SUBLANE - B - 1, D), F32)], axis=0)
    mod = ada_mod(cvec, ada_w, ada_b)
    rope_lat = _rope_tables(T, True)
    rope_ctx = _rope_tables(Tc, False)
    in_src = _in_cols()
    uq_src = _head_pad_src(MLA_QK, 0, MLA_QK)
    uk_src = _head_pad_src(MLA_NOPE + MLA_V, 0, MLA_NOPE)
    uv_src = _head_pad_src(MLA_NOPE + MLA_V, MLA_NOPE, MLA_V)
    pad_to = lambda v, n: jnp.pad(v, (0, n - v.shape[0])).reshape(1, n)

    xc = ctx
    for l in range(L):
        need_ctx = l < L - 1
        mod_lat = mod[l, :B].reshape(B, 1, ADA_CHUNKS * D)
        mod_ctx = jnp.broadcast_to(mod[l, B].reshape(1, 1, ADA_CHUNKS * D), (B, 1, ADA_CHUNKS * D))
        w_in_p = _take_cols(w_in[l], in_src, 1).astype(BF16)
        prep_w = (
            pad_to(mla_q_a_norm[l], 256),
            jnp.pad(_take_cols(mla_w_uq[l], uq_src, 1), ((0, 256 - MLA_Q_LORA), (0, 0))).astype(BF16),
            mla_kv_a_norm[l].reshape(1, MLA_KV_LORA),
            _take_cols(mla_w_ukv[l], uk_src, 1).astype(BF16),
            _take_cols(mla_w_ukv[l], uv_src, 1).T.astype(BF16),
            pad_to(mla_q_norm[l] * (MLA_QK ** -0.5 * math.log2(math.e)), HEAD_PAD),
            pad_to(mla_k_norm[l], HEAD_PAD),
            gdn_conv_w[l],
            _lane_vec(gdn_a_log[l], 0),
            _lane_vec(gdn_dt_bias[l], 0),
        )
        wo = w_out[l]
        n_att = MLA_HEADS * MLA_V
        n_gdn = GDN_HEADS * GDN_DV
        wbd = jnp.zeros((POOL_WIDTH, POOL_WIDTH), F32)
        for gi in range(len(POOL_WINDOWS)):
            wbd = wbd.at[gi * POOL_GROUP:(gi + 1) * POOL_GROUP, gi * POOL_GROUP:(gi + 1) * POOL_GROUP].set(pool_w[l, gi])
        mix_w = (
            gdn_norm_g[l].reshape(1, GDN_DV),
            wo[:n_att].astype(BF16),
            wo[n_att:n_att + n_gdn].astype(BF16),
            wo[n_att + n_gdn:].astype(BF16),
            wbd.astype(BF16),
            pool_scale[l].reshape(1, POOL_WIDTH),
            norm2_g[l].reshape(1, D),
            moe_router[l].T,
        )

        p_lat = dict(zip(SEG, inproj(x, mod_lat, norm1_g[l], w_in_p)))
        p_ctx = dict(zip(SEG, inproj(xc, mod_ctx, norm1_g[l], w_in_p)))
        a_ctx = prep(p_ctx, prep_w, rope_ctx, T, T + Tc)
        a_lat = prep(p_lat, prep_w, rope_lat, 0, T + Tc, shared=a_ctx)
        att_l = attention(a_lat["Q"], a_lat["K"], a_lat["VT"], 0, T + Tc)
        o_all = gdn_scan(a_lat["q"], a_lat["k"], a_lat["v"], a_lat["gb"], Tc)

        def channel_mix(att, o_off, p, xin, m):
            Tn = xin.shape[1]
            x1, h2, aff = mixout(att, o_all, o_off, p["gz"], p["pool"], xin, m, mix_w)
            cap = EC_CAPACITY_FACTOR * Tn // N_EXPERTS
            return moe(h2, route(aff, cap), x1, m, moe_w_gate, moe_w_up, moe_w_down, l, cap)

        x = channel_mix(att_l, 0, p_lat, x, mod_lat)
        if need_ctx:
            att_c = attention(a_ctx["Q"], a_lat["K"], a_lat["VT"], T, Tc)
            xc = channel_mix(att_c, T, p_ctx, xc, mod_ctx)
    return x
```

```python
import functools
import math

import numpy as np
import jax
import jax.numpy as jnp
from jax import lax
from jax.experimental import pallas as pl
from jax.experimental.pallas import tpu as pltpu

F32 = jnp.float32
BF16 = jnp.bfloat16
HI = lax.Precision.HIGHEST

EPS = 1e-6
GRID_W = 64
ADA_CHUNKS = 6
MLA_HEADS = 4
MLA_NOPE = 64
MLA_ROPE = 32
MLA_QK = MLA_NOPE + MLA_ROPE
MLA_V = 64
MLA_Q_LORA = 192
MLA_KV_LORA = 128
ROPE_THETA = 10000.0
GDN_HEADS = 4
GDN_DK = 128
GDN_DV = 128
GDN_CHUNK = 64
POOL_WINDOWS = (2, 4, 8, 16)
POOL_GROUP = 64
POOL_WIDTH = POOL_GROUP * len(POOL_WINDOWS)
N_EXPERTS = 16
EC_CAPACITY_FACTOR = 2

LANE = 128
SUBLANE = 8
HEAD_PAD = 128
VMEM_LIMIT = 48 * 1024 * 1024

NT = (((1,), (1,)), ((), ()))
TN = (((0,), (0,)), ((), ()))

SEG = {}
_off = 0
for _name, _w in (("pq", 256), ("pkv", 128), ("pkr", 128), ("gq", 512), ("gk", 512), ("gv", 512),
                  ("gz", 512), ("gab", 256), ("pool", 256)):
    SEG[_name] = (_off, _w)
    _off += _w
IN_PAD = _off


def _cp(*dims):
    return pltpu.CompilerParams(dimension_semantics=dims, vmem_limit_bytes=VMEM_LIMIT)


def _silu(v):
    return v / (1.0 + jnp.exp(-v))


def _ada_kernel(c_ref, w_ref, b_ref, o_ref):
    s = _silu(c_ref[...])
    o_ref[...] = jnp.dot(s, w_ref[...], precision=HI, preferred_element_type=F32) + b_ref[...]


def ada_mod(cvec, ada_w, ada_b):
    L, D, N = ada_w.shape
    tn = N // 4
    return pl.pallas_call(
        _ada_kernel, grid=(L, N // tn),
        in_specs=[pl.BlockSpec((SUBLANE, D), lambda l, j: (0, 0)),
                  pl.BlockSpec((None, D, tn), lambda l, j: (l, 0, j)),
                  pl.BlockSpec((None, 1, tn), lambda l, j: (l, 0, j))],
        out_specs=pl.BlockSpec((None, SUBLANE, tn), lambda l, j: (l, 0, j)),
        out_shape=jax.ShapeDtypeStruct((L, SUBLANE, N), F32),
        compiler_params=_cp("parallel", "parallel"), name="ada_mod",
    )(cvec, ada_w, ada_b.reshape(L, 1, N))


def _inproj_kernel(x_ref, sh_ref, sc_ref, g_ref, w_ref, *out_refs):
    x = x_ref[...]
    h = x * lax.rsqrt(jnp.mean(x * x, axis=-1, keepdims=True) + EPS) * g_ref[...]
    hb = (h * (1.0 + sc_ref[...]) + sh_ref[...]).astype(BF16)
    for (off, n), o_ref in zip(SEG.values(), out_refs):
        o_ref[...] = jnp.dot(hb, w_ref[:, off:off + n], preferred_element_type=F32)


def inproj(x, mod, norm_g, w_in_p):
    B, T, D = x.shape
    tm = min(512, T)
    modspec = lambda k: pl.BlockSpec((None, 1, D), lambda b, i, k=k: (b, 0, k))
    return pl.pallas_call(
        _inproj_kernel, grid=(B, T // tm),
        in_specs=[pl.BlockSpec((None, tm, D), lambda b, i: (b, i, 0)), modspec(0), modspec(1),
                  pl.BlockSpec((1, D), lambda b, i: (0, 0)),
                  pl.BlockSpec((D, IN_PAD), lambda b, i: (0, 0))],
        out_specs=[pl.BlockSpec((None, tm, n), lambda b, i: (b, i, 0)) for _, n in SEG.values()],
        out_shape=[jax.ShapeDtypeStruct((B, T, n), F32) for _, n in SEG.values()],
        compiler_params=_cp("parallel", "parallel"), name="inproj",
    )(x, mod, mod, norm_g.reshape(1, D), w_in_p)


def _prep_kernel(*refs, nt, tm):
    (pq_ref, pkv_ref, pkr_ref, gq_ref, gk_ref, gv_ref, gqp_ref, gkp_ref, gvp_ref,
     gqn_ref, gkn_ref, gvn_ref, gab_ref, qan_ref, wuq_ref, kvan_ref, wuk_ref, wuv_ref,
     qn_ref, kn_ref, cos_ref, sa_ref, sb_ref, cw_ref, alog_ref, dt_ref) = refs[:26]
    Q_ref, K_ref, VT_ref, q_ref, k_ref, v_ref, gb_ref = refs[-7:]
    i = pl.program_id(1)
    cos, sa, sb = cos_ref[...], sa_ref[...], sb_ref[...]

    def rope(xh):
        return xh * cos + pltpu.roll(xh, LANE - 8, 1) * sa + pltpu.roll(xh, 8, 1) * sb

    pq = pq_ref[...]
    qa = pq * lax.rsqrt(jnp.sum(pq * pq, axis=-1, keepdims=True) * (1.0 / MLA_Q_LORA) + EPS) * qan_ref[...]
    qall = jnp.dot(qa.astype(BF16), wuq_ref[...], preferred_element_type=F32)
    pkv = pkv_ref[...]
    kva = (pkv * lax.rsqrt(jnp.mean(pkv * pkv, axis=-1, keepdims=True) + EPS) * kvan_ref[...]).astype(BF16)
    kall = jnp.dot(kva, wuk_ref[...], preferred_element_type=F32)
    vt = lax.dot_general(wuv_ref[...], kva, NT, preferred_element_type=F32)
    ones_row = lax.broadcasted_iota(jnp.int32, vt.shape, 0) % HEAD_PAD == MLA_V
    VT_ref[...] = jnp.where(ones_row, 1.0, vt).astype(BF16)
    pkr = pkr_ref[...]
    for h in range(MLA_HEADS):
        sl = slice(h * HEAD_PAD, (h + 1) * HEAD_PAD)
        qh = qall[:, sl]
        qh = qh * lax.rsqrt(jnp.sum(qh * qh, axis=-1, keepdims=True) * (1.0 / MLA_QK) + EPS) * qn_ref[...]
        Q_ref[:, sl] = rope(qh).astype(BF16)
        kh = kall[:, sl] + pkr
        kh = kh * lax.rsqrt(jnp.sum(kh * kh, axis=-1, keepdims=True) * (1.0 / MLA_QK) + EPS) * kn_ref[...]
        K_ref[:, sl] = rope(kh).astype(BF16)

    rid = lax.broadcasted_iota(jnp.int32, (SUBLANE, GDN_HEADS * GDN_DK), 0)

    def conv_silu(u_ref, up_ref, un_ref, c0):
        u = u_ref[...]
        n = u.shape[1]
        prev_row = jnp.where(i > 0, up_ref[SUBLANE - 1:SUBLANE, :], 0.0)
        next_row = jnp.where(i < nt - 1, un_ref[0:1, :], 0.0)
        um = pltpu.roll(u, 1, 0)
        um = jnp.concatenate([jnp.where(rid == 0, prev_row, um[:SUBLANE]), um[SUBLANE:]], axis=0)
        up = pltpu.roll(u, tm - 1, 0)
        up = jnp.concatenate([up[:tm - SUBLANE], jnp.where(rid == SUBLANE - 1, next_row, up[tm - SUBLANE:])], axis=0)
        y = um * cw_ref[0:1, c0:c0 + n] + u * cw_ref[1:2, c0:c0 + n] + up * cw_ref[2:3, c0:c0 + n]
        return _silu(y)

    cq = conv_silu(gq_ref, gqp_ref, gqn_ref, 0)
    ck = conv_silu(gk_ref, gkp_ref, gkn_ref, GDN_HEADS * GDN_DK)
    v_ref[...] = conv_silu(gv_ref, gvp_ref, gvn_ref, 2 * GDN_HEADS * GDN_DK)
    for h in range(GDN_HEADS):
        sl = slice(h * GDN_DK, (h + 1) * GDN_DK)
        qh = cq[:, sl]
        q_ref[:, sl] = qh * lax.rsqrt(jnp.sum(qh * qh, axis=-1, keepdims=True) + EPS) * (GDN_DK ** -0.5)
        kh = ck[:, sl]
        k_ref[:, sl] = kh * lax.rsqrt(jnp.sum(kh * kh, axis=-1, keepdims=True) + EPS)

    pre = gab_ref[...]
    lane = lax.broadcasted_iota(jnp.int32, pre.shape, 1) % LANE
    sp_in = pre + dt_ref[...]
    softplus = jnp.maximum(sp_in, 0.0) + jnp.log(1.0 + jnp.exp(-jnp.abs(sp_in)))
    g = -jnp.exp(alog_ref[...]) * softplus
    beta = 1.0 / (1.0 + jnp.exp(-pre))
    gb_ref[...] = jnp.where(lane < GDN_HEADS, g, jnp.where(lane < 2 * GDN_HEADS, beta, 0.0))


def prep(p, wts, rope_tabs, row_off, t_all, shared=None):
    pq, pkv, pkr, gq, gk, gv, gab = (p[k] for k in ("pq", "pkv", "pkr", "gq", "gk", "gv", "gab"))
    B, T, _ = pq.shape
    tm = min(256, T)
    assert row_off % tm == 0
    nt = T // tm
    tb = tm // SUBLANE
    nb = T // SUBLANE
    ro = row_off // tm
    cur = lambda n: pl.BlockSpec((None, tm, n), lambda b, i: (b, i, 0))
    dst = lambda n: pl.BlockSpec((None, tm, n), lambda b, i: (b, i + ro, 0))
    prv = lambda n: pl.BlockSpec((None, SUBLANE, n), lambda b, i: (b, jnp.maximum(i * tb - 1, 0), 0))
    nxt = lambda n: pl.BlockSpec((None, SUBLANE, n), lambda b, i: (b, jnp.minimum((i + 1) * tb, nb - 1), 0))
    full = lambda a: pl.BlockSpec(a.shape, lambda b, i: (0,) * a.ndim)
    tab = pl.BlockSpec((tm, LANE), lambda b, i: (i, 0))
    W = GDN_HEADS * GDN_DK
    outs = [("Q", MLA_HEADS * HEAD_PAD, BF16), ("K", MLA_HEADS * HEAD_PAD, BF16), ("VT", None, BF16),
            ("q", W, F32), ("k", W, F32), ("v", W, F32), ("gb", 2 * LANE, F32)]
    HP = MLA_HEADS * HEAD_PAD
    shared = [] if shared is None else [shared[n] for n, _, _ in outs[1:]]
    n_in = 26
    res = pl.pallas_call(
        functools.partial(_prep_kernel, nt=nt, tm=tm), grid=(B, nt),
        in_specs=[cur(256), cur(128), cur(128), cur(W), cur(W), cur(W), prv(W), prv(W), prv(W),
                  nxt(W), nxt(W), nxt(W), cur(256)] + [full(a) for a in wts[:7]] + [tab, tab, tab]
                 + [full(a) for a in wts[7:]] + [pl.BlockSpec(memory_space=pl.ANY)] * len(shared),
        out_specs=[cur(outs[0][1])] + [dst(n) if n else pl.BlockSpec((None, HP, tm), lambda b, i: (b, 0, i + ro))
                                       for _, n, _ in outs[1:]],
        out_shape=[jax.ShapeDtypeStruct((B, T, outs[0][1]), outs[0][2])]
                  + [jax.ShapeDtypeStruct((B, t_all, n) if n else (B, HP, t_all), dt) for _, n, dt in outs[1:]],
        input_output_aliases={n_in + k: 1 + k for k in range(len(shared))},
        compiler_params=_cp("parallel", "parallel"), name="prep",
    )(pq, pkv, pkr, gq, gk, gv, gq, gk, gv, gq, gk, gv, gab, *wts[:7], *rope_tabs, *wts[7:], *shared)
    return dict(zip([n for n, _, _ in outs], res))


def _attn_kernel(q_ref, k_ref, vt_ref, o_ref, sa_ref, sb_ref, *, ck, nk):
    q = q_ref[...]
    tq = q.shape[0]

    def scores(j):
        return lax.dot_general(k_ref[j * ck:(j + 1) * ck, :], q, NT, preferred_element_type=F32)

    nv = MLA_V + 16

    def update(carry, s_ref, j):
        m, acc = carry
        s = s_ref[...]
        m_new = jnp.maximum(m, jnp.max(s, axis=0, keepdims=True))
        p = jnp.exp2(s - m_new).astype(BF16)
        acc = jnp.exp2(m - m_new) * acc + jnp.dot(vt_ref[0:nv, j * ck:(j + 1) * ck], p, preferred_element_type=F32)
        return m_new, acc

    bufs = (sa_ref, sb_ref)
    bufs[0][...] = scores(0)
    carry = (jnp.full((1, tq), -1e30, F32), jnp.zeros((nv, tq), F32))
    for j in range(nk):
        if j + 1 < nk:
            bufs[(j + 1) % 2][...] = scores(j + 1)
        carry = update(carry, bufs[j % 2], j)
    acc = carry[1]
    o = acc[:MLA_V] / acc[MLA_V:MLA_V + 1]
    o_ref[...] = o.astype(o_ref.dtype)


def attention(Q, K, VT, k_off, Tk):
    B, Tq, _ = Q.shape
    assert k_off % Tk == 0
    kb = k_off // Tk
    tq = min(1024, Tq)
    ck = next(c for c in (384, 256, 128) if Tk % c == 0)
    qo = pl.BlockSpec((None, tq, HEAD_PAD), lambda b, h, i: (b, i, h))
    return pl.pallas_call(
        functools.partial(_attn_kernel, ck=ck, nk=Tk // ck), grid=(B, MLA_HEADS, Tq // tq),
        in_specs=[qo, pl.BlockSpec((None, Tk, HEAD_PAD), lambda b, h, i: (b, kb, h)),
                  pl.BlockSpec((None, HEAD_PAD, Tk), lambda b, h, i: (b, h, kb))],
        out_specs=pl.BlockSpec((None, MLA_V, tq), lambda b, h, i: (b, h, i)),
        out_shape=jax.ShapeDtypeStruct((B, MLA_HEADS * MLA_V, Tq), BF16),
        scratch_shapes=[pltpu.VMEM((ck, tq), F32), pltpu.VMEM((ck, tq), F32)],
        compiler_params=_cp("parallel", "parallel", "parallel"), name="attention",
    )(Q, K, VT)


GDN_GROUP = 4


def _gdn_prep_kernel(q_ref, k_ref, v_ref, gb_ref, wq_ref, u_ref, qk_ref, kdt_ref, egl_ref, *, C, G):
    H, DK = GDN_HEADS, GDN_DK
    fwd = pl.program_id(1) == 0
    dot = functools.partial(jnp.dot, preferred_element_type=F32)
    row = lax.broadcasted_iota(jnp.int32, (C, H * C), 0)
    lane = lax.broadcasted_iota(jnp.int32, (C, H * C), 1)
    col = lane & (C - 1)
    hmask = [(lane >> int(math.log2(C))) == h for h in range(H)]
    wide = lax.broadcasted_iota(jnp.int32, (C, H * DK), 1)
    kmask = [(wide >> int(math.log2(DK))) == h for h in range(H)]
    ahead = jnp.where(fwd, row - col, col - row)
    incl = ahead >= 0
    strict = ahead > 0
    eye = (row == col).astype(F32)
    r1 = lax.broadcasted_iota(jnp.int32, (C, C), 0)
    c1 = lax.broadcasted_iota(jnp.int32, (C, C), 1)
    incl16 = (jnp.where(fwd, r1 - c1, c1 - r1) >= 0).astype(F32).astype(BF16)
    eye16 = (r1 == c1).astype(F32).astype(BF16)

    def blockdiag(m, masks):
        return jnp.concatenate([jnp.where(mk, m, 0.0) for mk in masks], axis=0).astype(BF16)

    def per_head(cols, width):
        n = cols.shape[0]
        if width == LANE:
            return jnp.concatenate([jnp.broadcast_to(cols[:, h:h + 1], (n, LANE)) for h in range(H)], axis=1)
        low = lax.broadcasted_iota(jnp.int32, (n, LANE), 1) < width
        return jnp.concatenate([jnp.where(low, cols[:, h:h + 1], cols[:, h + 1:h + 2]) for h in range(0, H, 2)], axis=1)

    def terms(v):
        hi = v.astype(BF16)
        rest = v - hi.astype(F32)
        mid = rest.astype(BF16)
        return jnp.concatenate([hi, mid, (rest - mid.astype(F32)).astype(BF16)], axis=1)

    fold = lambda a, axis: sum(jnp.split(a, 3, axis=axis)[1:], jnp.split(a, 3, axis=axis)[0])
    chunks = range(G)
    rows = [slice(g * C, (g + 1) * C) for g in chunks]
    gb = [gb_ref[r, :] for r in rows]
    gterms = [terms(v) for v in gb]
    gc = [fold(dot(incl16, t), 1) for t in gterms]
    gct = [fold(lax.dot_general(t, incl16, (((0,), (1,)), ((), ())), preferred_element_type=F32), 0)
           for t in gterms]
    glast = [jnp.where(fwd, v[C - 1:C, :], v[0:1, :]) for v in gc]
    k16 = [k_ref[r, :].astype(BF16) for r in rows]
    qkk = [lax.dot_general(jnp.concatenate([q_ref[rows[g], :].astype(BF16), k16[g]], axis=0),
                           blockdiag(k_ref[rows[g], :], kmask), NT, preferred_element_type=F32)
           for g in chunks]
    a, tinv = [], []
    same = lambda s: (row >> s) == (col >> s)
    pairs = jnp.where(same(1), 1.0, 0.0)
    for g in chunks:
        egl_ref[g] = jnp.broadcast_to(jnp.exp(glast[g]), (SUBLANE, LANE))
        grow = jnp.concatenate([gct[g][h:h + 1, :] for h in range(H)], axis=1)
        decay = jnp.exp(jnp.where(incl, per_head(gc[g], C) - grow, -1e30))
        qk_ref[g] = (qkk[g][:C] * decay).astype(BF16)
        a.append(jnp.where(strict, qkk[g][C:] * decay, 0.0) * per_head(gb[g][:, H:], C))
        tinv.append(eye - a[g] * pairs)
    for s in range(1, int(math.log2(C))):
        join = jnp.where(same(s + 1), jnp.where(same(s), 0.0, 1.0), 0.0)
        x16 = [tinv[g].astype(BF16) for g in chunks]
        xl = [dot(x16[g], blockdiag(a[g] * join, hmask)) for g in chunks]
        xlx = [dot(xl[g].astype(BF16), blockdiag(tinv[g], hmask)) for g in chunks]
        tinv = [tinv[g] - xlx[g] for g in chunks]
    gcw = [per_head(gc[g], DK) for g in chunks]
    bw = [per_head(gb[g][:, H:], DK) for g in chunks]
    wu = [dot(tinv[g].astype(BF16),
              jnp.concatenate([blockdiag(k_ref[rows[g], :] * (bw[g] * jnp.exp(gcw[g])), kmask),
                               blockdiag(v_ref[rows[g], :] * bw[g], kmask)], axis=1)) for g in chunks]
    kdt = [lax.dot_general((k_ref[rows[g], :] * jnp.exp(per_head(glast[g], DK) - gcw[g])).astype(BF16), eye16,
                           TN, preferred_element_type=F32) for g in chunks]
    for g in chunks:
        wq_ref[g, 0:C, :] = wu[g][:, :H * DK].astype(BF16)
        wq_ref[g, C:2 * C, :] = (q_ref[rows[g], :] * jnp.exp(gcw[g])).astype(BF16)
        u_ref[g] = wu[g][:, H * DK:]
        for h in range(H):
            kdt_ref[g, :, h * C:(h + 1) * C] = kdt[g][h * DK:(h + 1) * DK].astype(BF16)


def _gdn_rec_kernel(*refs, C, G, B):
    ins = (refs[0:5], refs[5:10])
    outs = refs[10:12]
    s_ref = refs[12]
    dot = functools.partial(jnp.dot, preferred_element_type=F32)

    @pl.when(pl.program_id(0) == 0)
    def _():
        s_ref[...] = jnp.zeros_like(s_ref)

    hsl = lambda h: slice(h * GDN_DK, (h + 1) * GDN_DK)
    csl = lambda h: slice(h * C, (h + 1) * C)
    for step in range(G):
        chains = [(d, b, h, step if d == 0 else G - 1 - step)
                  for d in range(2) for b in range(B) for h in range(GDN_HEADS)]
        S = {c: s_ref[c[0], c[1], c[2]] for c in chains}
        r = {(d, b, h, g): dot(ins[d][0][b, g, :, hsl(h)], S[(d, b, h, g)].astype(BF16))
             for (d, b, h, g) in chains}
        vn = {(d, b, h, g): (ins[d][1][b, g, :, hsl(h)] - r[(d, b, h, g)][:C]).astype(BF16) for (d, b, h, g) in chains}
        ou = {(d, b, h, g): dot(jnp.concatenate([ins[d][2][b, g, :, csl(h)], ins[d][3][b, g, :, csl(h)]], axis=0),
                                vn[(d, b, h, g)]) for (d, b, h, g) in chains}
        for c in chains:
            d, b, h, g = c
            outs[d][b, g, :, hsl(h)] = r[c][C:] + ou[c][:C]
            s_ref[d, b, h] = S[c] * ins[d][4][b, g, 0:1, h:h + 1] + ou[c][C:]


def gdn_scan(q, k, v, gb, n_ctx):
    B, Tt, W = q.shape
    C, G = GDN_CHUNK, GDN_GROUP
    n = Tt // C
    ng = n // G
    ncg = n_ctx // (C * G)
    assert n % G == 0 and n_ctx % (C * G) == 0
    gp = next(c for c in (12, 8, 6, 4, 3, 2, 1) if n % c == 0)
    tok = pl.BlockSpec((None, gp * C, W), lambda b, d, s: (b, s, 0))
    shapes = [((2 * C, W), BF16), ((C, W), F32), ((C, GDN_HEADS * C), BF16), ((GDN_DK, GDN_HEADS * C), BF16),
              ((SUBLANE, LANE), F32)]
    nat = lambda b, d, s: (b, d, s, 0, 0)
    mid = pl.pallas_call(
        functools.partial(_gdn_prep_kernel, C=C, G=gp), grid=(B, 2, n // gp),
        in_specs=[tok, tok, tok, pl.BlockSpec((None, gp * C, LANE), lambda b, d, s: (b, s, d))],
        out_specs=[pl.BlockSpec((None, None, gp, r, w), nat) for (r, w), _ in shapes],
        out_shape=[jax.ShapeDtypeStruct((B, 2, n, r, w), dt) for (r, w), dt in shapes],
        compiler_params=_cp("parallel", "parallel", "parallel"), name="gdn_prep",
    )(q, k, v, gb)

    fwd = lambda s: jnp.where(s < ncg, ng - ncg + s, s - ncg)
    bwd = lambda s: ng - 1 - s
    both = lambda r, w, d: pl.BlockSpec((B, None, G, r, w), (lambda s: (0, 0, fwd(s), 0, 0)) if d == 0
                                        else (lambda s: (0, 1, bwd(s), 0, 0)))
    o_spec = lambda d: pl.BlockSpec((B, G, C, W), (lambda s: (0, fwd(s), 0, 0)) if d == 0
                                    else (lambda s: (0, bwd(s), 0, 0)))
    o_f, o_b = pl.pallas_call(
        functools.partial(_gdn_rec_kernel, C=C, G=G, B=B), grid=(ng,),
        in_specs=[both(r, w, d) for d in range(2) for (r, w), _ in shapes],
        out_specs=[o_spec(0), o_spec(1)],
        out_shape=[jax.ShapeDtypeStruct((B, n, C, W), F32)] * 2,
        scratch_shapes=[pltpu.VMEM((2, B, GDN_HEADS, GDN_DK, GDN_DV), F32)],
        compiler_params=_cp("arbitrary"), name="gdn_rec",
    )(*mid, *mid)
    return o_f.reshape(B, Tt, W), o_b.reshape(B, Tt, W)


def _mixout_kernel(att_ref, *refs, tm, T, nt, n_o):
    o_refs, refs = refs[:2 * n_o], refs[2 * n_o:]
    (z_ref, u_ref, up_ref, un_ref, x_ref, g1_ref, sh2_ref, sc2_ref,
     gng_ref, wa_ref, wg_ref, wp_ref, wbd_ref, ps_ref, n2g_ref, wr_ref, x1_ref, h2_ref, aff_ref) = refs
    i = pl.program_id(1)
    o = jnp.concatenate([o_refs[k][...] + o_refs[n_o + k][...] for k in range(n_o)], axis=0)
    z = z_ref[...]
    parts = []
    for h in range(GDN_HEADS):
        sl = slice(h * GDN_DV, (h + 1) * GDN_DV)
        oh = o[:, sl]
        oh = oh * lax.rsqrt(jnp.mean(oh * oh, axis=-1, keepdims=True) + EPS) * gng_ref[...]
        parts.append((oh * _silu(z[:, sl])).astype(BF16))
    gdn = jnp.concatenate(parts, axis=1)

    u = u_ref[...]
    halo = SUBLANE
    ext = jnp.concatenate([jnp.where(i > 0, up_ref[...], 0.0), u, jnp.where(i < nt - 1, un_ref[...], 0.0)], axis=0)
    n_ext = tm + 2 * halo
    back = lambda a, s: pltpu.roll(a, s, 0)
    ahead = lambda a, s: pltpu.roll(a, n_ext - s, 0)
    s2 = ext + back(ext, 1)
    s4 = back(s2, 1) + ahead(s2, 1)
    s8 = back(s4, 2) + ahead(s4, 2)
    s16 = back(s8, 4) + ahead(s8, 4)
    t = i * tm + lax.broadcasted_iota(jnp.int32, (tm, 1), 0)
    lane = lax.broadcasted_iota(jnp.int32, (tm, POOL_WIDTH), 1)
    mean = None
    for gi, (win, sw) in reversed(list(enumerate(zip(POOL_WINDOWS, (s2, s4, s8, s16))))):
        lo = jnp.maximum(t - win // 2, 0)
        hi = jnp.minimum(t - win // 2 + win, T)
        m = sw[halo:halo + tm, :] / (hi - lo).astype(F32)
        mean = m if mean is None else jnp.where(lane < (gi + 1) * POOL_GROUP, m, mean)
    yp = jnp.dot((mean - u).astype(BF16), wbd_ref[...], preferred_element_type=F32) * ps_ref[...]

    y = (lax.dot_general(att_ref[...], wa_ref[...], TN, preferred_element_type=F32)
         + jnp.dot(gdn, wg_ref[...], preferred_element_type=F32)
         + jnp.dot(yp.astype(BF16), wp_ref[...], preferred_element_type=F32))
    x1 = x_ref[...] + g1_ref[...] * y
    x1_ref[...] = x1
    h2 = x1 * lax.rsqrt(jnp.mean(x1 * x1, axis=-1, keepdims=True) + EPS) * n2g_ref[...]
    h2 = h2 * (1.0 + sc2_ref[...]) + sh2_ref[...]
    h2_ref[...] = h2.astype(BF16)
    lg = lax.dot_general(wr_ref[...], h2, NT, precision=HI, preferred_element_type=F32)
    e = jnp.exp(lg - jnp.max(lg, axis=0, keepdims=True))
    aff_ref[...] = e / jnp.sum(e, axis=0, keepdims=True)


def mixout(att, o, o_off, z, u, x, mod, wts):
    B, T, D = x.shape
    tm = min(512, T)
    nt = T // tm
    tb = tm // SUBLANE
    nb = T // SUBLANE
    to = math.gcd(tm, o_off) if o_off else tm
    n_o = tm // to
    W = GDN_HEADS * GDN_DV
    cur = lambda n: pl.BlockSpec((None, tm, n), lambda b, i: (b, i, 0))
    odir = [pl.BlockSpec((None, to, W), lambda b, i, k=k: (b, i * n_o + o_off // to + k, 0)) for k in range(n_o)]
    modspec = lambda k: pl.BlockSpec((None, 1, D), lambda b, i: (b, 0, k))
    full = lambda a: pl.BlockSpec(a.shape, lambda b, i: (0,) * a.ndim)
    return pl.pallas_call(
        functools.partial(_mixout_kernel, tm=tm, T=T, nt=nt, n_o=n_o), grid=(B, nt),
        in_specs=[pl.BlockSpec((None, MLA_HEADS * MLA_V, tm), lambda b, i: (b, 0, i))] + odir + odir + [cur(W), cur(POOL_WIDTH),
                  pl.BlockSpec((None, SUBLANE, POOL_WIDTH), lambda b, i: (b, jnp.maximum(i * tb - 1, 0), 0)),
                  pl.BlockSpec((None, SUBLANE, POOL_WIDTH), lambda b, i: (b, jnp.minimum((i + 1) * tb, nb - 1), 0)),
                  cur(D), modspec(2), modspec(3), modspec(4)] + [full(a) for a in wts],
        out_specs=[cur(D), cur(D), pl.BlockSpec((None, N_EXPERTS, tm), lambda b, i: (b, 0, i))],
        out_shape=[jax.ShapeDtypeStruct((B, T, D), F32), jax.ShapeDtypeStruct((B, T, D), BF16),
                   jax.ShapeDtypeStruct((B, N_EXPERTS, T), F32)],
        compiler_params=_cp("parallel", "parallel"), name="mixout",
    )(att, *([o[0]] * n_o), *([o[1]] * n_o), z, u, u, u, x, mod, mod, mod, *wts)


MOE_SUB = 256
MOE_SLOTS = 128


def _route_kernel(aff_ref, gate_ref, slot_ref, starts_ref, *, cap, T):
    aff = aff_ref[...]

    def body(it, res):
        cand = res | jnp.left_shift(jnp.int32(1), 30 - it)
        cnt = jnp.sum((aff >= pltpu.bitcast(cand, F32)).astype(jnp.int32), axis=-1, keepdims=True)
        return jnp.where(cnt >= cap, cand, res)

    bits = lax.fori_loop(0, 31, body, jnp.zeros((N_EXPERTS, 1), jnp.int32))
    thr = pltpu.bitcast(bits, F32)
    above = pltpu.bitcast(bits + 1, F32)
    n_gt = jnp.sum((aff >= above).astype(jnp.int32), axis=-1, keepdims=True)
    need = (cap - n_gt).astype(F32)
    upper = (lax.broadcasted_iota(jnp.int32, (LANE, LANE), 0)
             < lax.broadcasted_iota(jnp.int32, (LANE, LANE), 1)).astype(BF16)
    seen = jnp.zeros((N_EXPERTS, 1), F32)
    taken = jnp.zeros((N_EXPERTS, 1), F32)
    lane = lax.broadcasted_iota(jnp.int32, (N_EXPERTS, LANE), 1)
    starts = jnp.zeros((N_EXPERTS, LANE), jnp.int32)
    per_sub = MOE_SUB // LANE
    for j in range(T // LANE):
        if j % per_sub == 0:
            starts = jnp.where(lane == j // per_sub, taken.astype(jnp.int32), starts)
        sl = slice(j * LANE, (j + 1) * LANE)
        aj = aff[:, sl]
        eq = jnp.where(aj >= thr, jnp.where(aj < above, 1.0, 0.0), 0.0)
        rank = jnp.dot(eq.astype(BF16), upper, preferred_element_type=F32) + seen
        sel = jnp.where(aj >= above, 1.0, jnp.where(rank < need, eq, 0.0))
        gate_ref[:, sl] = sel * aj
        slot = jnp.dot(sel.astype(BF16), upper, preferred_element_type=F32) + taken
        slot_ref[:, sl] = jnp.where(sel > 0.0, slot, -1.0)
        seen = seen + jnp.sum(eq, axis=-1, keepdims=True)
        taken = taken + jnp.sum(sel, axis=-1, keepdims=True)
    starts_ref[...] = jnp.where(lane == T // MOE_SUB, taken.astype(jnp.int32), starts)


def route(aff, cap):
    B, E, T = aff.shape
    assert T % MOE_SUB == 0 and T // MOE_SUB < LANE
    spec = pl.BlockSpec((None, E, T), lambda b: (b, 0, 0))
    return pl.pallas_call(
        functools.partial(_route_kernel, cap=cap, T=T), grid=(B,), in_specs=[spec],
        out_specs=[spec, spec, pl.BlockSpec((None, E, LANE), lambda b: (b, 0, 0))],
        out_shape=[jax.ShapeDtypeStruct(aff.shape, F32), jax.ShapeDtypeStruct(aff.shape, F32),
                   jax.ShapeDtypeStruct((B, E, LANE), jnp.int32)],
        compiler_params=_cp("parallel"), name="route",
    )(aff)


def _slot_blocks(starts_ref, b, e, sub, R):
    s0 = starts_ref[b, e, sub]
    s1 = starts_ref[b, e, sub + 1]
    return s0 // R, (s1 + R - 1) // R


def _moe_ffn_kernel(starts_ref, h_ref, slot_ref, gate_ref, wg_ref, wu_ref, wd_ref, y_ref, xs_ref, gs_ref, *, n_sub, R):
    e, b, j = pl.program_id(0), pl.program_id(1), pl.program_id(2)

    @pl.when(j == 0)
    def _():
        xs_ref[...] = jnp.zeros_like(xs_ref)
        gs_ref[...] = jnp.zeros_like(gs_ref)

    rows = lax.broadcasted_iota(jnp.int32, (R, 1), 0)
    for sub in range(n_sub):
        tsl = slice(sub * MOE_SUB, (sub + 1) * MOE_SUB)
        h = h_ref[tsl, :]
        srow = slot_ref[:, tsl]
        grow = gate_ref[:, tsl]

        def gather(i, carry):
            base = pl.multiple_of(i * R, R)
            match = srow == (base + rows).astype(F32)
            xs_ref[pl.ds(base, R), :] += jnp.dot(jnp.where(match, 1.0, 0.0).astype(BF16), h,
                                                 preferred_element_type=F32)
            gs_ref[pl.ds(base, R), :] += jnp.sum(jnp.where(match, grow, 0.0), axis=1, keepdims=True)
            return carry

        lax.fori_loop(*_slot_blocks(starts_ref, b, e, j * n_sub + sub, R), gather, 0)

    @pl.when(j == pl.num_programs(2) - 1)
    def _():
        xs = xs_ref[...].astype(BF16)
        a = jnp.dot(xs, wg_ref[...].astype(BF16), preferred_element_type=F32)
        hid = (_silu(a) * jnp.dot(xs, wu_ref[...].astype(BF16), preferred_element_type=F32)).astype(BF16)
        y_ref[...] = (jnp.dot(hid, wd_ref[...].astype(BF16), preferred_element_type=F32) * gs_ref[...]).astype(BF16)


def _moe_combine_kernel(starts_ref, y_ref, slot_ref, x1_ref, g2_ref, o_ref, acc_ref, *, n_sub, R, NE, n_blk):
    b, j, eg = pl.program_id(0), pl.program_id(1), pl.program_id(2)

    @pl.when(eg == 0)
    def _():
        acc_ref[...] = jnp.zeros_like(acc_ref)

    lane = lax.broadcasted_iota(jnp.int32, slot_ref.shape, 1)
    slots = slot_ref[...]
    scol = [jnp.sum(jnp.where(lane == eg * NE + k, slots, 0.0), axis=-1, keepdims=True) for k in range(NE)]
    wide = lax.broadcasted_iota(jnp.int32, (1, NE * R), 1)
    which = [wide // R == k for k in range(NE)]
    within = wide % R
    for sub in range(n_sub):
        tsl = slice(sub * MOE_SUB, (sub + 1) * MOE_SUB)
        tok_slot = jnp.zeros((MOE_SUB, NE * R), F32)
        for k in range(NE):
            tok_slot = jnp.where(which[k], scol[k][tsl], tok_slot)
        lo, hi = zip(*[_slot_blocks(starts_ref, b, eg * NE + k, j * n_sub + sub, R) for k in range(NE)])
        trips = functools.reduce(jnp.maximum, [hi[k] - lo[k] for k in range(NE)])

        def scatter(it, carry):
            target = jnp.full((1, NE * R), -2, jnp.int32)
            rows = []
            for k in range(NE):
                blk = jnp.minimum(lo[k] + it, n_blk - 1)
                target = jnp.where(which[k], jnp.where(lo[k] + it < hi[k], blk * R + within, -2), target)
                rows.append(y_ref[k, pl.ds(pl.multiple_of(blk * R, R), R), :])
            onehot = jnp.where(tok_slot == target.astype(F32), 1.0, 0.0).astype(BF16)
            acc_ref[tsl, :] += jnp.dot(onehot, jnp.concatenate(rows, axis=0), preferred_element_type=F32)
            return carry

        lax.fori_loop(0, trips, scatter, 0)

    @pl.when(eg == pl.num_programs(2) - 1)
    def _():
        o_ref[...] = x1_ref[...] + g2_ref[...] * acc_ref[...]


def moe(h2, routed, x1, mod, wg, wu, wd, layer, cap):
    gate, slot, starts = routed
    B, T, D = x1.shape
    E, F = N_EXPERTS, wg.shape[-1]
    R = min(MOE_SLOTS, cap)
    assert cap % R == 0
    tt = min(2048, T)
    n_sub = tt // MOE_SUB
    row = pl.BlockSpec((None, None, 1, tt), lambda e, b, j, st: (b, e, 0, j))
    wspec = lambda r, c: pl.BlockSpec((None, None, r, c), lambda e, b, j, st: (layer, e, 0, 0))
    y = pl.pallas_call(
        functools.partial(_moe_ffn_kernel, n_sub=n_sub, R=R),
        grid_spec=pltpu.PrefetchScalarGridSpec(
            num_scalar_prefetch=1, grid=(E, B, T // tt),
            in_specs=[pl.BlockSpec((None, tt, D), lambda e, b, j, st: (b, j, 0)), row, row,
                      wspec(D, F), wspec(D, F), wspec(F, D)],
            out_specs=pl.BlockSpec((None, None, cap, D), lambda e, b, j, st: (b, e, 0, 0)),
            scratch_shapes=[pltpu.VMEM((cap, D), F32), pltpu.VMEM((cap, 1), F32)]),
        out_shape=jax.ShapeDtypeStruct((B, E, cap, D), BF16),
        compiler_params=_cp("parallel", "parallel", "arbitrary"), name="moe_ffn",
    )(starts, h2, slot.reshape(B, E, 1, T), gate.reshape(B, E, 1, T), wg, wu, wd)
    tc = min(1024, T)
    ne = 4
    rc = min(2 * LANE // ne, cap)
    tok = lambda n: pl.BlockSpec((None, tc, n), lambda b, j, e, st: (b, j, 0))
    return pl.pallas_call(
        functools.partial(_moe_combine_kernel, n_sub=tc // MOE_SUB, R=rc, NE=ne, n_blk=cap // rc),
        grid_spec=pltpu.PrefetchScalarGridSpec(
            num_scalar_prefetch=1, grid=(B, T // tc, E // ne),
            in_specs=[pl.BlockSpec((None, ne, cap, D), lambda b, j, e, st: (b, e, 0, 0)),
                      tok(E), tok(D), pl.BlockSpec((None, 1, D), lambda b, j, e, st: (b, 0, 5))],
            out_specs=tok(D),
            scratch_shapes=[pltpu.VMEM((tc, D), F32)]),
        out_shape=jax.ShapeDtypeStruct((B, T, D), F32),
        compiler_params=_cp("parallel", "parallel", "arbitrary"), name="moe_combine",
    )(starts, y, jnp.swapaxes(slot, 1, 2), x1, mod)


def _in_cols():
    src = np.full((IN_PAD,), -1, np.int64)
    splits = (MLA_Q_LORA, MLA_KV_LORA, MLA_ROPE, 512, 512, 512, 512, 2 * GDN_HEADS, 2 * GDN_HEADS, POOL_WIDTH)
    o = np.concatenate([[0], np.cumsum(splits)])
    put = lambda name, at, lo, n: src.__setitem__(slice(SEG[name][0] + at, SEG[name][0] + at + n), np.arange(lo, lo + n))
    put("pq", 0, o[0], MLA_Q_LORA)
    put("pkv", 0, o[1], MLA_KV_LORA)
    put("pkr", MLA_NOPE, o[2], MLA_ROPE)
    for name, k in (("gq", 3), ("gk", 4), ("gv", 5), ("gz", 6)):
        put(name, 0, o[k], 512)
    for d in range(2):
        put("gab", d * LANE, o[7] + d * GDN_HEADS, GDN_HEADS)
        put("gab", d * LANE + GDN_HEADS, o[8] + d * GDN_HEADS, GDN_HEADS)
    put("pool", 0, o[9], POOL_WIDTH)
    return src


def _take_cols(w, src, axis):
    pieces, p, n = [], 0, len(src)
    while p < n:
        q = p + 1
        while q < n and ((src[q] < 0 and src[p] < 0) or (src[p] >= 0 and src[q] == src[q - 1] + 1)):
            q += 1
        if src[p] < 0:
            shape = list(w.shape)
            shape[axis] = q - p
            pieces.append(jnp.zeros(shape, w.dtype))
        else:
            pieces.append(lax.slice_in_dim(w, int(src[p]), int(src[p]) + q - p, axis=axis))
        p = q
    return jnp.concatenate(pieces, axis=axis)


def _head_pad_src(per_head, lo, n):
    src = np.full((MLA_HEADS * HEAD_PAD,), -1, np.int64)
    for h in range(MLA_HEADS):
        src[h * HEAD_PAD:h * HEAD_PAD + n] = h * per_head + lo + np.arange(n)
    return src


def _rope_tables(T, rotate):
    cos = np.ones((T, LANE), np.float32)
    sa = np.zeros((T, LANE), np.float32)
    sb = np.zeros((T, LANE), np.float32)
    if rotate:
        n_freq = MLA_ROPE // 4
        inv = ROPE_THETA ** (-np.arange(n_freq, dtype=np.float64) / n_freq)
        pos_r = np.repeat(np.arange(T // GRID_W, dtype=np.float64), GRID_W)
        pos_c = np.tile(np.arange(GRID_W, dtype=np.float64), T // GRID_W)
        for base, pos in ((MLA_NOPE, pos_r), (MLA_NOPE + 2 * n_freq, pos_c)):
            ang = pos[:, None] * inv[None, :]
            c, s = np.cos(ang), np.sin(ang)
            cos[:, base:base + n_freq] = c
            cos[:, base + n_freq:base + 2 * n_freq] = c
            sa[:, base:base + n_freq] = -s
            sb[:, base + n_freq:base + 2 * n_freq] = s
    return jnp.asarray(cos), jnp.asarray(sa), jnp.asarray(sb)


def _lane_vec(vals_by_dir, at):
    v = jnp.zeros((2, LANE), F32).at[:, at:at + GDN_HEADS].set(vals_by_dir)
    return v.reshape(1, 2 * LANE)


def kernel(x, c, ctx, c_ctx, ada_w, ada_b, norm1_g, norm2_g, w_in, mla_q_a_norm, mla_w_uq, mla_kv_a_norm, mla_w_ukv, mla_q_norm, mla_k_norm, gdn_conv_w, gdn_a_log, gdn_dt_bias, gdn_norm_g, pool_w, pool_scale, w_out, moe_router, moe_w_gate, moe_w_up, moe_w_down):
    B, T, D = x.shape
    Tc = ctx.shape[1]
    L = ada_w.shape[0]
    cvec = jnp.concatenate([c, c_ctx[None, :], jnp.zeros((SUBLANE - B - 1, D), F32)], axis=0)
    mod = ada_mod(cvec, ada_w, ada_b)
    rope_lat = _rope_tables(T, True)
    rope_ctx = _rope_tables(Tc, False)
    in_src = _in_cols()
    uq_src = _head_pad_src(MLA_QK, 0, MLA_QK)
    uk_src = _head_pad_src(MLA_NOPE + MLA_V, 0, MLA_NOPE)
    uv_src = _head_pad_src(MLA_NOPE + MLA_V, MLA_NOPE, MLA_V)
    pad_to = lambda v, n: jnp.pad(v, (0, n - v.shape[0])).reshape(1, n)

    xc = ctx
    for l in range(L):
        need_ctx = l < L - 1
        mod_lat = mod[l, :B].reshape(B, 1, ADA_CHUNKS * D)
        mod_ctx = jnp.broadcast_to(mod[l, B].reshape(1, 1, ADA_CHUNKS * D), (B, 1, ADA_CHUNKS * D))
        w_in_p = _take_cols(w_in[l], in_src, 1).astype(BF16)
        prep_w = (
            pad_to(mla_q_a_norm[l], 256),
            jnp.pad(_take_cols(mla_w_uq[l], uq_src, 1), ((0, 256 - MLA_Q_LORA), (0, 0))).astype(BF16),
            mla_kv_a_norm[l].reshape(1, MLA_KV_LORA),
            _take_cols(mla_w_ukv[l], uk_src, 1).astype(BF16),
            _take_cols(mla_w_ukv[l], uv_src, 1).T.astype(BF16),
            pad_to(mla_q_norm[l] * (MLA_QK ** -0.5 * math.log2(math.e)), HEAD_PAD),
            pad_to(mla_k_norm[l], HEAD_PAD),
            gdn_conv_w[l],
            _lane_vec(gdn_a_log[l], 0),
            _lane_vec(gdn_dt_bias[l], 0),
        )
        wo = w_out[l]
        n_att = MLA_HEADS * MLA_V
        n_gdn = GDN_HEADS * GDN_DV
        wbd = jnp.zeros((POOL_WIDTH, POOL_WIDTH), F32)
        for gi in range(len(POOL_WINDOWS)):
            wbd = wbd.at[gi * POOL_GROUP:(gi + 1) * POOL_GROUP, gi * POOL_GROUP:(gi + 1) * POOL_GROUP].set(pool_w[l, gi])
        mix_w = (
            gdn_norm_g[l].reshape(1, GDN_DV),
            wo[:n_att].astype(BF16),
            wo[n_att:n_att + n_gdn].astype(BF16),
            wo[n_att + n_gdn:].astype(BF16),
            wbd.astype(BF16),
            pool_scale[l].reshape(1, POOL_WIDTH),
            norm2_g[l].reshape(1, D),
            moe_router[l].T,
        )

        p_lat = dict(zip(SEG, inproj(x, mod_lat, norm1_g[l], w_in_p)))
        p_ctx = dict(zip(SEG, inproj(xc, mod_ctx, norm1_g[l], w_in_p)))
        a_ctx = prep(p_ctx, prep_w, rope_ctx, T, T + Tc)
        a_lat = prep(p_lat, prep_w, rope_lat, 0, T + Tc, shared=a_ctx)
        att_l = attention(a_lat["Q"], a_lat["K"], a_lat["VT"], 0, T + Tc)
        o_all = gdn_scan(a_lat["q"], a_lat["k"], a_lat["v"], a_lat["gb"], Tc)

        def channel_mix(att, o_off, p, xin, m):
            Tn = xin.shape[1]
            x1, h2, aff = mixout(att, o_all, o_off, p["gz"], p["pool"], xin, m, mix_w)
            cap = EC_CAPACITY_FACTOR * Tn // N_EXPERTS
            return moe(h2, route(aff, cap), x1, m, moe_w_gate, moe_w_up, moe_w_down, l, cap)

        x = channel_mix(att_l, 0, p_lat, x, mod_lat)
        if need_ctx:
            att_c = attention(a_ctx["Q"], a_lat["K"], a_lat["VT"], T, Tc)
            xc = channel_mix(att_c, T, p_ctx, xc, mod_ctx)
    return x
```

```python
import functools
import math

import numpy as np
import jax
import jax.numpy as jnp
from jax import lax
from jax.experimental import pallas as pl
from jax.experimental.pallas import tpu as pltpu

F32 = jnp.float32
BF16 = jnp.bfloat16
HI = lax.Precision.HIGHEST

EPS = 1e-6
GRID_W = 64
ADA_CHUNKS = 6
MLA_HEADS = 4
MLA_NOPE = 64
MLA_ROPE = 32
MLA_QK = MLA_NOPE + MLA_ROPE
MLA_V = 64
MLA_Q_LORA = 192
MLA_KV_LORA = 128
ROPE_THETA = 10000.0
GDN_HEADS = 4
GDN_DK = 128
GDN_DV = 128
GDN_CHUNK = 64
POOL_WINDOWS = (2, 4, 8, 16)
POOL_GROUP = 64
POOL_WIDTH = POOL_GROUP * len(POOL_WINDOWS)
N_EXPERTS = 16
EC_CAPACITY_FACTOR = 2

LANE = 128
SUBLANE = 8
HEAD_PAD = 128
VMEM_LIMIT = 48 * 1024 * 1024

NT = (((1,), (1,)), ((), ()))
TN = (((0,), (0,)), ((), ()))

SEG = {}
_off = 0
for _name, _w in (("pq", 256), ("pkv", 128), ("pkr", 128), ("gq", 512), ("gk", 512), ("gv", 512),
                  ("gz", 512), ("gab", 256), ("pool", 256)):
    SEG[_name] = (_off, _w)
    _off += _w
IN_PAD = _off


def _cp(*dims):
    return pltpu.CompilerParams(dimension_semantics=dims, vmem_limit_bytes=VMEM_LIMIT)


def _silu(v):
    return v / (1.0 + jnp.exp(-v))


def _ada_kernel(c_ref, w_ref, b_ref, o_ref):
    s = _silu(c_ref[...])
    o_ref[...] = jnp.dot(s, w_ref[...], precision=HI, preferred_element_type=F32) + b_ref[...]


def ada_mod(cvec, ada_w, ada_b):
    L, D, N = ada_w.shape
    tn = N // 4
    return pl.pallas_call(
        _ada_kernel, grid=(L, N // tn),
        in_specs=[pl.BlockSpec((SUBLANE, D), lambda l, j: (0, 0)),
                  pl.BlockSpec((None, D, tn), lambda l, j: (l, 0, j)),
                  pl.BlockSpec((None, 1, tn), lambda l, j: (l, 0, j))],
        out_specs=pl.BlockSpec((None, SUBLANE, tn), lambda l, j: (l, 0, j)),
        out_shape=jax.ShapeDtypeStruct((L, SUBLANE, N), F32),
        compiler_params=_cp("parallel", "parallel"), name="ada_mod",
    )(cvec, ada_w, ada_b.reshape(L, 1, N))


def _inproj_kernel(x_ref, sh_ref, sc_ref, g_ref, w_ref, *out_refs):
    x = x_ref[...]
    h = x * lax.rsqrt(jnp.mean(x * x, axis=-1, keepdims=True) + EPS) * g_ref[...]
    hb = (h * (1.0 + sc_ref[...]) + sh_ref[...]).astype(BF16)
    for (off, n), o_ref in zip(SEG.values(), out_refs):
        o_ref[...] = jnp.dot(hb, w_ref[:, off:off + n], preferred_element_type=F32)


def inproj(x, mod, norm_g, w_in_p):
    B, T, D = x.shape
    tm = min(512, T)
    modspec = lambda k: pl.BlockSpec((None, 1, D), lambda b, i, k=k: (b, 0, k))
    return pl.pallas_call(
        _inproj_kernel, grid=(B, T // tm),
        in_specs=[pl.BlockSpec((None, tm, D), lambda b, i: (b, i, 0)), modspec(0), modspec(1),
                  pl.BlockSpec((1, D), lambda b, i: (0, 0)),
                  pl.BlockSpec((D, IN_PAD), lambda b, i: (0, 0))],
        out_specs=[pl.BlockSpec((None, tm, n), lambda b, i: (b, i, 0)) for _, n in SEG.values()],
        out_shape=[jax.ShapeDtypeStruct((B, T, n), F32) for _, n in SEG.values()],
        compiler_params=_cp("parallel", "parallel"), name="inproj",
    )(x, mod, mod, norm_g.reshape(1, D), w_in_p)


def _prep_kernel(*refs, nt, tm):
    (pq_ref, pkv_ref, pkr_ref, gq_ref, gk_ref, gv_ref, gqp_ref, gkp_ref, gvp_ref,
     gqn_ref, gkn_ref, gvn_ref, gab_ref, qan_ref, wuq_ref, kvan_ref, wuk_ref, wuv_ref,
     qn_ref, kn_ref, cos_ref, sa_ref, sb_ref, cw_ref, alog_ref, dt_ref) = refs[:26]
    Q_ref, K_ref, VT_ref, q_ref, k_ref, v_ref, gb_ref = refs[-7:]
    i = pl.program_id(1)
    cos, sa, sb = cos_ref[...], sa_ref[...], sb_ref[...]

    def rope(xh):
        return xh * cos + pltpu.roll(xh, LANE - 8, 1) * sa + pltpu.roll(xh, 8, 1) * sb

    pq = pq_ref[...]
    qa = pq * lax.rsqrt(jnp.sum(pq * pq, axis=-1, keepdims=True) * (1.0 / MLA_Q_LORA) + EPS) * qan_ref[...]
    qall = jnp.dot(qa.astype(BF16), wuq_ref[...], preferred_element_type=F32)
    pkv = pkv_ref[...]
    kva = (pkv * lax.rsqrt(jnp.mean(pkv * pkv, axis=-1, keepdims=True) + EPS) * kvan_ref[...]).astype(BF16)
    kall = jnp.dot(kva, wuk_ref[...], preferred_element_type=F32)
    vt = lax.dot_general(wuv_ref[...], kva, NT, preferred_element_type=F32)
    ones_row = lax.broadcasted_iota(jnp.int32, vt.shape, 0) % HEAD_PAD == MLA_V
    VT_ref[...] = jnp.where(ones_row, 1.0, vt).astype(BF16)
    pkr = pkr_ref[...]
    for h in range(MLA_HEADS):
        sl = slice(h * HEAD_PAD, (h + 1) * HEAD_PAD)
        qh = qall[:, sl]
        qh = qh * lax.rsqrt(jnp.sum(qh * qh, axis=-1, keepdims=True) * (1.0 / MLA_QK) + EPS) * qn_ref[...]
        Q_ref[:, sl] = rope(qh).astype(BF16)
        kh = kall[:, sl] + pkr
        kh = kh * lax.rsqrt(jnp.sum(kh * kh, axis=-1, keepdims=True) * (1.0 / MLA_QK) + EPS) * kn_ref[...]
        K_ref[:, sl] = rope(kh).astype(BF16)

    rid = lax.broadcasted_iota(jnp.int32, (SUBLANE, GDN_HEADS * GDN_DK), 0)

    def conv_silu(u_ref, up_ref, un_ref, c0):
        u = u_ref[...]
        n = u.shape[1]
        prev_row = jnp.where(i > 0, up_ref[SUBLANE - 1:SUBLANE, :], 0.0)
        next_row = jnp.where(i < nt - 1, un_ref[0:1, :], 0.0)
        um = pltpu.roll(u, 1, 0)
        um = jnp.concatenate([jnp.where(rid == 0, prev_row, um[:SUBLANE]), um[SUBLANE:]], axis=0)
        up = pltpu.roll(u, tm - 1, 0)
        up = jnp.concatenate([up[:tm - SUBLANE], jnp.where(rid == SUBLANE - 1, next_row, up[tm - SUBLANE:])], axis=0)
        y = um * cw_ref[0:1, c0:c0 + n] + u * cw_ref[1:2, c0:c0 + n] + up * cw_ref[2:3, c0:c0 + n]
        return _silu(y)

    cq = conv_silu(gq_ref, gqp_ref, gqn_ref, 0)
    ck = conv_silu(gk_ref, gkp_ref, gkn_ref, GDN_HEADS * GDN_DK)
    v_ref[...] = conv_silu(gv_ref, gvp_ref, gvn_ref, 2 * GDN_HEADS * GDN_DK)
    for h in range(GDN_HEADS):
        sl = slice(h * GDN_DK, (h + 1) * GDN_DK)
        qh = cq[:, sl]
        q_ref[:, sl] = qh * lax.rsqrt(jnp.sum(qh * qh, axis=-1, keepdims=True) + EPS) * (GDN_DK ** -0.5)
        kh = ck[:, sl]
        k_ref[:, sl] = kh * lax.rsqrt(jnp.sum(kh * kh, axis=-1, keepdims=True) + EPS)

    pre = gab_ref[...]
    lane = lax.broadcasted_iota(jnp.int32, pre.shape, 1) % LANE
    sp_in = pre + dt_ref[...]
    softplus = jnp.maximum(sp_in, 0.0) + jnp.log(1.0 + jnp.exp(-jnp.abs(sp_in)))
    g = -jnp.exp(alog_ref[...]) * softplus
    beta = 1.0 / (1.0 + jnp.exp(-pre))
    gb_ref[...] = jnp.where(lane < GDN_HEADS, g, jnp.where(lane < 2 * GDN_HEADS, beta, 0.0))


def prep(p, wts, rope_tabs, row_off, t_all, shared=None):
    pq, pkv, pkr, gq, gk, gv, gab = (p[k] for k in ("pq", "pkv", "pkr", "gq", "gk", "gv", "gab"))
    B, T, _ = pq.shape
    tm = min(256, T)
    assert row_off % tm == 0
    nt = T // tm
    tb = tm // SUBLANE
    nb = T // SUBLANE
    ro = row_off // tm
    cur = lambda n: pl.BlockSpec((None, tm, n), lambda b, i: (b, i, 0))
    dst = lambda n: pl.BlockSpec((None, tm, n), lambda b, i: (b, i + ro, 0))
    prv = lambda n: pl.BlockSpec((None, SUBLANE, n), lambda b, i: (b, jnp.maximum(i * tb - 1, 0), 0))
    nxt = lambda n: pl.BlockSpec((None, SUBLANE, n), lambda b, i: (b, jnp.minimum((i + 1) * tb, nb - 1), 0))
    full = lambda a: pl.BlockSpec(a.shape, lambda b, i: (0,) * a.ndim)
    tab = pl.BlockSpec((tm, LANE), lambda b, i: (i, 0))
    W = GDN_HEADS * GDN_DK
    outs = [("Q", MLA_HEADS * HEAD_PAD, BF16), ("K", MLA_HEADS * HEAD_PAD, BF16), ("VT", None, BF16),
            ("q", W, F32), ("k", W, F32), ("v", W, F32), ("gb", 2 * LANE, F32)]
    HP = MLA_HEADS * HEAD_PAD
    shared = [] if shared is None else [shared[n] for n, _, _ in outs[1:]]
    n_in = 26
    res = pl.pallas_call(
        functools.partial(_prep_kernel, nt=nt, tm=tm), grid=(B, nt),
        in_specs=[cur(256), cur(128), cur(128), cur(W), cur(W), cur(W), prv(W), prv(W), prv(W),
                  nxt(W), nxt(W), nxt(W), cur(256)] + [full(a) for a in wts[:7]] + [tab, tab, tab]
                 + [full(a) for a in wts[7:]] + [pl.BlockSpec(memory_space=pl.ANY)] * len(shared),
        out_specs=[cur(outs[0][1])] + [dst(n) if n else pl.BlockSpec((None, HP, tm), lambda b, i: (b, 0, i + ro))
                                       for _, n, _ in outs[1:]],
        out_shape=[jax.ShapeDtypeStruct((B, T, outs[0][1]), outs[0][2])]
                  + [jax.ShapeDtypeStruct((B, t_all, n) if n else (B, HP, t_all), dt) for _, n, dt in outs[1:]],
        input_output_aliases={n_in + k: 1 + k for k in range(len(shared))},
        compiler_params=_cp("parallel", "parallel"), name="prep",
    )(pq, pkv, pkr, gq, gk, gv, gq, gk, gv, gq, gk, gv, gab, *wts[:7], *rope_tabs, *wts[7:], *shared)
    return dict(zip([n for n, _, _ in outs], res))


def _attn_kernel(q_ref, k_ref, vt_ref, o_ref, sa_ref, sb_ref, *, ck, nk):
    q = q_ref[...]
    tq = q.shape[0]

    def scores(j):
        return lax.dot_general(k_ref[j * ck:(j + 1) * ck, :], q, NT, preferred_element_type=F32)

    nv = MLA_V + 16

    def update(carry, s_ref, j):
        m, acc = carry
        s = s_ref[...]
        m_new = jnp.maximum(m, jnp.max(s, axis=0, keepdims=True))
        p = jnp.exp2(s - m_new).astype(BF16)
        acc = jnp.exp2(m - m_new) * acc + jnp.dot(vt_ref[0:nv, j * ck:(j + 1) * ck], p, preferred_element_type=F32)
        return m_new, acc

    bufs = (sa_ref, sb_ref)
    bufs[0][...] = scores(0)
    carry = (jnp.full((1, tq), -1e30, F32), jnp.zeros((nv, tq), F32))
    for j in range(nk):
        if j + 1 < nk:
            bufs[(j + 1) % 2][...] = scores(j + 1)
        carry = update(carry, bufs[j % 2], j)
    acc = carry[1]
    o = acc[:MLA_V] / acc[MLA_V:MLA_V + 1]
    o_ref[...] = o.astype(o_ref.dtype)


def attention(Q, K, VT, k_off, Tk):
    B, Tq, _ = Q.shape
    assert k_off % Tk == 0
    kb = k_off // Tk
    tq = min(1024, Tq)
    ck = next(c for c in (384, 256, 128) if Tk % c == 0)
    qo = pl.BlockSpec((None, tq, HEAD_PAD), lambda b, h, i: (b, i, h))
    return pl.pallas_call(
        functools.partial(_attn_kernel, ck=ck, nk=Tk // ck), grid=(B, MLA_HEADS, Tq // tq),
        in_specs=[qo, pl.BlockSpec((None, Tk, HEAD_PAD), lambda b, h, i: (b, kb, h)),
                  pl.BlockSpec((None, HEAD_PAD, Tk), lambda b, h, i: (b, h, kb))],
        out_specs=pl.BlockSpec((None, MLA_V, tq), lambda b, h, i: (b, h, i)),
        out_shape=jax.ShapeDtypeStruct((B, MLA_HEADS * MLA_V, Tq), BF16),
        scratch_shapes=[pltpu.VMEM((ck, tq), F32), pltpu.VMEM((ck, tq), F32)],
        compiler_params=_cp("parallel", "parallel", "parallel"), name="attention",
    )(Q, K, VT)


GDN_GROUP = 4


def _gdn_prep_kernel(q_ref, k_ref, v_ref, gb_ref, wq_ref, u_ref, qk_ref, kdt_ref, egl_ref, *, C, G):
    H, DK = GDN_HEADS, GDN_DK
    fwd = pl.program_id(1) == 0
    dot = functools.partial(jnp.dot, preferred_element_type=F32)
    row = lax.broadcasted_iota(jnp.int32, (C, H * C), 0)
    lane = lax.broadcasted_iota(jnp.int32, (C, H * C), 1)
    col = lane & (C - 1)
    hmask = [(lane >> int(math.log2(C))) == h for h in range(H)]
    wide = lax.broadcasted_iota(jnp.int32, (C, H * DK), 1)
    kmask = [(wide >> int(math.log2(DK))) == h for h in range(H)]
    ahead = jnp.where(fwd, row - col, col - row)
    incl = ahead >= 0
    strict = ahead > 0
    eye = (row == col).astype(F32)
    r1 = lax.broadcasted_iota(jnp.int32, (C, C), 0)
    c1 = lax.broadcasted_iota(jnp.int32, (C, C), 1)
    incl16 = (jnp.where(fwd, r1 - c1, c1 - r1) >= 0).astype(F32).astype(BF16)
    eye16 = (r1 == c1).astype(F32).astype(BF16)

    def blockdiag(m, masks):
        return jnp.concatenate([jnp.where(mk, m, 0.0) for mk in masks], axis=0).astype(BF16)

    def per_head(cols, width):
        n = cols.shape[0]
        if width == LANE:
            return jnp.concatenate([jnp.broadcast_to(cols[:, h:h + 1], (n, LANE)) for h in range(H)], axis=1)
        low = lax.broadcasted_iota(jnp.int32, (n, LANE), 1) < width
        return jnp.concatenate([jnp.where(low, cols[:, h:h + 1], cols[:, h + 1:h + 2]) for h in range(0, H, 2)], axis=1)

    def terms(v):
        hi = v.astype(BF16)
        rest = v - hi.astype(F32)
        mid = rest.astype(BF16)
        return jnp.concatenate([hi, mid, (rest - mid.astype(F32)).astype(BF16)], axis=1)

    fold = lambda a, axis: sum(jnp.split(a, 3, axis=axis)[1:], jnp.split(a, 3, axis=axis)[0])
    chunks = range(G)
    rows = [slice(g * C, (g + 1) * C) for g in chunks]
    gb = [gb_ref[r, :] for r in rows]
    gterms = [terms(v) for v in gb]
    gc = [fold(dot(incl16, t), 1) for t in gterms]
    gct = [fold(lax.dot_general(t, incl16, (((0,), (1,)), ((), ())), preferred_element_type=F32), 0)
           for t in gterms]
    glast = [jnp.where(fwd, v[C - 1:C, :], v[0:1, :]) for v in gc]
    k16 = [k_ref[r, :].astype(BF16) for r in rows]
    qkk = [lax.dot_general(jnp.concatenate([q_ref[rows[g], :].astype(BF16), k16[g]], axis=0),
                           blockdiag(k_ref[rows[g], :], kmask), NT, preferred_element_type=F32)
           for g in chunks]
    a, tinv = [], []
    same = lambda s: (row >> s) == (col >> s)
    pairs = jnp.where(same(1), 1.0, 0.0)
    for g in chunks:
        egl_ref[g] = jnp.broadcast_to(jnp.exp(glast[g]), (SUBLANE, LANE))
        grow = jnp.concatenate([gct[g][h:h + 1, :] for h in range(H)], axis=1)
        decay = jnp.exp(jnp.where(incl, per_head(gc[g], C) - grow, -1e30))
        qk_ref[g] = (qkk[g][:C] * decay).astype(BF16)
        a.append(jnp.where(strict, qkk[g][C:] * decay, 0.0) * per_head(gb[g][:, H:], C))
        tinv.append(eye - a[g] * pairs)
    for s in range(1, int(math.log2(C))):
        join = jnp.where(same(s + 1), jnp.where(same(s), 0.0, 1.0), 0.0)
        x16 = [tinv[g].astype(BF16) for g in chunks]
        xl = [dot(x16[g], blockdiag(a[g] * join, hmask)) for g in chunks]
        xlx = [dot(xl[g].astype(BF16), blockdiag(tinv[g], hmask)) for g in chunks]
        tinv = [tinv[g] - xlx[g] for g in chunks]
    gcw = [per_head(gc[g], DK) for g in chunks]
    bw = [per_head(gb[g][:, H:], DK) for g in chunks]
    wu = [dot(tinv[g].astype(BF16),
              jnp.concatenate([blockdiag(k_ref[rows[g], :] * (bw[g] * jnp.exp(gcw[g])), kmask),
                               blockdiag(v_ref[rows[g], :] * bw[g], kmask)], axis=1)) for g in chunks]
    kdt = [lax.dot_general((k_ref[rows[g], :] * jnp.exp(per_head(glast[g], DK) - gcw[g])).astype(BF16), eye16,
                           TN, preferred_element_type=F32) for g in chunks]
    for g in chunks:
        wq_ref[g, 0:C, :] = wu[g][:, :H * DK].astype(BF16)
        wq_ref[g, C:2 * C, :] = (q_ref[rows[g], :] * jnp.exp(gcw[g])).astype(BF16)
        u_ref[g] = wu[g][:, H * DK:]
        for h in range(H):
            kdt_ref[g, :, h * C:(h + 1) * C] = kdt[g][h * DK:(h + 1) * DK].astype(BF16)


def _gdn_rec_kernel(*refs, C, G, B):
    ins = (refs[0:5], refs[5:10])
    outs = refs[10:12]
    s_ref = refs[12]
    dot = functools.partial(jnp.dot, preferred_element_type=F32)

    @pl.when(pl.program_id(0) == 0)
    def _():
        s_ref[...] = jnp.zeros_like(s_ref)

    hsl = lambda h: slice(h * GDN_DK, (h + 1) * GDN_DK)
    csl = lambda h: slice(h * C, (h + 1) * C)
    for step in range(G):
        chains = [(d, b, h, step if d == 0 else G - 1 - step)
                  for d in range(2) for b in range(B) for h in range(GDN_HEADS)]
        S = {c: s_ref[c[0], c[1], c[2]] for c in chains}
        r = {(d, b, h, g): dot(ins[d][0][b, g, :, hsl(h)], S[(d, b, h, g)].astype(BF16))
             for (d, b, h, g) in chains}
        vn = {(d, b, h, g): (ins[d][1][b, g, :, hsl(h)] - r[(d, b, h, g)][:C]).astype(BF16) for (d, b, h, g) in chains}
        ou = {(d, b, h, g): dot(jnp.concatenate([ins[d][2][b, g, :, csl(h)], ins[d][3][b, g, :, csl(h)]], axis=0),
                                vn[(d, b, h, g)]) for (d, b, h, g) in chains}
        for c in chains:
            d, b, h, g = c
            outs[d][b, g, :, hsl(h)] = r[c][C:] + ou[c][:C]
            s_ref[d, b, h] = S[c] * ins[d][4][b, g, 0:1, h:h + 1] + ou[c][C:]


def gdn_scan(q, k, v, gb, n_ctx):
    B, Tt, W = q.shape
    C, G = GDN_CHUNK, GDN_GROUP
    n = Tt // C
    ng = n // G
    ncg = n_ctx // (C * G)
    assert n % G == 0 and n_ctx % (C * G) == 0
    gp = next(c for c in (12, 8, 6, 4, 3, 2, 1) if n % c == 0)
    tok = pl.BlockSpec((None, gp * C, W), lambda b, d, s: (b, s, 0))
    shapes = [((2 * C, W), BF16), ((C, W), F32), ((C, GDN_HEADS * C), BF16), ((GDN_DK, GDN_HEADS * C), BF16),
              ((SUBLANE, LANE), F32)]
    nat = lambda b, d, s: (b, d, s, 0, 0)
    mid = pl.pallas_call(
        functools.partial(_gdn_prep_kernel, C=C, G=gp), grid=(B, 2, n // gp),
        in_specs=[tok, tok, tok, pl.BlockSpec((None, gp * C, LANE), lambda b, d, s: (b, s, d))],
        out_specs=[pl.BlockSpec((None, None, gp, r, w), nat) for (r, w), _ in shapes],
        out_shape=[jax.ShapeDtypeStruct((B, 2, n, r, w), dt) for (r, w), dt in shapes],
        compiler_params=_cp("parallel", "parallel", "parallel"), name="gdn_prep",
    )(q, k, v, gb)

    fwd = lambda s: jnp.where(s < ncg, ng - ncg + s, s - ncg)
    bwd = lambda s: ng - 1 - s
    both = lambda r, w, d: pl.BlockSpec((B, None, G, r, w), (lambda s: (0, 0, fwd(s), 0, 0)) if d == 0
                                        else (lambda s: (0, 1, bwd(s), 0, 0)))
    o_spec = lambda d: pl.BlockSpec((B, G, C, W), (lambda s: (0, fwd(s), 0, 0)) if d == 0
                                    else (lambda s: (0, bwd(s), 0, 0)))
    o_f, o_b = pl.pallas_call(
        functools.partial(_gdn_rec_kernel, C=C, G=G, B=B), grid=(ng,),
        in_specs=[both(r, w, d) for d in range(2) for (r, w), _ in shapes],
        out_specs=[o_spec(0), o_spec(1)],
        out_shape=[jax.ShapeDtypeStruct((B, n, C, W), F32)] * 2,
        scratch_shapes=[pltpu.VMEM((2, B, GDN_HEADS, GDN_DK, GDN_DV), F32)],
        compiler_params=_cp("arbitrary"), name="gdn_rec",
    )(*mid, *mid)
    return o_f.reshape(B, Tt, W), o_b.reshape(B, Tt, W)


def _mixout_kernel(att_ref, *refs, tm, T, nt, n_o):
    o_refs, refs = refs[:2 * n_o], refs[2 * n_o:]
    (z_ref, u_ref, up_ref, un_ref, x_ref, g1_ref, sh2_ref, sc2_ref,
     gng_ref, wa_ref, wg_ref, wp_ref, wbd_ref, ps_ref, n2g_ref, wr_ref, x1_ref, h2_ref, aff_ref) = refs
    i = pl.program_id(1)
    o = jnp.concatenate([o_refs[k][...] + o_refs[n_o + k][...] for k in range(n_o)], axis=0)
    z = z_ref[...]
    parts = []
    for h in range(GDN_HEADS):
        sl = slice(h * GDN_DV, (h + 1) * GDN_DV)
        oh = o[:, sl]
        oh = oh * lax.rsqrt(jnp.mean(oh * oh, axis=-1, keepdims=True) + EPS) * gng_ref[...]
        parts.append((oh * _silu(z[:, sl])).astype(BF16))
    gdn = jnp.concatenate(parts, axis=1)

    u = u_ref[...]
    halo = SUBLANE
    ext = jnp.concatenate([jnp.where(i > 0, up_ref[...], 0.0), u, jnp.where(i < nt - 1, un_ref[...], 0.0)], axis=0)
    n_ext = tm + 2 * halo
    back = lambda a, s: pltpu.roll(a, s, 0)
    ahead = lambda a, s: pltpu.roll(a, n_ext - s, 0)
    s2 = ext + back(ext, 1)
    s4 = back(s2, 1) + ahead(s2, 1)
    s8 = back(s4, 2) + ahead(s4, 2)
    s16 = back(s8, 4) + ahead(s8, 4)
    t = i * tm + lax.broadcasted_iota(jnp.int32, (tm, 1), 0)
    lane = lax.broadcasted_iota(jnp.int32, (tm, POOL_WIDTH), 1)
    mean = None
    for gi, (win, sw) in reversed(list(enumerate(zip(POOL_WINDOWS, (s2, s4, s8, s16))))):
        lo = jnp.maximum(t - win // 2, 0)
        hi = jnp.minimum(t - win // 2 + win, T)
        m = sw[halo:halo + tm, :] / (hi - lo).astype(F32)
        mean = m if mean is None else jnp.where(lane < (gi + 1) * POOL_GROUP, m, mean)
    yp = jnp.dot((mean - u).astype(BF16), wbd_ref[...], preferred_element_type=F32) * ps_ref[...]

    y = (lax.dot_general(att_ref[...], wa_ref[...], TN, preferred_element_type=F32)
         + jnp.dot(gdn, wg_ref[...], preferred_element_type=F32)
         + jnp.dot(yp.astype(BF16), wp_ref[...], preferred_element_type=F32))
    x1 = x_ref[...] + g1_ref[...] * y
    x1_ref[...] = x1
    h2 = x1 * lax.rsqrt(jnp.mean(x1 * x1, axis=-1, keepdims=True) + EPS) * n2g_ref[...]
    h2 = h2 * (1.0 + sc2_ref[...]) + sh2_ref[...]
    h_hi = h2.astype(BF16)
    h2_ref[...] = h_hi
    h_lo = (h2 - h_hi.astype(F32)).astype(BF16)
    wr = wr_ref[...]
    w_hi = wr.astype(BF16)
    w_lo = (wr - w_hi.astype(F32)).astype(BF16)
    ntdot = lambda a, c: lax.dot_general(a, c, NT, preferred_element_type=F32)
    lg = ntdot(w_hi, h_hi) + (ntdot(w_hi, h_lo) + ntdot(w_lo, h_hi))
    e = jnp.exp(lg - jnp.max(lg, axis=0, keepdims=True))
    aff_ref[...] = e / jnp.sum(e, axis=0, keepdims=True)


def mixout(att, o, o_off, z, u, x, mod, wts):
    B, T, D = x.shape
    tm = min(512, T)
    nt = T // tm
    tb = tm // SUBLANE
    nb = T // SUBLANE
    to = math.gcd(tm, o_off) if o_off else tm
    n_o = tm // to
    W = GDN_HEADS * GDN_DV
    cur = lambda n: pl.BlockSpec((None, tm, n), lambda b, i: (b, i, 0))
    odir = [pl.BlockSpec((None, to, W), lambda b, i, k=k: (b, i * n_o + o_off // to + k, 0)) for k in range(n_o)]
    modspec = lambda k: pl.BlockSpec((None, 1, D), lambda b, i: (b, 0, k))
    full = lambda a: pl.BlockSpec(a.shape, lambda b, i: (0,) * a.ndim)
    return pl.pallas_call(
        functools.partial(_mixout_kernel, tm=tm, T=T, nt=nt, n_o=n_o), grid=(B, nt),
        in_specs=[pl.BlockSpec((None, MLA_HEADS * MLA_V, tm), lambda b, i: (b, 0, i))] + odir + odir + [cur(W), cur(POOL_WIDTH),
                  pl.BlockSpec((None, SUBLANE, POOL_WIDTH), lambda b, i: (b, jnp.maximum(i * tb - 1, 0), 0)),
                  pl.BlockSpec((None, SUBLANE, POOL_WIDTH), lambda b, i: (b, jnp.minimum((i + 1) * tb, nb - 1), 0)),
                  cur(D), modspec(2), modspec(3), modspec(4)] + [full(a) for a in wts],
        out_specs=[cur(D), cur(D), pl.BlockSpec((None, N_EXPERTS, tm), lambda b, i: (b, 0, i))],
        out_shape=[jax.ShapeDtypeStruct((B, T, D), F32), jax.ShapeDtypeStruct((B, T, D), BF16),
                   jax.ShapeDtypeStruct((B, N_EXPERTS, T), F32)],
        compiler_params=_cp("parallel", "parallel"), name="mixout",
    )(att, *([o[0]] * n_o), *([o[1]] * n_o), z, u, u, u, x, mod, mod, mod, *wts)


MOE_SUB = 256
MOE_SLOTS = 128


def _route_kernel(aff_ref, gate_ref, slot_ref, starts_ref, *, cap, T):
    aff = aff_ref[...]

    def body(it, res):
        cand = res | jnp.left_shift(jnp.int32(1), 30 - it)
        cnt = jnp.sum((aff >= pltpu.bitcast(cand, F32)).astype(jnp.int32), axis=-1, keepdims=True)
        return jnp.where(cnt >= cap, cand, res)

    bits = lax.fori_loop(0, 31, body, jnp.zeros((N_EXPERTS, 1), jnp.int32))
    thr = pltpu.bitcast(bits, F32)
    above = pltpu.bitcast(bits + 1, F32)
    n_gt = jnp.sum((aff >= above).astype(jnp.int32), axis=-1, keepdims=True)
    need = (cap - n_gt).astype(F32)
    upper = (lax.broadcasted_iota(jnp.int32, (LANE, LANE), 0)
             < lax.broadcasted_iota(jnp.int32, (LANE, LANE), 1)).astype(BF16)
    seen = jnp.zeros((N_EXPERTS, 1), F32)
    taken = jnp.zeros((N_EXPERTS, 1), F32)
    lane = lax.broadcasted_iota(jnp.int32, (N_EXPERTS, LANE), 1)
    starts = jnp.zeros((N_EXPERTS, LANE), jnp.int32)
    per_sub = MOE_SUB // LANE
    for j in range(T // LANE):
        if j % per_sub == 0:
            starts = jnp.where(lane == j // per_sub, taken.astype(jnp.int32), starts)
        sl = slice(j * LANE, (j + 1) * LANE)
        aj = aff[:, sl]
        eq = jnp.where(aj >= thr, jnp.where(aj < above, 1.0, 0.0), 0.0)
        rank = jnp.dot(eq.astype(BF16), upper, preferred_element_type=F32) + seen
        sel = jnp.where(aj >= above, 1.0, jnp.where(rank < need, eq, 0.0))
        gate_ref[:, sl] = sel * aj
        slot = jnp.dot(sel.astype(BF16), upper, preferred_element_type=F32) + taken
        slot_ref[:, sl] = jnp.where(sel > 0.0, slot, -1.0)
        seen = seen + jnp.sum(eq, axis=-1, keepdims=True)
        taken = taken + jnp.sum(sel, axis=-1, keepdims=True)
    starts_ref[...] = jnp.where(lane == T // MOE_SUB, taken.astype(jnp.int32), starts)


def route(aff, cap):
    B, E, T = aff.shape
    assert T % MOE_SUB == 0 and T // MOE_SUB < LANE
    spec = pl.BlockSpec((None, E, T), lambda b: (b, 0, 0))
    return pl.pallas_call(
        functools.partial(_route_kernel, cap=cap, T=T), grid=(B,), in_specs=[spec],
        out_specs=[spec, spec, pl.BlockSpec((None, E, LANE), lambda b: (b, 0, 0))],
        out_shape=[jax.ShapeDtypeStruct(aff.shape, F32), jax.ShapeDtypeStruct(aff.shape, F32),
                   jax.ShapeDtypeStruct((B, E, LANE), jnp.int32)],
        compiler_params=_cp("parallel"), name="route",
    )(aff)


def _slot_blocks(starts_ref, b, e, sub, R):
    s0 = starts_ref[b, e, sub]
    s1 = starts_ref[b, e, sub + 1]
    return s0 // R, (s1 + R - 1) // R


def _moe_ffn_kernel(starts_ref, h_ref, slot_ref, gate_ref, wg_ref, wu_ref, wd_ref, y_ref, xs_ref, gs_ref, *, n_sub, R):
    e, b, j = pl.program_id(0), pl.program_id(1), pl.program_id(2)

    @pl.when(j == 0)
    def _():
        xs_ref[...] = jnp.zeros_like(xs_ref)
        gs_ref[...] = jnp.zeros_like(gs_ref)

    rows = lax.broadcasted_iota(jnp.int32, (R, 1), 0)
    for sub in range(n_sub):
        tsl = slice(sub * MOE_SUB, (sub + 1) * MOE_SUB)
        h = h_ref[tsl, :]
        srow = slot_ref[:, tsl]
        grow = gate_ref[:, tsl]

        def gather(i, carry):
            base = pl.multiple_of(i * R, R)
            match = srow == (base + rows).astype(F32)
            xs_ref[pl.ds(base, R), :] += jnp.dot(jnp.where(match, 1.0, 0.0).astype(BF16), h,
                                                 preferred_element_type=F32)
            gs_ref[pl.ds(base, R), :] += jnp.sum(jnp.where(match, grow, 0.0), axis=1, keepdims=True)
            return carry

        lax.fori_loop(*_slot_blocks(starts_ref, b, e, j * n_sub + sub, R), gather, 0)

    @pl.when(j == pl.num_programs(2) - 1)
    def _():
        xs = xs_ref[...].astype(BF16)
        a = jnp.dot(xs, wg_ref[...].astype(BF16), preferred_element_type=F32)
        hid = (_silu(a) * jnp.dot(xs, wu_ref[...].astype(BF16), preferred_element_type=F32)).astype(BF16)
        y_ref[...] = (jnp.dot(hid, wd_ref[...].astype(BF16), preferred_element_type=F32) * gs_ref[...]).astype(BF16)


def _moe_combine_kernel(starts_ref, y_ref, slot_ref, x1_ref, g2_ref, o_ref, acc_ref, *, n_sub, R, NE, n_blk):
    b, j, eg = pl.program_id(0), pl.program_id(1), pl.program_id(2)

    @pl.when(eg == 0)
    def _():
        acc_ref[...] = jnp.zeros_like(acc_ref)

    lane = lax.broadcasted_iota(jnp.int32, slot_ref.shape, 1)
    slots = slot_ref[...]
    scol = [jnp.sum(jnp.where(lane == eg * NE + k, slots, 0.0), axis=-1, keepdims=True) for k in range(NE)]
    wide = lax.broadcasted_iota(jnp.int32, (1, NE * R), 1)
    which = [wide // R == k for k in range(NE)]
    within = wide % R
    for sub in range(n_sub):
        tsl = slice(sub * MOE_SUB, (sub + 1) * MOE_SUB)
        tok_slot = jnp.zeros((MOE_SUB, NE * R), F32)
        for k in range(NE):
            tok_slot = jnp.where(which[k], scol[k][tsl], tok_slot)
        lo, hi = zip(*[_slot_blocks(starts_ref, b, eg * NE + k, j * n_sub + sub, R) for k in range(NE)])
        trips = functools.reduce(jnp.maximum, [hi[k] - lo[k] for k in range(NE)])

        def scatter(it, carry):
            target = jnp.full((1, NE * R), -2, jnp.int32)
            rows = []
            for k in range(NE):
                blk = jnp.minimum(lo[k] + it, n_blk - 1)
                target = jnp.where(which[k], jnp.where(lo[k] + it < hi[k], blk * R + within, -2), target)
                rows.append(y_ref[k, pl.ds(pl.multiple_of(blk * R, R), R), :])
            onehot = jnp.where(tok_slot == target.astype(F32), 1.0, 0.0).astype(BF16)
            acc_ref[tsl, :] += jnp.dot(onehot, jnp.concatenate(rows, axis=0), preferred_element_type=F32)
            return carry

        lax.fori_loop(0, trips, scatter, 0)

    @pl.when(eg == pl.num_programs(2) - 1)
    def _():
        o_ref[...] = x1_ref[...] + g2_ref[...] * acc_ref[...]


def moe(h2, routed, x1, mod, wg, wu, wd, layer, cap):
    gate, slot, starts = routed
    B, T, D = x1.shape
    E, F = N_EXPERTS, wg.shape[-1]
    R = min(MOE_SLOTS, cap)
    assert cap % R == 0
    tt = min(2048, T)
    n_sub = tt // MOE_SUB
    row = pl.BlockSpec((None, None, 1, tt), lambda e, b, j, st: (b, e, 0, j))
    wspec = lambda r, c: pl.BlockSpec((None, None, r, c), lambda e, b, j, st: (layer, e, 0, 0))
    y = pl.pallas_call(
        functools.partial(_moe_ffn_kernel, n_sub=n_sub, R=R),
        grid_spec=pltpu.PrefetchScalarGridSpec(
            num_scalar_prefetch=1, grid=(E, B, T // tt),
            in_specs=[pl.BlockSpec((None, tt, D), lambda e, b, j, st: (b, j, 0)), row, row,
                      wspec(D, F), wspec(D, F), wspec(F, D)],
            out_specs=pl.BlockSpec((None, None, cap, D), lambda e, b, j, st: (b, e, 0, 0)),
            scratch_shapes=[pltpu.VMEM((cap, D), F32), pltpu.VMEM((cap, 1), F32)]),
        out_shape=jax.ShapeDtypeStruct((B, E, cap, D), BF16),
        compiler_params=_cp("parallel", "parallel", "arbitrary"), name="moe_ffn",
    )(starts, h2, slot.reshape(B, E, 1, T), gate.reshape(B, E, 1, T), wg, wu, wd)
    tc = min(1024, T)
    ne = 4
    rc = min(2 * LANE // ne, cap)
    tok = lambda n: pl.BlockSpec((None, tc, n), lambda b, j, e, st: (b, j, 0))
    return pl.pallas_call(
        functools.partial(_moe_combine_kernel, n_sub=tc // MOE_SUB, R=rc, NE=ne, n_blk=cap // rc),
        grid_spec=pltpu.PrefetchScalarGridSpec(
            num_scalar_prefetch=1, grid=(B, T // tc, E // ne),
            in_specs=[pl.BlockSpec((None, ne, cap, D), lambda b, j, e, st: (b, e, 0, 0)),
                      tok(E), tok(D), pl.BlockSpec((None, 1, D), lambda b, j, e, st: (b, 0, 5))],
            out_specs=tok(D),
            scratch_shapes=[pltpu.VMEM((tc, D), F32)]),
        out_shape=jax.ShapeDtypeStruct((B, T, D), F32),
        compiler_params=_cp("parallel", "parallel", "arbitrary"), name="moe_combine",
    )(starts, y, jnp.swapaxes(slot, 1, 2), x1, mod)


def _in_cols():
    src = np.full((IN_PAD,), -1, np.int64)
    splits = (MLA_Q_LORA, MLA_KV_LORA, MLA_ROPE, 512, 512, 512, 512, 2 * GDN_HEADS, 2 * GDN_HEADS, POOL_WIDTH)
    o = np.concatenate([[0], np.cumsum(splits)])
    put = lambda name, at, lo, n: src.__setitem__(slice(SEG[name][0] + at, SEG[name][0] + at + n), np.arange(lo, lo + n))
    put("pq", 0, o[0], MLA_Q_LORA)
    put("pkv", 0, o[1], MLA_KV_LORA)
    put("pkr", MLA_NOPE, o[2], MLA_ROPE)
    for name, k in (("gq", 3), ("gk", 4), ("gv", 5), ("gz", 6)):
        put(name, 0, o[k], 512)
    for d in range(2):
        put("gab", d * LANE, o[7] + d * GDN_HEADS, GDN_HEADS)
        put("gab", d * LANE + GDN_HEADS, o[8] + d * GDN_HEADS, GDN_HEADS)
    put("pool", 0, o[9], POOL_WIDTH)
    return src


def _take_cols(w, src, axis):
    pieces, p, n = [], 0, len(src)
    while p < n:
        q = p + 1
        while q < n and ((src[q] < 0 and src[p] < 0) or (src[p] >= 0 and src[q] == src[q - 1] + 1)):
            q += 1
        if src[p] < 0:
            shape = list(w.shape)
            shape[axis] = q - p
            pieces.append(jnp.zeros(shape, w.dtype))
        else:
            pieces.append(lax.slice_in_dim(w, int(src[p]), int(src[p]) + q - p, axis=axis))
        p = q
    return jnp.concatenate(pieces, axis=axis)


def _head_pad_src(per_head, lo, n):
    src = np.full((MLA_HEADS * HEAD_PAD,), -1, np.int64)
    for h in range(MLA_HEADS):
        src[h * HEAD_PAD:h * HEAD_PAD + n] = h * per_head + lo + np.arange(n)
    return src


def _rope_tables(T, rotate):
    cos = np.ones((T, LANE), np.float32)
    sa = np.zeros((T, LANE), np.float32)
    sb = np.zeros((T, LANE), np.float32)
    if rotate:
        n_freq = MLA_ROPE // 4
        inv = ROPE_THETA ** (-np.arange(n_freq, dtype=np.float64) / n_freq)
        pos_r = np.repeat(np.arange(T // GRID_W, dtype=np.float64), GRID_W)
        pos_c = np.tile(np.arange(GRID_W, dtype=np.float64), T // GRID_W)
        for base, pos in ((MLA_NOPE, pos_r), (MLA_NOPE + 2 * n_freq, pos_c)):
            ang = pos[:, None] * inv[None, :]
            c, s = np.cos(ang), np.sin(ang)
            cos[:, base:base + n_freq] = c
            cos[:, base + n_freq:base + 2 * n_freq] = c
            sa[:, base:base + n_freq] = -s
            sb[:, base + n_freq:base + 2 * n_freq] = s
    return jnp.asarray(cos), jnp.asarray(sa), jnp.asarray(sb)


def _lane_vec(vals_by_dir, at):
    v = jnp.zeros((2, LANE), F32).at[:, at:at + GDN_HEADS].set(vals_by_dir)
    return v.reshape(1, 2 * LANE)


def kernel(x, c, ctx, c_ctx, ada_w, ada_b, norm1_g, norm2_g, w_in, mla_q_a_norm, mla_w_uq, mla_kv_a_norm, mla_w_ukv, mla_q_norm, mla_k_norm, gdn_conv_w, gdn_a_log, gdn_dt_bias, gdn_norm_g, pool_w, pool_scale, w_out, moe_router, moe_w_gate, moe_w_up, moe_w_down):
    B, T, D = x.shape
    Tc = ctx.shape[1]
    L = ada_w.shape[0]
    cvec = jnp.concatenate([c, c_ctx[None, :], jnp.zeros((SUBLANE - B - 1, D), F32)], axis=0)
    mod = ada_mod(cvec, ada_w, ada_b)
    rope_lat = _rope_tables(T, True)
    rope_ctx = _rope_tables(Tc, False)
    in_src = _in_cols()
    uq_src = _head_pad_src(MLA_QK, 0, MLA_QK)
    uk_src = _head_pad_src(MLA_NOPE + MLA_V, 0, MLA_NOPE)
    uv_src = _head_pad_src(MLA_NOPE + MLA_V, MLA_NOPE, MLA_V)
    pad_to = lambda v, n: jnp.pad(v, (0, n - v.shape[0])).reshape(1, n)

    xc = ctx
    for l in range(L):
        need_ctx = l < L - 1
        mod_lat = mod[l, :B].reshape(B, 1, ADA_CHUNKS * D)
        mod_ctx = jnp.broadcast_to(mod[l, B].reshape(1, 1, ADA_CHUNKS * D), (B, 1, ADA_CHUNKS * D))
        w_in_p = _take_cols(w_in[l], in_src, 1).astype(BF16)
        prep_w = (
            pad_to(mla_q_a_norm[l], 256),
            jnp.pad(_take_cols(mla_w_uq[l], uq_src, 1), ((0, 256 - MLA_Q_LORA), (0, 0))).astype(BF16),
            mla_kv_a_norm[l].reshape(1, MLA_KV_LORA),
            _take_cols(mla_w_ukv[l], uk_src, 1).astype(BF16),
            _take_cols(mla_w_ukv[l], uv_src, 1).T.astype(BF16),
            pad_to(mla_q_norm[l] * (MLA_QK ** -0.5 * math.log2(math.e)), HEAD_PAD),
            pad_to(mla_k_norm[l], HEAD_PAD),
            gdn_conv_w[l],
            _lane_vec(gdn_a_log[l], 0),
            _lane_vec(gdn_dt_bias[l], 0),
        )
        wo = w_out[l]
        n_att = MLA_HEADS * MLA_V
        n_gdn = GDN_HEADS * GDN_DV
        wbd = jnp.zeros((POOL_WIDTH, POOL_WIDTH), F32)
        for gi in range(len(POOL_WINDOWS)):
            wbd = wbd.at[gi * POOL_GROUP:(gi + 1) * POOL_GROUP, gi * POOL_GROUP:(gi + 1) * POOL_GROUP].set(pool_w[l, gi])
        mix_w = (
            gdn_norm_g[l].reshape(1, GDN_DV),
            wo[:n_att].astype(BF16),
            wo[n_att:n_att + n_gdn].astype(BF16),
            wo[n_att + n_gdn:].astype(BF16),
            wbd.astype(BF16),
            pool_scale[l].reshape(1, POOL_WIDTH),
            norm2_g[l].reshape(1, D),
            moe_router[l].T,
        )

        p_lat = dict(zip(SEG, inproj(x, mod_lat, norm1_g[l], w_in_p)))
        p_ctx = dict(zip(SEG, inproj(xc, mod_ctx, norm1_g[l], w_in_p)))
        a_ctx = prep(p_ctx, prep_w, rope_ctx, T, T + Tc)
        a_lat = prep(p_lat, prep_w, rope_lat, 0, T + Tc, shared=a_ctx)
        att_l = attention(a_lat["Q"], a_lat["K"], a_lat["VT"], 0, T + Tc)
        o_all = gdn_scan(a_lat["q"], a_lat["k"], a_lat["v"], a_lat["gb"], Tc)

        def channel_mix(att, o_off, p, xin, m):
            Tn = xin.shape[1]
            x1, h2, aff = mixout(att, o_all, o_off, p["gz"], p["pool"], xin, m, mix_w)
            cap = EC_CAPACITY_FACTOR * Tn // N_EXPERTS
            return moe(h2, route(aff, cap), x1, m, moe_w_gate, moe_w_up, moe_w_down, l, cap)

        x = channel_mix(att_l, 0, p_lat, x, mod_lat)
        if need_ctx:
            att_c = attention(a_ctx["Q"], a_lat["K"], a_lat["VT"], T, Tc)
            xc = channel_mix(att_c, T, p_ctx, xc, mod_ctx)
    return x
```
